```python
import math
import jax, jax.numpy as jnp
from jax import lax
import numpy as np

D_MODEL = 1024
BATCH = 16
SEQ = 4096
DEPTH = 4

GRID_W = 64
CTX_LEN = 256
EPS = 1e-6
N_MOD = 6

SSD_EXPAND = 2
D_INNER = SSD_EXPAND * D_MODEL
SSD_HEAD_DIM = 64
SSD_HEADS = D_INNER // SSD_HEAD_DIM
SSD_GROUPS = 8
D_STATE = 128
CONV_K = 5
CHUNK = 128
GN = SSD_GROUPS * D_STATE
CONV_CH = D_INNER + 2 * GN
SSD_IN = D_INNER + CONV_CH + 2 * SSD_HEADS

Q_HEADS = 16
KV_HEADS = 4
Q_PER_KV = Q_HEADS // KV_HEADS
HEAD_DIM = 64
Q_W = Q_HEADS * HEAD_DIM
KV_W = KV_HEADS * HEAD_DIM
WINDOW = 128
BLOCK = 128
ROPE_BASE = 10000.0
AXIS_DIM = HEAD_DIM // 2
ROPE_FREQS = AXIS_DIM // 2

D_FF = 2816
N_EXPERTS = 8
TOP_K = 2
D_FF_EXPERT = 3584

N_SSD = (DEPTH + 1) // 2
N_ATTN = DEPTH // 2
N_DENSE = (DEPTH + 1) // 2
N_MOE = DEPTH // 2

kernel_name = 'hybrid_ssd_swa_moe_dit_block'

F32 = jnp.float32


def rmsnorm(x, w):
    xf = x.astype(F32)
    y = xf * lax.rsqrt(jnp.mean(xf * xf, axis=-1, keepdims=True) + EPS)
    return (y * w.astype(F32)).astype(x.dtype)


def modulate(h, shift, scale):
    return h * (1 + scale[:, None, :]) + shift[:, None, :]


def swiglu(h, w1, w3, w2):
    return (jax.nn.silu(h @ w1) * (h @ w3)) @ w2


def moe_swiglu(h, router, w1, w3, w2):
    shp = h.shape
    t = h.reshape(-1, shp[-1])
    logits = (t @ router).astype(F32)
    top_v, top_i = lax.top_k(logits, TOP_K)
    gates = jax.nn.softmax(top_v, axis=-1)
    combine = jnp.einsum('tk,tke->te', gates, jax.nn.one_hot(top_i, N_EXPERTS, dtype=F32)).astype(h.dtype)
    out = jnp.zeros_like(t)
    for e in range(N_EXPERTS):
        out = out + combine[:, e:e + 1] * swiglu(t, w1[e], w3[e], w2[e])
    return out.reshape(shp)


def depthwise_conv_same(x, w, b):
    pad = CONV_K // 2
    n = x.shape[1]
    xp = jnp.pad(x, ((0, 0), (pad, pad), (0, 0)))
    return sum(xp[:, k:k + n] * w[k] for k in range(CONV_K)) + b


def ssd_chunked(xs, dt, a_diag, bm, cm):
    b, t, h, p = xs.shape
    g, n = bm.shape[2], bm.shape[3]
    r = h // g
    nc = t // CHUNK

    def chunks(a):
        return jnp.moveaxis(a.astype(F32).reshape((b, nc, CHUNK) + a.shape[2:]), 1, 0)

    xc = chunks(xs.reshape(b, t, g, r, p))
    dtc = chunks(dt.reshape(b, t, g, r))
    ac = dtc * a_diag.astype(F32).reshape(g, r)
    bc = chunks(bm)
    cc = chunks(cm)
    lower = jnp.tril(jnp.ones((CHUNK, CHUNK), dtype=bool))[None, :, :, None, None]

    def step(state, inp):
        x_, dt_, a_, b_, c_ = inp
        acum = jnp.cumsum(a_, axis=1)
        seg = acum[:, :, None] - acum[:, None, :]
        decay = jnp.exp(jnp.where(lower, seg, -jnp.inf))
        cb = jnp.einsum('bign,bjgn->bijg', c_, b_)
        y_intra = jnp.einsum('bijg,bijgr,bjgr,bjgrp->bigrp', cb, decay, dt_, x_)
        y_state = jnp.einsum('bign,bgrpn->bigrp', c_, state) * jnp.exp(acum)[..., None]
        tail = jnp.exp(acum[:, -1:] - acum) * dt_
        new_state = state * jnp.exp(acum[:, -1])[..., None, None] + jnp.einsum('bjgn,bjgr,bjgrp->bgrpn', b_, tail, x_)
        return new_state, y_intra + y_state

    state0 = jnp.zeros((b, g, r, p, n), F32)
    _, y = lax.scan(step, state0, (xc, dtc, ac, bc, cc))
    return jnp.moveaxis(y, 0, 1).reshape(b, t, h, p)


def gated_rmsnorm(y, z, w):
    g = (y * jax.nn.silu(z)).astype(F32)
    shp = g.shape
    g = g.reshape(shp[:-1] + (SSD_GROUPS, D_INNER // SSD_GROUPS))
    g = g * lax.rsqrt(jnp.mean(g * g, axis=-1, keepdims=True) + EPS)
    return (g.reshape(shp) * w.astype(F32)).astype(y.dtype)


def ssd_mixer(hx, hc, w_in, conv_w, conv_b, dt_bias, a_log, d_skip, norm_w, w_out):
    bsz, n_lat, _ = hx.shape
    n_ctx = hc.shape[1]

    def project(h):
        zxbcdt = h @ w_in
        z = zxbcdt[..., :D_INNER]
        xbc = jax.nn.silu(depthwise_conv_same(zxbcdt[..., D_INNER:D_INNER + CONV_CH], conv_w, conv_b))
        dt = zxbcdt[..., D_INNER + CONV_CH:]
        return z, xbc, dt

    zc, xbcc, dtc = project(hc)
    zx, xbcx, dtx = project(hx)
    z = jnp.concatenate([zc, zx], axis=1)
    xbc = jnp.concatenate([xbcc, xbcx], axis=1)
    dt_raw = jnp.concatenate([dtc, dtx], axis=1)
    total = n_ctx + n_lat
    xs = xbc[..., :D_INNER].reshape(bsz, total, SSD_HEADS, SSD_HEAD_DIM)
    bm = xbc[..., D_INNER:D_INNER + GN].reshape(bsz, total, SSD_GROUPS, D_STATE)
    cm = xbc[..., D_INNER + GN:].reshape(bsz, total, SSD_GROUPS, D_STATE)

    def reverse_segments(t):
        return jnp.concatenate([jnp.flip(t[:, :n_ctx], axis=1), jnp.flip(t[:, n_ctx:], axis=1)], axis=1)

    def direction(d, xs_d, bm_d, cm_d, dt_raw_d):
        dt_d = jax.nn.softplus((dt_raw_d[..., d * SSD_HEADS:(d + 1) * SSD_HEADS] + dt_bias[d]).astype(F32))
        a_d = -jnp.exp(a_log[d].astype(F32))
        y_d = ssd_chunked(xs_d, dt_d, a_d, bm_d, cm_d)
        return y_d + d_skip[d].astype(F32)[:, None] * xs_d.astype(F32)

    y_fwd = direction(0, xs, bm, cm, dt_raw)
    y_bwd = reverse_segments(direction(1, reverse_segments(xs), reverse_segments(bm),
                                       reverse_segments(cm), reverse_segments(dt_raw)))
    y = (y_fwd + y_bwd).reshape(bsz, total, D_INNER).astype(hx.dtype)
    out = gated_rmsnorm(y, z, norm_w) @ w_out
    return out[:, n_ctx:], out[:, :n_ctx]


def axial_rope(x, cos, sin):
    shp = x.shape
    xr = x.astype(F32).reshape(shp[:-1] + (2, 2, ROPE_FREQS))
    bshape = (1, shp[1]) + (1,) * (len(shp) - 3) + (2, 1, ROPE_FREQS)
    c = cos.reshape(bshape)
    s = sin.reshape(bshape)
    rot = jnp.stack([-xr[..., 1, :], xr[..., 0, :]], axis=-2)
    return (xr * c + rot * s).reshape(shp).astype(x.dtype)


def sink_attention(q, ks, vs, masks, sink):
    scores = []
    for k, m in zip(ks, masks):
        s = jnp.einsum('bqgrd,bkgd->bgrqk', q, k).astype(F32)
        if m is not None:
            s = jnp.where(m, s, -jnp.inf)
        scores.append(s)
    sink_col = jnp.broadcast_to(sink.astype(F32)[None, :, :, None, None], scores[0].shape[:-1] + (1,))
    p = jax.nn.softmax(jnp.concatenate(scores + [sink_col], axis=-1), axis=-1)
    out = 0
    off = 0
    for k, v in zip(ks, vs):
        n = k.shape[1]
        out = out + jnp.einsum('bgrqk,bkgd->bqgrd', p[..., off:off + n].astype(v.dtype), v)
        off += n
    return out


def window_attention(hx, hc, w_qkv, b_qkv, sink, w_o, b_o, cos, sin, need_ctx):
    bsz, n_lat, _ = hx.shape
    n_ctx = hc.shape[1]
    n_blk = n_lat // BLOCK
    scale = 1.0 / math.sqrt(HEAD_DIM)

    def project(h):
        t = h @ w_qkv + b_qkv
        l = h.shape[1]
        q = t[..., :Q_W].reshape(bsz, l, KV_HEADS, Q_PER_KV, HEAD_DIM) * scale
        k = t[..., Q_W:Q_W + KV_W].reshape(bsz, l, KV_HEADS, HEAD_DIM)
        v = t[..., Q_W + KV_W:].reshape(bsz, l, KV_HEADS, HEAD_DIM)
        return q, k, v

    qx, kx, vx = project(hx)
    qc, kc, vc = project(hc)
    qx = axial_rope(qx, cos, sin)
    kx = axial_rope(kx, cos, sin)
    pad = ((0, 0), (BLOCK, BLOCK), (0, 0), (0, 0))
    kp = jnp.pad(kx, pad)
    vp = jnp.pad(vx, pad)
    sink_gr = sink.reshape(KV_HEADS, Q_PER_KV)

    def latent_block(n):
        start = n * BLOCK
        qb = lax.dynamic_slice_in_dim(qx, start, BLOCK, axis=1)
        kb = lax.dynamic_slice_in_dim(kp, start, 3 * BLOCK, axis=1)
        vb = lax.dynamic_slice_in_dim(vp, start, 3 * BLOCK, axis=1)
        qpos = start + jnp.arange(BLOCK)
        kpos = start - BLOCK + jnp.arange(3 * BLOCK)
        mask = ((jnp.abs(qpos[:, None] - kpos[None, :]) <= WINDOW)
                & (kpos >= 0)[None, :] & (kpos < n_lat)[None, :])
        return sink_attention(qb, (kb, kc), (vb, vc), (mask, None), sink_gr)

    o = lax.map(latent_block, jnp.arange(n_blk))
    o = jnp.moveaxis(o, 0, 1).reshape(bsz, n_lat, Q_W)
    yx = o @ w_o + b_o
    if not need_ctx:
        return yx, None
    oc = sink_attention(qc, (kc,), (vc,), (None,), sink_gr).reshape(bsz, n_ctx, Q_W)
    return yx, oc @ w_o + b_o


def setup_inputs(seed: int = 0) -> dict:
    key = jax.random.key(seed)
    ks = iter(jax.random.split(key, 40))

    def nrm(shape, scale):
        return jax.random.normal(next(ks), shape, F32) * scale

    u = jax.random.uniform(next(ks), (N_SSD, 2, SSD_HEADS), F32)
    dt0 = jnp.exp(u * (math.log(0.1) - math.log(1e-3)) + math.log(1e-3))
    dt_bias = dt0 + jnp.log(-jnp.expm1(-dt0))
    a_log = jnp.log(jax.random.uniform(next(ks), (N_SSD, 2, SSD_HEADS), F32, minval=1.0, maxval=16.0))
    return {
        'x': nrm((BATCH, SEQ, D_MODEL), 1.0),
        'c': nrm((BATCH, D_MODEL), 1.0),
        'ctx': nrm((BATCH, CTX_LEN, D_MODEL), 1.0),
        'c_ctx': nrm((D_MODEL,), 1.0),
        'w_mod': nrm((DEPTH, D_MODEL, N_MOD * D_MODEL), 0.5 * D_MODEL ** -0.5),
        'b_mod': nrm((DEPTH, N_MOD * D_MODEL), 0.02),
        'norm1_w': 1.0 + nrm((DEPTH, D_MODEL), 0.02),
        'norm2_w': 1.0 + nrm((DEPTH, D_MODEL), 0.02),
        'ssd_w_in': nrm((N_SSD, D_MODEL, SSD_IN), D_MODEL ** -0.5),
        'ssd_conv_w': nrm((N_SSD, CONV_K, CONV_CH), CONV_K ** -0.5),
        'ssd_conv_b': nrm((N_SSD, CONV_CH), 0.02),
        'ssd_dt_bias': dt_bias,
        'ssd_a_log': a_log,
        'ssd_d': 1.0 + nrm((N_SSD, 2, SSD_HEADS), 0.1),
        'ssd_norm_w': 1.0 + nrm((N_SSD, D_INNER), 0.02),
        'ssd_w_out': nrm((N_SSD, D_INNER, D_MODEL), D_INNER ** -0.5),
        'attn_w_qkv': nrm((N_ATTN, D_MODEL, Q_W + 2 * KV_W), D_MODEL ** -0.5),
        'attn_b_qkv': nrm((N_ATTN, Q_W + 2 * KV_W), 0.02),
        'attn_sink': nrm((N_ATTN, Q_HEADS), 0.5),
        'attn_w_o': nrm((N_ATTN, Q_W, D_MODEL), Q_W ** -0.5),
        'attn_b_o': nrm((N_ATTN, D_MODEL), 0.02),
        'ffn_w1': nrm((N_DENSE, D_MODEL, D_FF), D_MODEL ** -0.5),
        'ffn_w3': nrm((N_DENSE, D_MODEL, D_FF), D_MODEL ** -0.5),
        'ffn_w2': nrm((N_DENSE, D_FF, D_MODEL), D_FF ** -0.5),
        'moe_router': nrm((N_MOE, D_MODEL, N_EXPERTS), D_MODEL ** -0.5),
        'moe_w1': nrm((N_MOE, N_EXPERTS, D_MODEL, D_FF_EXPERT), D_MODEL ** -0.5),
        'moe_w3': nrm((N_MOE, N_EXPERTS, D_MODEL, D_FF_EXPERT), D_MODEL ** -0.5),
        'moe_w2': nrm((N_MOE, N_EXPERTS, D_FF_EXPERT, D_MODEL), D_FF_EXPERT ** -0.5),
        'final_norm_w': 1.0 + nrm((D_MODEL,), 0.02),
    }


def reference(x, c, ctx, c_ctx, w_mod, b_mod, norm1_w, norm2_w, ssd_w_in, ssd_conv_w, ssd_conv_b,
              ssd_dt_bias, ssd_a_log, ssd_d, ssd_norm_w, ssd_w_out, attn_w_qkv, attn_b_qkv, attn_sink,
              attn_w_o, attn_b_o, ffn_w1, ffn_w3, ffn_w2, moe_router, moe_w1, moe_w3, moe_w2, final_norm_w):
    n_lat = x.shape[1]
    rows = n_lat // GRID_W
    row_ids = jnp.repeat(jnp.arange(rows), GRID_W).astype(F32)
    col_ids = jnp.tile(jnp.arange(GRID_W), rows).astype(F32)
    inv_freq = ROPE_BASE ** (-jnp.arange(ROPE_FREQS, dtype=F32) * 2.0 / AXIS_DIM)
    ang = jnp.stack([row_ids[:, None] * inv_freq, col_ids[:, None] * inv_freq], axis=1)[:, :, None, :]
    cos = jnp.cos(ang)
    sin = jnp.sin(ang)

    cx = ctx
    for i in range(DEPTH):
        last = i == DEPTH - 1
        j = i // 2
        mod_x = (jax.nn.silu(c) @ w_mod[i] + b_mod[i]).reshape(-1, N_MOD, D_MODEL)
        mod_c = (jax.nn.silu(c_ctx[None]) @ w_mod[i] + b_mod[i]).reshape(1, N_MOD, D_MODEL)

        hx = modulate(rmsnorm(x, norm1_w[i]), mod_x[:, 0], mod_x[:, 1])
        hc = modulate(rmsnorm(cx, norm1_w[i]), mod_c[:, 0], mod_c[:, 1])
        if i % 2 == 0:
            yx, yc = ssd_mixer(hx, hc, ssd_w_in[j], ssd_conv_w[j], ssd_conv_b[j], ssd_dt_bias[j],
                               ssd_a_log[j], ssd_d[j], ssd_norm_w[j], ssd_w_out[j])
        else:
            yx, yc = window_attention(hx, hc, attn_w_qkv[j], attn_b_qkv[j], attn_sink[j], attn_w_o[j],
                                      attn_b_o[j], cos, sin, not last)
        x = x + mod_x[:, 2][:, None] * yx

        hx = modulate(rmsnorm(x, norm2_w[i]), mod_x[:, 3], mod_x[:, 4])
        if last:
            h = hx
            n_ctx = 0
        else:
            cx = cx + mod_c[:, 2][:, None] * yc
            hc = modulate(rmsnorm(cx, norm2_w[i]), mod_c[:, 3], mod_c[:, 4])
            h = jnp.concatenate([hc, hx], axis=1)
            n_ctx = cx.shape[1]
        if i % 2 == 0:
            f = swiglu(h, ffn_w1[j], ffn_w3[j], ffn_w2[j])
        else:
            f = moe_swiglu(h, moe_router[j], moe_w1[j], moe_w3[j], moe_w2[j])
        x = x + mod_x[:, 5][:, None] * f[:, n_ctx:]
        if not last:
            cx = cx + mod_c[:, 5][:, None] * f[:, :n_ctx]
    return rmsnorm(x, final_norm_w)
```

```python
import functools
import math

import jax
import jax.numpy as jnp
from jax import lax
from jax.experimental import pallas as pl
from jax.experimental.pallas import tpu as pltpu

F32 = jnp.float32
BF16 = jnp.bfloat16
HIGHEST = lax.Precision.HIGHEST

EPS = 1e-6
N_MOD = 6
GRID_W = 64
ROPE_BASE = 10000.0

SSD_HEAD_DIM = 64
SSD_GROUPS = 8
D_STATE = 128
CONV_K = 5
CHUNK = 128
HEADS_PER_GROUP = 4
GROUP_W = HEADS_PER_GROUP * SSD_HEAD_DIM

Q_HEADS = 16
KV_HEADS = 4
HEAD_DIM = 64
BLOCK = 128

N_EXPERTS = 8
TOP_K = 2

ROW_TILE = 256
EXPERT_TILE = 256
LANES = 128
PACK_ROWS_BF16 = 16

MIB = 1024 * 1024


def _cparams(n_axes, vmem_mib):
    return pltpu.CompilerParams(dimension_semantics=("arbitrary",) * n_axes,
                                vmem_limit_bytes=vmem_mib * MIB)


def _resident(block_shape, index_map):
    return pl.BlockSpec(block_shape, index_map, pipeline_mode=pl.Buffered(1))


def _sigmoid(v):
    return 1.0 / (1.0 + jnp.exp(-v))


def _silu(v):
    return v * _sigmoid(v)


def _norm_mod(x, nw, shift, scale):
    ms = jnp.mean(x * x, axis=-1, keepdims=True)
    y = x * lax.rsqrt(ms + EPS) * nw
    return y * (1.0 + scale) + shift


class _Tiles:
    def __init__(self, batch, t_total, n_ctx, lo_tiles):
        assert t_total % ROW_TILE == 0 and n_ctx % ROW_TILE == 0
        self.batch = batch
        self.tpb = t_total // ROW_TILE
        self.ctx_tiles = n_ctx // ROW_TILE
        self.lo = lo_tiles
        self.n_w = self.tpb - lo_tiles
        self.grid = batch * self.n_w

    def split(self, i):
        return i // self.n_w, self.lo + i % self.n_w

    def row(self, i):
        b, w = self.split(i)
        return b * self.tpb + w

    def mod_row(self, i):
        b, w = self.split(i)
        return jnp.where(w < self.ctx_tiles, self.batch, b)

    def x_spec(self, width):
        return pl.BlockSpec((ROW_TILE, width), lambda i: (self.row(i), 0))

    def mod_spec(self, d):
        return pl.BlockSpec((1, N_MOD, d), lambda i: (self.mod_row(i), 0, 0))


def _mod_kernel(c_ref, w_ref, b_ref, o_ref):
    s = _silu(c_ref[...])
    o_ref[0] = jnp.dot(s, w_ref[0], preferred_element_type=F32, precision=HIGHEST) + b_ref[0]


def _mod_all(c, c_ctx, w_mod, b_mod):
    depth, d, n = w_mod.shape
    cc = jnp.concatenate([c, c_ctx[None]], axis=0)
    rows = cc.shape[0]
    tn = 1536
    assert n % tn == 0
    out = pl.pallas_call(
        _mod_kernel,
        out_shape=jax.ShapeDtypeStruct((depth, rows, n), F32),
        grid=(depth, n // tn),
        in_specs=[pl.BlockSpec((rows, d), lambda i, j: (0, 0)),
                  pl.BlockSpec((1, d, tn), lambda i, j: (i, 0, j)),
                  pl.BlockSpec((1, 1, tn), lambda i, j: (i, 0, j))],
        out_specs=pl.BlockSpec((1, rows, tn), lambda i, j: (i, 0, j)),
        compiler_params=_cparams(2, 40),
        name="mod_all",
    )(cc, w_mod, b_mod.reshape(depth, 1, n))
    return out.reshape(depth, rows, N_MOD, d)


def _ssd_inproj_kernel(x_ref, mod_ref, nw_ref, wzx_ref, wdt_ref, zx_ref, dt_ref, *, n_chunk):
    h = _norm_mod(x_ref[...], nw_ref[...], mod_ref[0, 0:1, :], mod_ref[0, 1:2, :]).astype(BF16)
    n = wzx_ref.shape[1]
    for n0 in range(0, n, n_chunk):
        zx_ref[:, n0:n0 + n_chunk] = jnp.dot(
            h, wzx_ref[:, n0:n0 + n_chunk], preferred_element_type=F32).astype(BF16)
    dt_ref[...] = jnp.dot(h, wdt_ref[...], preferred_element_type=F32)


def _ssd_inproj(xs, mods, nw, wzx, wdt, tiles):
    m, d = xs.shape
    n = wzx.shape[1]
    return pl.pallas_call(
        functools.partial(_ssd_inproj_kernel, n_chunk=1536),
        out_shape=(jax.ShapeDtypeStruct((m, n), BF16), jax.ShapeDtypeStruct((m, LANES), F32)),
        grid=(tiles.grid,),
        in_specs=[tiles.x_spec(d), tiles.mod_spec(d),
                  _resident((1, d), lambda i: (0, 0)),
                  _resident((d, n), lambda i: (0, 0)),
                  _resident((d, LANES), lambda i: (0, 0))],
        out_specs=(tiles.x_spec(n), tiles.x_spec(LANES)),
        compiler_params=_cparams(1, 48),
        name="ssd_inproj",
    )(xs, mods, nw, wzx, wdt)


def _dt_prep_kernel(raw_ref, bias_ref, a_ref, dtc_ref, acc_ref, dtr_ref, acr_ref):
    v = raw_ref[...] + bias_ref[...]
    dt = jnp.maximum(v, 0.0) + jnp.log(1.0 + jnp.exp(-jnp.abs(v)))
    a = dt * a_ref[...]
    ii = lax.broadcasted_iota(jnp.int32, (CHUNK, CHUNK), 0)
    jj = lax.broadcasted_iota(jnp.int32, (CHUNK, CHUNK), 1)
    prefix = jnp.dot((ii >= jj).astype(F32), a, preferred_element_type=F32, precision=HIGHEST)
    suffix = jnp.dot((ii <= jj).astype(F32), a, preferred_element_type=F32, precision=HIGHEST)
    col = lax.broadcasted_iota(jnp.int32, (CHUNK, LANES), 1)
    is_fwd = (col % (2 * HEADS_PER_GROUP)) < HEADS_PER_GROUP
    ac = jnp.where(is_fwd, prefix, suffix)
    dtc_ref[...] = dt
    acc_ref[...] = ac
    dtr_ref[0] = dt.T
    acr_ref[0] = ac.T


def _dt_prep(dt_raw, bias, a_neg):
    m = dt_raw.shape[0]
    nchunks = m // CHUNK
    tile = pl.BlockSpec((CHUNK, LANES), lambda i: (i, 0))
    vec = pl.BlockSpec((1, LANES), lambda i: (0, 0))
    sq = pl.BlockSpec((1, LANES, CHUNK), lambda i: (i, 0, 0))
    return pl.pallas_call(
        _dt_prep_kernel,
        out_shape=(jax.ShapeDtypeStruct((m, LANES), F32), jax.ShapeDtypeStruct((m, LANES), F32),
                   jax.ShapeDtypeStruct((nchunks, LANES, CHUNK), F32),
                   jax.ShapeDtypeStruct((nchunks, LANES, CHUNK), F32)),
        grid=(nchunks,),
        in_specs=[tile, vec, vec],
        out_specs=(tile, tile, sq, sq),
        compiler_params=_cparams(1, 32),
        name="ssd_dt_prep",
    )(dt_raw, bias, a_neg)


def _expand_heads(cols):
    rows = max(c.shape[0] for c in cols)
    lane = lax.broadcasted_iota(jnp.int32, (rows, GROUP_W), 1)
    out = cols[3]
    for r in (2, 1, 0):
        out = jnp.where(lane < (r + 1) * SSD_HEAD_DIM, cols[r], out)
    return out


def _ssd_scan_kernel(z_ref, x_ref, b_ref, c_ref, cw_ref, dtc_ref, acc_ref, dtr_ref, acr_ref,
                     dsk_ref, nw_ref, o_ref, xc_s, cc_s, bt_s, y_s, st_s, *, n_chunks, ctx_chunks):
    t_total = n_chunks * CHUNK
    halo = PACK_ROWS_BF16
    win = CHUNK + 2 * halo
    pad = CONV_K // 2

    def conv_silu(ref, c, t0, col0, width):
        cur = ref[0, pl.ds(t0, CHUNK), :]
        prev = ref[0, pl.ds(pl.multiple_of(jnp.maximum(t0 - halo, 0), halo), halo), :]
        nxt = ref[0, pl.ds(pl.multiple_of(jnp.minimum(t0 + CHUNK, t_total - halo), halo), halo), :]
        prev_ok = jnp.logical_and(c != 0, c != ctx_chunks)
        next_ok = jnp.logical_and(c != ctx_chunks - 1, c != n_chunks - 1)
        prev = jnp.where(prev_ok, prev, jnp.zeros_like(prev))
        nxt = jnp.where(next_ok, nxt, jnp.zeros_like(nxt))
        w = jnp.concatenate([prev, cur, nxt], axis=0).astype(F32)
        acc = jnp.broadcast_to(cw_ref[0, CONV_K:CONV_K + 1, col0:col0 + width], (CHUNK, width))
        for k in range(CONV_K):
            shifted = w if k == pad else pltpu.roll(w, (pad - k) % win, 0)
            acc = acc + shifted[halo:halo + CHUNK] * cw_ref[0, k:k + 1, col0:col0 + width]
        return _silu(acc)

    def conv_body(c, carry):
        t0 = pl.multiple_of(c * CHUNK, CHUNK)
        xc_s[pl.ds(t0, CHUNK), :] = conv_silu(x_ref, c, t0, 0, GROUP_W).astype(BF16)
        cc_s[pl.ds(t0, CHUNK), :] = conv_silu(c_ref, c, t0, GROUP_W + D_STATE, D_STATE).astype(BF16)
        bt_s[c] = conv_silu(b_ref, c, t0, GROUP_W, D_STATE).T.astype(BF16)
        return carry

    lax.fori_loop(0, n_chunks, conv_body, 0)

    ii = lax.broadcasted_iota(jnp.int32, (CHUNK, CHUNK), 0)
    jj = lax.broadcasted_iota(jnp.int32, (CHUNK, CHUNK), 1)
    lane_w = lax.broadcasted_iota(jnp.int32, (CHUNK, GROUP_W), 1)

    for d in range(2):
        mask = (ii >= jj) if d == 0 else (ii <= jj)
        last = CHUNK - 1 if d == 0 else 0
        st_s[...] = jnp.zeros_like(st_s)

        def scan_body(s, carry, d=d, mask=mask, last=last):
            if d == 0:
                c = s
            else:
                c = jnp.where(s < ctx_chunks, ctx_chunks - 1 - s, n_chunks - 1 - (s - ctx_chunks))
            t0 = pl.multiple_of(c * CHUNK, CHUNK)
            xch = xc_s[pl.ds(t0, CHUNK), :]
            cch = cc_s[pl.ds(t0, CHUNK), :]
            bt = bt_s[c]
            dtc = dtc_ref[0, 0, pl.ds(t0, CHUNK), :]
            acc = acc_ref[0, 0, pl.ds(t0, CHUNK), :]
            dtr = dtr_ref[0, c]
            acr = acr_ref[0, c]
            state = st_s[...]

            cb = jnp.dot(cch, bt, preferred_element_type=F32)
            y = jnp.dot(cch, state.astype(BF16), preferred_element_type=F32)
            y = y * _expand_heads([jnp.exp(acc[:, 4 * d + r:4 * d + r + 1]) for r in range(4)])
            for r in range(HEADS_PER_GROUP):
                k = 4 * d + r
                seg = acc[:, k:k + 1] - acr[k:k + 1, :]
                decay = jnp.exp(jnp.where(mask, seg, -jnp.inf))
                m = (cb * decay * dtr[k:k + 1, :]).astype(BF16)
                in_head = jnp.logical_and(lane_w >= r * SSD_HEAD_DIM, lane_w < (r + 1) * SSD_HEAD_DIM)
                xr = jnp.where(in_head, xch, jnp.zeros_like(xch))
                y = y + jnp.dot(m, xr, preferred_element_type=F32)
            if d == 0:
                y_s[pl.ds(t0, CHUNK), :] = y
            else:
                y_s[pl.ds(t0, CHUNK), :] = y_s[pl.ds(t0, CHUNK), :] + y

            a_last = [acr[4 * d + r:4 * d + r + 1, last:last + 1] for r in range(4)]
            tail = _expand_heads([jnp.exp(a_last[r] - acc[:, 4 * d + r:4 * d + r + 1])
                                  * dtc[:, 4 * d + r:4 * d + r + 1] for r in range(4)])
            xw = (xch.astype(F32) * tail).astype(BF16)
            carry_decay = _expand_heads([jnp.exp(a_last[r]) for r in range(4)])
            st_s[...] = state * carry_decay + jnp.dot(bt, xw, preferred_element_type=F32)
            return carry

        lax.fori_loop(0, n_chunks, scan_body, 0)

    def out_body(c, carry):
        t0 = pl.multiple_of(c * CHUNK, CHUNK)
        y = y_s[pl.ds(t0, CHUNK), :] + dsk_ref[0] * xc_s[pl.ds(t0, CHUNK), :].astype(F32)
        g = y * _silu(z_ref[0, pl.ds(t0, CHUNK), :].astype(F32))
        g = g * lax.rsqrt(jnp.mean(g * g, axis=-1, keepdims=True) + EPS)
        o_ref[0, pl.ds(t0, CHUNK), :] = (g * nw_ref[0]).astype(BF16)
        return carry

    lax.fori_loop(0, n_chunks, out_body, 0)


def _ssd_scan(zx, cw, dtc, acc, dtr, acr, dskip, norm_w, batch, t_total, n_ctx):
    d_inner = SSD_GROUPS * GROUP_W
    n_chunks = t_total // CHUNK
    zx3 = zx.reshape(batch, t_total, zx.shape[-1])
    xoff = d_inner // GROUP_W
    boff = 2 * d_inner // D_STATE
    coff = boff + SSD_GROUPS
    pc = CONV_K + 3
    small = pl.BlockSpec((1, 1, t_total, 2 * HEADS_PER_GROUP), lambda b, g: (b, g, 0, 0))
    rows = pl.BlockSpec((1, n_chunks, 2 * HEADS_PER_GROUP, CHUNK), lambda b, g: (b, 0, g, 0))
    vec = pl.BlockSpec((1, 1, GROUP_W), lambda b, g: (g, 0, 0))
    return pl.pallas_call(
        functools.partial(_ssd_scan_kernel, n_chunks=n_chunks, ctx_chunks=n_ctx // CHUNK),
        out_shape=jax.ShapeDtypeStruct((batch, t_total, d_inner), BF16),
        grid=(batch, SSD_GROUPS),
        in_specs=[pl.BlockSpec((1, t_total, GROUP_W), lambda b, g: (b, 0, g)),
                  pl.BlockSpec((1, t_total, GROUP_W), lambda b, g: (b, 0, xoff + g)),
                  pl.BlockSpec((1, t_total, D_STATE), lambda b, g: (b, 0, boff + g)),
                  pl.BlockSpec((1, t_total, D_STATE), lambda b, g: (b, 0, coff + g)),
                  pl.BlockSpec((1, pc, GROUP_W + 2 * D_STATE), lambda b, g: (g, 0, 0)),
                  small, small, rows, rows, vec, vec],
        out_specs=pl.BlockSpec((1, t_total, GROUP_W), lambda b, g: (b, 0, g)),
        scratch_shapes=[pltpu.VMEM((t_total, GROUP_W), BF16),
                        pltpu.VMEM((t_total, D_STATE), BF16),
                        pltpu.VMEM((n_chunks, D_STATE, CHUNK), BF16),
                        pltpu.VMEM((t_total, GROUP_W), F32),
                        pltpu.VMEM((D_STATE, GROUP_W), F32)],
        compiler_params=_cparams(2, 56),
        name="ssd_scan",
    )(zx3, zx3, zx3, zx3, cw, dtc, acc, dtr, acr, dskip, norm_w)


def _ssd_mixer(xs, mods, nw, w_in, conv_w, conv_b, dt_bias, a_log, d_skip, norm_w, w_out,
               tiles, batch, t_total, n_ctx):
    d = xs.shape[1]
    d_inner = SSD_GROUPS * GROUP_W
    conv_ch = d_inner + 2 * SSD_GROUPS * D_STATE
    heads = SSD_GROUPS * HEADS_PER_GROUP
    n_dt = 2 * heads
    perm = jnp.arange(n_dt).reshape(2, SSD_GROUPS, HEADS_PER_GROUP).transpose(1, 0, 2).reshape(-1)
    wzx = w_in[:, :d_inner + conv_ch].astype(BF16)
    wdt = jnp.zeros((d, LANES), F32).at[:, :n_dt].set(w_in[:, d_inner + conv_ch:][:, perm]).astype(BF16)
    bias = jnp.zeros((1, LANES), F32).at[0, :n_dt].set(dt_bias.reshape(-1)[perm])
    a_neg = jnp.zeros((1, LANES), F32).at[0, :n_dt].set(-jnp.exp(a_log.astype(F32)).reshape(-1)[perm])

    zx, dt_raw = _ssd_inproj(xs, mods, nw, wzx, wdt, tiles)
    dtc, acc, dtr, acr = _dt_prep(dt_raw, bias, a_neg)

    def cols(a):
        return a[:, :n_dt].reshape(batch, t_total, SSD_GROUPS, 2 * HEADS_PER_GROUP).transpose(0, 2, 1, 3)

    n_chunks = t_total // CHUNK
    dtr = dtr.reshape(batch, n_chunks, LANES, CHUNK)
    acr = acr.reshape(batch, n_chunks, LANES, CHUNK)

    def per_group(v):
        gx = v[:, :d_inner].reshape(-1, SSD_GROUPS, GROUP_W)
        gb = v[:, d_inner:d_inner + SSD_GROUPS * D_STATE].reshape(-1, SSD_GROUPS, D_STATE)
        gc = v[:, d_inner + SSD_GROUPS * D_STATE:].reshape(-1, SSD_GROUPS, D_STATE)
        return jnp.concatenate([gx, gb, gc], axis=-1).transpose(1, 0, 2)

    cw = per_group(jnp.concatenate([conv_w, conv_b[None], jnp.zeros((2, conv_ch), F32)], axis=0))
    dsk = jnp.repeat((d_skip[0] + d_skip[1]).astype(F32), SSD_HEAD_DIM).reshape(SSD_GROUPS, 1, GROUP_W)
    gnw = norm_w.astype(F32).reshape(SSD_GROUPS, 1, GROUP_W)

    g = _ssd_scan(zx, cw, cols(dtc), cols(acc), dtr, acr, dsk, gnw, batch, t_total, n_ctx)
    return _proj_residual(xs, g.reshape(batch * t_total, d_inner), w_out.astype(BF16),
                          jnp.zeros((1, d), F32), mods, 2, tiles)


def _proj_residual_kernel(x_ref, a_ref, w_ref, b_ref, mod_ref, o_ref, *, gate_row):
    y = jnp.dot(a_ref[...], w_ref[...], preferred_element_type=F32) + b_ref[...]
    o_ref[...] = x_ref[...] + mod_ref[0, gate_row:gate_row + 1, :] * y


def _proj_residual(xs, a, w, b, mods, gate_row, tiles):
    m, d = xs.shape
    k = a.shape[1]
    return pl.pallas_call(
        functools.partial(_proj_residual_kernel, gate_row=gate_row),
        out_shape=jax.ShapeDtypeStruct((m, d), F32),
        grid=(tiles.grid,),
        in_specs=[tiles.x_spec(d), tiles.x_spec(k),
                  _resident((k, d), lambda i: (0, 0)),
                  _resident((1, d), lambda i: (0, 0)),
                  tiles.mod_spec(d)],
        out_specs=tiles.x_spec(d),
        input_output_aliases={0: 0},
        compiler_params=_cparams(1, 40),
        name="proj_residual",
    )(xs, a, w, b, mods)


def _qkv_rope_kernel(x_ref, mod_ref, nw_ref, w_ref, b_ref, cos_ref, sa_ref, sb_ref, o_ref, *, n_rope):
    h = _norm_mod(x_ref[...], nw_ref[...], mod_ref[0, 0:1, :], mod_ref[0, 1:2, :]).astype(BF16)
    acc = jnp.dot(h, w_ref[...], preferred_element_type=F32) + b_ref[...]
    cos = cos_ref[...]
    sa = sa_ref[...]
    sb = sb_ref[...]
    half = HEAD_DIM // 4
    for cb in range(n_rope // LANES):
        blk = acc[:, cb * LANES:(cb + 1) * LANES]
        rot = blk * cos + pltpu.roll(blk, half, 1) * sa + pltpu.roll(blk, LANES - half, 1) * sb
        o_ref[:, cb * LANES:(cb + 1) * LANES] = rot.astype(BF16)
    o_ref[:, n_rope:] = acc[:, n_rope:].astype(BF16)


def _qkv_rope(xs, mods, nw, w, b, cos, sa, sb, tiles):
    m, d = xs.shape
    n = w.shape[1]
    n_rope = (Q_HEADS + KV_HEADS) * HEAD_DIM
    tab = pl.BlockSpec((ROW_TILE, LANES), lambda i: (tiles.split(i)[1], 0))
    return pl.pallas_call(
        functools.partial(_qkv_rope_kernel, n_rope=n_rope),
        out_shape=jax.ShapeDtypeStruct((m, n), BF16),
        grid=(tiles.grid,),
        in_specs=[tiles.x_spec(d), tiles.mod_spec(d),
                  _resident((1, d), lambda i: (0, 0)),
                  _resident((d, n), lambda i: (0, 0)),
                  _resident((1, n), lambda i: (0, 0)),
                  tab, tab, tab],
        out_specs=tiles.x_spec(n),
        compiler_params=_cparams(1, 40),
        name="attn_qkv_rope",
    )(xs, mods, nw, w, b, cos, sa, sb)


def _attention_kernel(q_ref, kp_ref, kc_ref, kn_ref, kx_ref, sink_ref, o_ref, *,
                      u0, n_chunks, ctx_chunks, n_ctx):
    u = pl.program_id(1) + u0
    ii = lax.broadcasted_iota(jnp.int32, (BLOCK, BLOCK), 0)
    jj = lax.broadcasted_iota(jnp.int32, (BLOCK, BLOCK), 1)
    latent = u >= ctx_chunks
    m_prev = jnp.logical_and(jj >= ii, jnp.logical_and(latent, u - 1 >= ctx_chunks))
    m_cur = jnp.logical_and(ii >= 0, latent)
    m_next = jnp.logical_and(ii >= jj, jnp.logical_and(latent, u + 1 <= n_chunks - 1))
    bias = jnp.concatenate([jnp.where(mk, 0.0, -jnp.inf).astype(F32) for mk in (m_prev, m_cur, m_next)]
                           + [jnp.zeros((BLOCK, n_ctx), F32)], axis=1)
    kv = jnp.concatenate([kp_ref[0], kc_ref[0], kn_ref[0], kx_ref[0]], axis=0)
    kv_w = KV_HEADS * HEAD_DIM
    per_kv = Q_HEADS // KV_HEADS
    for g in range(KV_HEADS):
        kg = kv[:, g * HEAD_DIM:(g + 1) * HEAD_DIM]
        vg = kv[:, kv_w + g * HEAD_DIM:kv_w + (g + 1) * HEAD_DIM]
        for r in range(per_kv):
            hd = g * per_kv + r
            qh = q_ref[0, :, hd * HEAD_DIM:(hd + 1) * HEAD_DIM]
            s = lax.dot_general(qh, kg, (((1,), (1,)), ((), ())), preferred_element_type=F32)
            s = s + bias
            sink = sink_ref[0:1, hd:hd + 1]
            mx = jnp.maximum(jnp.max(s, axis=-1, keepdims=True), sink)
            p = jnp.exp(s - mx)
            den = jnp.sum(p, axis=-1, keepdims=True) + jnp.exp(sink - mx)
            o = jnp.dot(p.astype(BF16), vg, preferred_element_type=F32) / den
            o_ref[0, :, hd * HEAD_DIM:(hd + 1) * HEAD_DIM] = o.astype(BF16)


def _attention(qkv, sink, batch, t_total, n_ctx, need_ctx):
    n_chunks = t_total // BLOCK
    ctx_chunks = n_ctx // BLOCK
    u0 = 0 if need_ctx else ctx_chunks
    q_w = Q_HEADS * HEAD_DIM
    kv2 = 2 * KV_HEADS * HEAD_DIM
    kv_blk = q_w // kv2
    assert q_w % kv2 == 0
    qkv3 = qkv.reshape(batch, t_total, q_w + kv2)

    def win(delta):
        return pl.BlockSpec(
            (1, BLOCK, kv2),
            lambda b, n: (b, jnp.clip(n + u0 + delta, ctx_chunks, n_chunks - 1), kv_blk))

    return pl.pallas_call(
        functools.partial(_attention_kernel, u0=u0, n_chunks=n_chunks, ctx_chunks=ctx_chunks, n_ctx=n_ctx),
        out_shape=jax.ShapeDtypeStruct((batch, t_total, q_w), BF16),
        grid=(batch, n_chunks - u0),
        in_specs=[pl.BlockSpec((1, BLOCK, q_w), lambda b, n: (b, n + u0, 0)),
                  win(-1), win(0), win(1),
                  pl.BlockSpec((1, n_ctx, kv2), lambda b, n: (b, 0, kv_blk)),
                  pl.BlockSpec((1, Q_HEADS), lambda b, n: (0, 0))],
        out_specs=pl.BlockSpec((1, BLOCK, q_w), lambda b, n: (b, n + u0, 0)),
        compiler_params=_cparams(2, 40),
        name="attn_core",
    )(qkv3, qkv3, qkv3, qkv3, qkv3, sink)


def _rope_tables(seq, n_ctx):
    axis_dim = HEAD_DIM // 2
    freqs = axis_dim // 2
    rows = seq // GRID_W
    row_ids = jnp.repeat(jnp.arange(rows), GRID_W).astype(F32)
    col_ids = jnp.tile(jnp.arange(GRID_W), rows).astype(F32)
    inv_freq = ROPE_BASE ** (-jnp.arange(freqs, dtype=F32) * 2.0 / axis_dim)
    ang_r = row_ids[:, None] * inv_freq
    ang_c = col_ids[:, None] * inv_freq
    ang = jnp.concatenate([ang_r, ang_r, ang_c, ang_c], axis=1)
    cos = jnp.cos(ang)
    sin = jnp.sin(ang)
    lane = jnp.arange(HEAD_DIM)
    second_half = (lane % axis_dim) >= freqs
    sa = jnp.where(second_half, sin, 0.0)
    sb = jnp.where(second_half, 0.0, -sin)

    def full(t, ctx_val):
        t = jnp.concatenate([jnp.full((n_ctx, HEAD_DIM), ctx_val, F32), t], axis=0)
        return jnp.tile(t, (1, LANES // HEAD_DIM))

    return full(cos, 1.0), full(sa, 0.0), full(sb, 0.0)


def _swiglu_rows(h, w1_ref, w3_ref, w2_ref, lead, f_chunk):
    f_total = w1_ref.shape[-1]
    out = None
    for f0 in range(0, f_total, f_chunk):
        sl = lead + (slice(None), slice(f0, f0 + f_chunk))
        a = jnp.dot(h, w1_ref[sl], preferred_element_type=F32)
        b = jnp.dot(h, w3_ref[sl], preferred_element_type=F32)
        u = (_silu(a) * b).astype(BF16)
        part = jnp.dot(u, w2_ref[lead + (slice(f0, f0 + f_chunk), slice(None))], preferred_element_type=F32)
        out = part if out is None else out + part
    return out


def _ffn_dense_kernel(x_ref, mod_ref, nw_ref, w1_ref, w3_ref, w2_ref, o_ref, *, f_chunk):
    x = x_ref[...]
    h = _norm_mod(x, nw_ref[...], mod_ref[0, 3:4, :], mod_ref[0, 4:5, :]).astype(BF16)
    f = _swiglu_rows(h, w1_ref, w3_ref, w2_ref, (), f_chunk)
    o_ref[...] = x + mod_ref[0, 5:6, :] * f


def _ffn_dense(xs, mods, nw, w1, w3, w2, tiles):
    m, d = xs.shape
    f = w1.shape[1]
    f_chunk = f // 2 if (f // 2) % LANES == 0 else f
    return pl.pallas_call(
        functools.partial(_ffn_dense_kernel, f_chunk=f_chunk),
        out_shape=jax.ShapeDtypeStruct((m, d), F32),
        grid=(tiles.grid,),
        in_specs=[tiles.x_spec(d), tiles.mod_spec(d),
                  _resident((1, d), lambda i: (0, 0)),
                  _resident((d, f), lambda i: (0, 0)),
                  _resident((d, f), lambda i: (0, 0)),
                  _resident((f, d), lambda i: (0, 0))],
        out_specs=tiles.x_spec(d),
        input_output_aliases={0: 0},
        compiler_params=_cparams(1, 48),
        name="ffn_dense",
    )(xs, mods, nw, w1, w3, w2)


def _moe_router_kernel(x_ref, mod_ref, nw_ref, rt_ref, h_ref, idx_ref, gate_ref):
    h = _norm_mod(x_ref[...], nw_ref[...], mod_ref[0, 3:4, :], mod_ref[0, 4:5, :])
    h_ref[:, 0, :] = h
    logits = lax.dot_general(rt_ref[...], h, (((1,), (1,)), ((), ())),
                             preferred_element_type=F32, precision=HIGHEST)
    e = lax.broadcasted_iota(jnp.int32, logits.shape, 0)
    m1 = jnp.max(logits, axis=0, keepdims=True)
    i1 = jnp.min(jnp.where(logits == m1, e, N_EXPERTS), axis=0, keepdims=True)
    rest = jnp.where(e == i1, -jnp.inf, logits)
    m2 = jnp.max(rest, axis=0, keepdims=True)
    i2 = jnp.min(jnp.where(rest == m2, e, N_EXPERTS), axis=0, keepdims=True)
    t = jnp.exp(m2 - m1)
    idx_ref[...] = jnp.concatenate([i1, i2], axis=0)
    gate_ref[...] = jnp.concatenate([1.0 / (1.0 + t), t / (1.0 + t)], axis=0)


def _moe_router(xs, mods, nw, router_t, tiles):
    m, d = xs.shape
    pair = pl.BlockSpec((TOP_K, ROW_TILE), lambda i: (0, tiles.row(i)))
    return pl.pallas_call(
        _moe_router_kernel,
        out_shape=(jax.ShapeDtypeStruct((m, 1, d), F32),
                   jax.ShapeDtypeStruct((TOP_K, m), jnp.int32),
                   jax.ShapeDtypeStruct((TOP_K, m), F32)),
        grid=(tiles.grid,),
        in_specs=[tiles.x_spec(d), tiles.mod_spec(d),
                  _resident((1, d), lambda i: (0, 0)),
                  _resident((N_EXPERTS, d), lambda i: (0, 0))],
        out_specs=(pl.BlockSpec((ROW_TILE, 1, d), lambda i: (tiles.row(i), 0, 0)), pair, pair),
        compiler_params=_cparams(1, 32),
        name="moe_router",
    )(xs, mods, nw, router_t)


def _moe_experts_kernel(te_ref, nu_ref, rid_ref, rid_next_ref, h_hbm, gate_ref, w1_ref, w3_ref, w2_ref,
                        o_ref, gbuf, sem, *, f_chunk):
    j = pl.program_id(0)
    n = pl.num_programs(0)
    slot = j % 2

    def row_copy(rids, i, slot_):
        return pltpu.make_async_copy(h_hbm.at[rids[0, 0, i]], gbuf.at[slot_, i], sem.at[slot_])

    def gather_start(rids, slot_):
        def body(i, carry):
            row_copy(rids, i, slot_).start()
            return carry
        lax.fori_loop(0, EXPERT_TILE, body, 0)

    @pl.when(j == 0)
    def _():
        gather_start(rid_ref, slot)

    @pl.when(j + 1 < n)
    def _():
        gather_start(rid_next_ref, 1 - slot)

    def wait_body(i, carry):
        row_copy(rid_ref, i, slot).wait()
        return carry
    lax.fori_loop(0, EXPERT_TILE, wait_body, 0)

    @pl.when(j < nu_ref[0])
    def _():
        f = _swiglu_rows(gbuf[slot, :, 0, :].astype(BF16), w1_ref, w3_ref, w2_ref, (0,), f_chunk)
        o_ref[:, 0, :] = gate_ref[...] * f

    @pl.when(j >= nu_ref[0])
    def _():
        o_ref[...] = jnp.zeros_like(o_ref)


def _moe_experts(h, tile_expert, n_used, row_ids, slot_gate, w1, w3, w2):
    m, _, d = h.shape
    n_tiles = tile_expert.shape[0]
    f = w1.shape[2]
    rid3 = row_ids.reshape(n_tiles, 1, EXPERT_TILE)
    grid_spec = pltpu.PrefetchScalarGridSpec(
        num_scalar_prefetch=2,
        grid=(n_tiles,),
        in_specs=[
            pl.BlockSpec((1, 1, EXPERT_TILE), lambda j, te, nu: (j, 0, 0), memory_space=pltpu.SMEM),
            pl.BlockSpec((1, 1, EXPERT_TILE), lambda j, te, nu: (jnp.minimum(j + 1, n_tiles - 1), 0, 0),
                         memory_space=pltpu.SMEM),
            pl.BlockSpec(memory_space=pl.ANY),
            pl.BlockSpec((EXPERT_TILE, 1), lambda j, te, nu: (j, 0)),
            _resident((1, d, f), lambda j, te, nu: (te[j], 0, 0)),
            _resident((1, d, f), lambda j, te, nu: (te[j], 0, 0)),
            _resident((1, f, d), lambda j, te, nu: (te[j], 0, 0)),
        ],
        out_specs=pl.BlockSpec((EXPERT_TILE, 1, d), lambda j, te, nu: (j, 0, 0)),
        scratch_shapes=[pltpu.VMEM((2, EXPERT_TILE, 1, d), F32), pltpu.SemaphoreType.DMA((2,))],
    )
    return pl.pallas_call(
        functools.partial(_moe_experts_kernel, f_chunk=f // 2),
        out_shape=jax.ShapeDtypeStruct((n_tiles * EXPERT_TILE, 1, d), F32),
        grid_spec=grid_spec,
        compiler_params=_cparams(1, 56),
        name="moe_experts",
    )(tile_expert, n_used, rid3, rid3, h, slot_gate, w1, w3, w2)


def _moe_combine_kernel(p0_ref, p1_ref, x_ref, mod_ref, ys_hbm, o_ref, rbuf, sem):
    def row_copy(pref, k, i):
        return pltpu.make_async_copy(ys_hbm.at[pref[0, 0, i]], rbuf.at[k, i], sem.at[k])

    def start_body(i, carry):
        row_copy(p0_ref, 0, i).start()
        row_copy(p1_ref, 1, i).start()
        return carry
    lax.fori_loop(0, ROW_TILE, start_body, 0)

    def wait_body(i, carry):
        row_copy(p0_ref, 0, i).wait()
        row_copy(p1_ref, 1, i).wait()
        return carry
    lax.fori_loop(0, ROW_TILE, wait_body, 0)

    o_ref[...] = x_ref[...] + mod_ref[0, 5:6, :] * (rbuf[0, :, 0, :] + rbuf[1, :, 0, :])


def _moe_combine(xs, mods, ys, pos, tiles):
    m, d = xs.shape
    n_row_tiles = m // ROW_TILE
    p3 = pos.reshape(TOP_K, n_row_tiles, 1, ROW_TILE)
    pspec = pl.BlockSpec((1, 1, ROW_TILE), lambda i: (tiles.row(i), 0, 0), memory_space=pltpu.SMEM)
    return pl.pallas_call(
        _moe_combine_kernel,
        out_shape=jax.ShapeDtypeStruct((m, d), F32),
        grid=(tiles.grid,),
        in_specs=[pspec, pspec, tiles.x_spec(d), tiles.mod_spec(d), pl.BlockSpec(memory_space=pl.ANY)],
        out_specs=tiles.x_spec(d),
        scratch_shapes=[pltpu.VMEM((TOP_K, ROW_TILE, 1, d), F32), pltpu.SemaphoreType.DMA((TOP_K,))],
        input_output_aliases={2: 0},
        compiler_params=_cparams(1, 32),
        name="moe_combine",
    )(p3[0], p3[1], xs, mods, ys)


def _moe_layer(xs, mods, nw, router, w1, w3, w2, tiles, route_tiles):
    m, d = xs.shape
    h, top_i, gates = _moe_router(xs, mods, nw, router.T.astype(F32), route_tiles)

    visited = jnp.repeat((jnp.arange(m // ROW_TILE) % route_tiles.tpb) >= route_tiles.lo, ROW_TILE)
    eid = jnp.where(visited[None, :], top_i, N_EXPERTS).reshape(-1)
    onehot = (eid[:, None] == jnp.arange(N_EXPERTS)[None, :]).astype(jnp.int32)
    csum = jnp.cumsum(onehot, axis=0)
    counts = csum[-1]
    rank = jnp.take_along_axis(csum, jnp.minimum(eid, N_EXPERTS - 1)[:, None], axis=1)[:, 0] - 1
    padded = ((counts + EXPERT_TILE - 1) // EXPERT_TILE) * EXPERT_TILE
    ends = jnp.cumsum(padded)
    starts = ends - padded
    n_slots = TOP_K * route_tiles.grid * ROW_TILE + N_EXPERTS * EXPERT_TILE
    n_tiles = n_slots // EXPERT_TILE
    pos = jnp.where(eid < N_EXPERTS, starts[jnp.minimum(eid, N_EXPERTS - 1)] + rank, n_slots)
    token = jnp.tile(jnp.arange(m, dtype=jnp.int32), TOP_K)
    row_ids = jnp.zeros((n_slots,), jnp.int32).at[pos].set(token, mode="drop")
    slot_gate = jnp.zeros((n_slots,), F32).at[pos].set(gates.reshape(-1), mode="drop")
    tile_start = jnp.arange(n_tiles, dtype=jnp.int32) * EXPERT_TILE
    n_used = (ends[-1] // EXPERT_TILE).astype(jnp.int32)
    tile_expert = jnp.searchsorted(ends, jnp.minimum(tile_start, ends[-1] - 1), side="right")
    tile_expert = jnp.clip(tile_expert, 0, N_EXPERTS - 1).astype(jnp.int32)

    ys = _moe_experts(h, tile_expert, n_used.reshape(1), row_ids, slot_gate.reshape(-1, 1), w1, w3, w2)
    pos = jnp.minimum(pos, n_slots - 1).astype(jnp.int32).reshape(TOP_K, m)
    return _moe_combine(xs, mods, ys, pos, tiles)


def _final_norm_kernel(x_ref, w_ref, o_ref):
    x = x_ref[...]
    o_ref[...] = x * lax.rsqrt(jnp.mean(x * x, axis=-1, keepdims=True) + EPS) * w_ref[...]


def _final_norm(xs, w, tiles):
    d = xs.shape[1]
    return pl.pallas_call(
        _final_norm_kernel,
        out_shape=jax.ShapeDtypeStruct((tiles.grid * ROW_TILE, d), F32),
        grid=(tiles.grid,),
        in_specs=[tiles.x_spec(d), pl.BlockSpec((1, d), lambda i: (0, 0))],
        out_specs=pl.BlockSpec((ROW_TILE, d), lambda i: (i, 0)),
        compiler_params=_cparams(1, 32),
        name="final_norm",
    )(xs, w)


def kernel(x, c, ctx, c_ctx, w_mod, b_mod, norm1_w, norm2_w, ssd_w_in, ssd_conv_w, ssd_conv_b, ssd_dt_bias, ssd_a_log, ssd_d, ssd_norm_w, ssd_w_out, attn_w_qkv, attn_b_qkv, attn_sink, attn_w_o, attn_b_o, ffn_w1, ffn_w3, ffn_w2, moe_router, moe_w1, moe_w3, moe_w2, final_norm_w):
    batch, seq, d = x.shape
    n_ctx = ctx.shape[1]
    t_total = n_ctx + seq
    depth = w_mod.shape[0]
    assert seq % GRID_W == 0 and seq % BLOCK == 0 and n_ctx % ROW_TILE == 0

    all_tiles = _Tiles(batch, t_total, n_ctx, 0)
    lat_tiles = _Tiles(batch, t_total, n_ctx, n_ctx // ROW_TILE)

    xs = jnp.concatenate([ctx, x], axis=1).reshape(batch * t_total, d)
    mods_all = _mod_all(c, c_ctx, w_mod, b_mod)
    cos, sa, sb = _rope_tables(seq, n_ctx)
    q_w = Q_HEADS * HEAD_DIM
    q_scale = jnp.concatenate([jnp.full((q_w,), 1.0 / math.sqrt(HEAD_DIM), F32),
                               jnp.ones((attn_w_qkv.shape[2] - q_w,), F32)])

    for i in range(depth):
        last = i == depth - 1
        j = i // 2
        mods = mods_all[i]
        nw1 = norm1_w[i].reshape(1, d)
        nw2 = norm2_w[i].reshape(1, d)
        upd = lat_tiles if last else all_tiles
        if i % 2 == 0:
            xs = _ssd_mixer(xs, mods, nw1, ssd_w_in[j], ssd_conv_w[j], ssd_conv_b[j], ssd_dt_bias[j],
                            ssd_a_log[j], ssd_d[j], ssd_norm_w[j], ssd_w_out[j],
                            all_tiles, batch, t_total, n_ctx)
        else:
            qkv = _qkv_rope(xs, mods, nw1, (attn_w_qkv[j] * q_scale).astype(BF16),
                            (attn_b_qkv[j] * q_scale).reshape(1, -1), cos, sa, sb, all_tiles)
            o = _attention(qkv, attn_sink[j].reshape(1, Q_HEADS), batch, t_total, n_ctx, not last)
            xs = _proj_residual(xs, o.reshape(batch * t_total, q_w), attn_w_o[j].astype(BF16),
                                attn_b_o[j].reshape(1, d), mods, 2, upd)
        if i % 2 == 0:
            xs = _ffn_dense(xs, mods, nw2, ffn_w1[j].astype(BF16), ffn_w3[j].astype(BF16),
                            ffn_w2[j].astype(BF16), upd)
        else:
            xs = _moe_layer(xs, mods, nw2, moe_router[j], moe_w1[j].astype(BF16), moe_w3[j].astype(BF16),
                            moe_w2[j].astype(BF16), upd, upd)
    out = _final_norm(xs, final_norm_w.reshape(1, d), lat_tiles)
    return out.reshape(batch, seq, d)
```

```python
import functools
import math

import jax
import jax.numpy as jnp
from jax import lax
from jax.experimental import pallas as pl
from jax.experimental.pallas import tpu as pltpu

F32 = jnp.float32
BF16 = jnp.bfloat16
HIGHEST = lax.Precision.HIGHEST

EPS = 1e-6
N_MOD = 6
GRID_W = 64
ROPE_BASE = 10000.0

SSD_HEAD_DIM = 64
SSD_GROUPS = 8
D_STATE = 128
CONV_K = 5
CHUNK = 128
HEADS_PER_GROUP = 4
GROUP_W = HEADS_PER_GROUP * SSD_HEAD_DIM

Q_HEADS = 16
KV_HEADS = 4
HEAD_DIM = 64
BLOCK = 128

N_EXPERTS = 8
TOP_K = 2

ROW_TILE = 256
EXPERT_TILE = 256
LANES = 128
PACK_ROWS_BF16 = 16

MIB = 1024 * 1024


def _cparams(n_axes, vmem_mib):
    return pltpu.CompilerParams(dimension_semantics=("arbitrary",) * n_axes,
                                vmem_limit_bytes=vmem_mib * MIB)


def _resident(block_shape, index_map):
    return pl.BlockSpec(block_shape, index_map, pipeline_mode=pl.Buffered(1))


def _sigmoid(v):
    return 1.0 / (1.0 + jnp.exp(-v))


def _silu(v):
    return v * _sigmoid(v)


def _norm_mod(x, nw, shift, scale):
    ms = jnp.mean(x * x, axis=-1, keepdims=True)
    y = x * lax.rsqrt(ms + EPS) * nw
    return y * (1.0 + scale) + shift


class _Tiles:
    def __init__(self, batch, t_total, n_ctx, lo_tiles):
        assert t_total % ROW_TILE == 0 and n_ctx % ROW_TILE == 0
        self.batch = batch
        self.tpb = t_total // ROW_TILE
        self.ctx_tiles = n_ctx // ROW_TILE
        self.lo = lo_tiles
        self.n_w = self.tpb - lo_tiles
        self.grid = batch * self.n_w

    def split(self, i):
        return i // self.n_w, self.lo + i % self.n_w

    def row(self, i):
        b, w = self.split(i)
        return b * self.tpb + w

    def mod_row(self, i):
        b, w = self.split(i)
        return jnp.where(w < self.ctx_tiles, self.batch, b)

    def x_spec(self, width):
        return pl.BlockSpec((ROW_TILE, width), lambda i: (self.row(i), 0))

    def mod_spec(self, d):
        return pl.BlockSpec((1, N_MOD, d), lambda i: (self.mod_row(i), 0, 0))


def _mod_kernel(c_ref, w_ref, b_ref, o_ref):
    s = _silu(c_ref[...])
    o_ref[0] = jnp.dot(s, w_ref[0], preferred_element_type=F32, precision=HIGHEST) + b_ref[0]


def _mod_all(c, c_ctx, w_mod, b_mod):
    depth, d, n = w_mod.shape
    cc = jnp.concatenate([c, c_ctx[None]], axis=0)
    rows = cc.shape[0]
    tn = 1536
    assert n % tn == 0
    out = pl.pallas_call(
        _mod_kernel,
        out_shape=jax.ShapeDtypeStruct((depth, rows, n), F32),
        grid=(depth, n // tn),
        in_specs=[pl.BlockSpec((rows, d), lambda i, j: (0, 0)),
                  pl.BlockSpec((1, d, tn), lambda i, j: (i, 0, j)),
                  pl.BlockSpec((1, 1, tn), lambda i, j: (i, 0, j))],
        out_specs=pl.BlockSpec((1, rows, tn), lambda i, j: (i, 0, j)),
        compiler_params=_cparams(2, 40),
        name="mod_all",
    )(cc, w_mod, b_mod.reshape(depth, 1, n))
    return out.reshape(depth, rows, N_MOD, d)


def _ssd_inproj_kernel(x_ref, mod_ref, nw_ref, wzx_ref, wdt_ref, zx_ref, dt_ref, *, n_chunk):
    h = _norm_mod(x_ref[...], nw_ref[...], mod_ref[0, 0:1, :], mod_ref[0, 1:2, :]).astype(BF16)
    n = wzx_ref.shape[1]
    for n0 in range(0, n, n_chunk):
        zx_ref[:, n0:n0 + n_chunk] = jnp.dot(
            h, wzx_ref[:, n0:n0 + n_chunk], preferred_element_type=F32).astype(BF16)
    dt_ref[...] = jnp.dot(h, wdt_ref[...], preferred_element_type=F32)


def _ssd_inproj(xs, mods, nw, wzx, wdt, tiles):
    m, d = xs.shape
    n = wzx.shape[1]
    return pl.pallas_call(
        functools.partial(_ssd_inproj_kernel, n_chunk=1536),
        out_shape=(jax.ShapeDtypeStruct((m, n), BF16), jax.ShapeDtypeStruct((m, LANES), F32)),
        grid=(tiles.grid,),
        in_specs=[tiles.x_spec(d), tiles.mod_spec(d),
                  _resident((1, d), lambda i: (0, 0)),
                  _resident((d, n), lambda i: (0, 0)),
                  _resident((d, LANES), lambda i: (0, 0))],
        out_specs=(tiles.x_spec(n), tiles.x_spec(LANES)),
        compiler_params=_cparams(1, 48),
        name="ssd_inproj",
    )(xs, mods, nw, wzx, wdt)


def _dt_prep_kernel(raw_ref, bias_ref, a_ref, dtc_ref, acc_ref, dtr_ref, acr_ref):
    v = raw_ref[...] + bias_ref[...]
    dt = jnp.maximum(v, 0.0) + jnp.log(1.0 + jnp.exp(-jnp.abs(v)))
    a = dt * a_ref[...]
    ii = lax.broadcasted_iota(jnp.int32, (CHUNK, CHUNK), 0)
    jj = lax.broadcasted_iota(jnp.int32, (CHUNK, CHUNK), 1)
    prefix = jnp.dot((ii >= jj).astype(F32), a, preferred_element_type=F32, precision=HIGHEST)
    suffix = jnp.dot((ii <= jj).astype(F32), a, preferred_element_type=F32, precision=HIGHEST)
    col = lax.broadcasted_iota(jnp.int32, (CHUNK, LANES), 1)
    is_fwd = (col % (2 * HEADS_PER_GROUP)) < HEADS_PER_GROUP
    ac = jnp.where(is_fwd, prefix, suffix)
    dtc_ref[...] = dt
    acc_ref[...] = ac
    dtr_ref[0] = dt.T
    acr_ref[0] = ac.T


def _dt_prep(dt_raw, bias, a_neg):
    m = dt_raw.shape[0]
    nchunks = m // CHUNK
    tile = pl.BlockSpec((CHUNK, LANES), lambda i: (i, 0))
    vec = pl.BlockSpec((1, LANES), lambda i: (0, 0))
    sq = pl.BlockSpec((1, LANES, CHUNK), lambda i: (i, 0, 0))
    return pl.pallas_call(
        _dt_prep_kernel,
        out_shape=(jax.ShapeDtypeStruct((m, LANES), F32), jax.ShapeDtypeStruct((m, LANES), F32),
                   jax.ShapeDtypeStruct((nchunks, LANES, CHUNK), F32),
                   jax.ShapeDtypeStruct((nchunks, LANES, CHUNK), F32)),
        grid=(nchunks,),
        in_specs=[tile, vec, vec],
        out_specs=(tile, tile, sq, sq),
        compiler_params=_cparams(1, 32),
        name="ssd_dt_prep",
    )(dt_raw, bias, a_neg)


def _split3(v):
    def top_bits(a):
        return lax.bitcast_convert_type(
            lax.bitcast_convert_type(a, jnp.uint32) & jnp.uint32(0xFFFF0000), F32)

    hi = top_bits(v)
    r1 = v - hi
    mid = top_bits(r1)
    lo = r1 - mid
    return hi.astype(BF16), mid.astype(BF16), lo.astype(BF16)


N_BCAST = 4 * CHUNK + 2 * GROUP_W
PIECE_LANES = 3 * 2 * HEADS_PER_GROUP


def _bcast_selector():
    lane = jnp.arange(LANES)[:, None]
    col = jnp.arange(N_BCAST)[None, :]
    n_q = 2 * HEADS_PER_GROUP
    sel = []
    for d in range(2):
        rel = lane - PIECE_LANES * d
        q = rel % n_q
        tile = (col < 4 * CHUNK) & (col // CHUNK == q)
        acc_exp = (col >= 4 * CHUNK) & (col < 4 * CHUNK + GROUP_W) & ((col - 4 * CHUNK) // SSD_HEAD_DIM == q)
        dt_exp = (col >= 4 * CHUNK + GROUP_W) & (
            (col - 4 * CHUNK - GROUP_W) // SSD_HEAD_DIM + HEADS_PER_GROUP == q)
        in_dir = (rel >= 0) & (rel < PIECE_LANES)
        sel.append(((tile | acc_exp | dt_exp) & in_dir).astype(BF16))
    return jnp.stack(sel)


def _ssd_scan_kernel(z_ref, x_ref, b_ref, c_ref, cw_ref, colp_ref, dtr_ref, acr_ref, e_ref,
                     dsk_ref, nw_ref, o_ref, xc_s, cc_s, bt_s, yf_s, yb_s, sf_s, sb_s, *, n_chunks, ctx_chunks):
    t_total = n_chunks * CHUNK
    halo = PACK_ROWS_BF16
    win = CHUNK + 2 * halo
    pad = CONV_K // 2

    def conv_silu(ref, c, t0, col0, width):
        cur = ref[0, pl.ds(t0, CHUNK), :]
        prev = ref[0, pl.ds(pl.multiple_of(jnp.maximum(t0 - halo, 0), halo), halo), :]
        nxt = ref[0, pl.ds(pl.multiple_of(jnp.minimum(t0 + CHUNK, t_total - halo), halo), halo), :]
        prev_ok = jnp.logical_and(c != 0, c != ctx_chunks)
        next_ok = jnp.logical_and(c != ctx_chunks - 1, c != n_chunks - 1)
        prev = jnp.where(prev_ok, prev, jnp.zeros_like(prev))
        nxt = jnp.where(next_ok, nxt, jnp.zeros_like(nxt))
        w = jnp.concatenate([prev, cur, nxt], axis=0).astype(F32)
        acc = jnp.broadcast_to(cw_ref[0, CONV_K:CONV_K + 1, col0:col0 + width], (CHUNK, width))
        for k in range(CONV_K):
            shifted = w if k == pad else pltpu.roll(w, (pad - k) % win, 0)
            acc = acc + shifted[halo:halo + CHUNK] * cw_ref[0, k:k + 1, col0:col0 + width]
        return _silu(acc)

    def conv_body(c, carry):
        t0 = pl.multiple_of(c * CHUNK, CHUNK)
        xc_s[pl.ds(t0, CHUNK), :] = conv_silu(x_ref, c, t0, 0, GROUP_W).astype(BF16)
        cc_s[pl.ds(t0, CHUNK), :] = conv_silu(c_ref, c, t0, GROUP_W + D_STATE, D_STATE).astype(BF16)
        bt_s[c] = conv_silu(b_ref, c, t0, GROUP_W, D_STATE).T.astype(BF16)
        return carry

    lax.fori_loop(0, n_chunks, conv_body, 0)

    ii = lax.broadcasted_iota(jnp.int32, (CHUNK, CHUNK), 0)
    jj = lax.broadcasted_iota(jnp.int32, (CHUNK, CHUNK), 1)
    lane_w = lax.broadcasted_iota(jnp.int32, (CHUNK, GROUP_W), 1)
    y_s = (yf_s, yb_s)
    st_s = (sf_s, sb_s)
    sf_s[...] = jnp.zeros_like(sf_s)
    sb_s[...] = jnp.zeros_like(sb_s)

    def chunk_step(d, c):
        mask = (ii >= jj) if d == 0 else (ii <= jj)
        last = CHUNK - 1 if d == 0 else 0
        t0 = pl.multiple_of(c * CHUNK, CHUNK)
        xch = xc_s[pl.ds(t0, CHUNK), :]
        cch = cc_s[pl.ds(t0, CHUNK), :]
        bt = bt_s[c]
        dtr = dtr_ref[0, c]
        acr = acr_ref[0, c]
        state = st_s[d][...]

        bc = jnp.dot(colp_ref[0, 0, pl.ds(t0, CHUNK), :], e_ref[d], preferred_element_type=F32)
        acc_exp = bc[:, 4 * CHUNK:4 * CHUNK + GROUP_W]
        dt_exp = bc[:, 4 * CHUNK + GROUP_W:]
        cb = jnp.dot(cch, bt, preferred_element_type=F32)
        ms = []
        xs_ = []
        for r in range(HEADS_PER_GROUP):
            k = 4 * d + r
            seg = bc[:, r * CHUNK:(r + 1) * CHUNK] - acr[k:k + 1, :]
            decay = jnp.exp(jnp.where(mask, seg, -jnp.inf))
            ms.append((cb * decay * dtr[k:k + 1, :]).astype(BF16))
            in_head = jnp.logical_and(lane_w >= r * SSD_HEAD_DIM, lane_w < (r + 1) * SSD_HEAD_DIM)
            xs_.append(jnp.where(in_head, xch, jnp.zeros_like(xch)))
        y = jnp.dot(jnp.concatenate(ms, axis=1), jnp.concatenate(xs_, axis=0), preferred_element_type=F32)
        y = y + jnp.dot(cch, state.astype(BF16), preferred_element_type=F32) * jnp.exp(acc_exp)
        y_s[d][pl.ds(t0, CHUNK), :] = y

        a_last = acc_exp[last:last + 1, :]
        xw = (xch.astype(F32) * (jnp.exp(a_last - acc_exp) * dt_exp)).astype(BF16)
        st_s[d][...] = state * jnp.exp(a_last) + jnp.dot(bt, xw, preferred_element_type=F32)

    def scan_body(s, carry):
        chunk_step(0, s)
        chunk_step(1, jnp.where(s < ctx_chunks, ctx_chunks - 1 - s, n_chunks - 1 - (s - ctx_chunks)))
        return carry

    lax.fori_loop(0, n_chunks, scan_body, 0, unroll=2)

    def out_body(c, carry):
        t0 = pl.multiple_of(c * CHUNK, CHUNK)
        y = (yf_s[pl.ds(t0, CHUNK), :] + yb_s[pl.ds(t0, CHUNK), :]
             + dsk_ref[0] * xc_s[pl.ds(t0, CHUNK), :].astype(F32))
        g = y * _silu(z_ref[0, pl.ds(t0, CHUNK), :].astype(F32))
        g = g * lax.rsqrt(jnp.mean(g * g, axis=-1, keepdims=True) + EPS)
        o_ref[0, pl.ds(t0, CHUNK), :] = (g * nw_ref[0]).astype(BF16)
        return carry

    lax.fori_loop(0, n_chunks, out_body, 0)


def _ssd_scan(zx, cw, colp, dtr, acr, dskip, norm_w, batch, t_total, n_ctx):
    d_inner = SSD_GROUPS * GROUP_W
    n_chunks = t_total // CHUNK
    zx3 = zx.reshape(batch, t_total, zx.shape[-1])
    xoff = d_inner // GROUP_W
    boff = 2 * d_inner // D_STATE
    coff = boff + SSD_GROUPS
    pc = CONV_K + 3
    rows =pl.BlockSpec((1, n_chunks, 2 * HEADS_PER_GROUP, CHUNK), lambda b, g: (b, 0, g, 0))
    vec = pl.BlockSpec((1, 1, GROUP_W), lambda b, g: (g, 0, 0))
    return pl.pallas_call(
        functools.partial(_ssd_scan_kernel, n_chunks=n_chunks, ctx_chunks=n_ctx // CHUNK),
        out_shape=jax.ShapeDtypeStruct((batch, t_total, d_inner), BF16),
        grid=(batch, SSD_GROUPS),
        in_specs=[pl.BlockSpec((1, t_total, GROUP_W), lambda b, g: (b, 0, g)),
                  pl.BlockSpec((1, t_total, GROUP_W), lambda b, g: (b, 0, xoff + g)),
                  pl.BlockSpec((1, t_total, D_STATE), lambda b, g: (b, 0, boff + g)),
                  pl.BlockSpec((1, t_total, D_STATE), lambda b, g: (b, 0, coff + g)),
                  pl.BlockSpec((1, pc, GROUP_W + 2 * D_STATE), lambda b, g: (g, 0, 0)),
                  pl.BlockSpec((1, 1, t_total, LANES), lambda b, g: (b, g, 0, 0)),
                  rows, rows,
                  _resident((2, LANES, N_BCAST), lambda b, g: (0, 0, 0)),
                  vec, vec],
        out_specs=pl.BlockSpec((1, t_total, GROUP_W), lambda b, g: (b, 0, g)),
        scratch_shapes=[pltpu.VMEM((t_total, GROUP_W), BF16),
                        pltpu.VMEM((t_total, D_STATE), BF16),
                        pltpu.VMEM((n_chunks, D_STATE, CHUNK), BF16),
                        pltpu.VMEM((t_total, GROUP_W), F32),
                        pltpu.VMEM((t_total, GROUP_W), F32),
                        pltpu.VMEM((D_STATE, GROUP_W), F32),
                        pltpu.VMEM((D_STATE, GROUP_W), F32)],
        compiler_params=_cparams(2, 56),
        name="ssd_scan",
    )(zx3, zx3, zx3, zx3, cw, colp, dtr, acr, _bcast_selector(), dskip, norm_w)


def _ssd_mixer(xs, mods, nw, w_in, conv_w, conv_b, dt_bias, a_log, d_skip, norm_w, w_out,
               tiles, batch, t_total, n_ctx):
    d = xs.shape[1]
    d_inner = SSD_GROUPS * GROUP_W
    conv_ch = d_inner + 2 * SSD_GROUPS * D_STATE
    heads = SSD_GROUPS * HEADS_PER_GROUP
    n_dt = 2 * heads
    perm = jnp.arange(n_dt).reshape(2, SSD_GROUPS, HEADS_PER_GROUP).transpose(1, 0, 2).reshape(-1)
    wzx = w_in[:, :d_inner + conv_ch].astype(BF16)
    wdt = jnp.zeros((d, LANES), F32).at[:, :n_dt].set(w_in[:, d_inner + conv_ch:][:, perm]).astype(BF16)
    bias = jnp.zeros((1, LANES), F32).at[0, :n_dt].set(dt_bias.reshape(-1)[perm])
    a_neg = jnp.zeros((1, LANES), F32).at[0, :n_dt].set(-jnp.exp(a_log.astype(F32)).reshape(-1)[perm])

    zx, dt_raw = _ssd_inproj(xs, mods, nw, wzx, wdt, tiles)
    dtc, acc, dtr, acr = _dt_prep(dt_raw, bias, a_neg)

    def cols(a):
        return a[:, :n_dt].reshape(batch, t_total, SSD_GROUPS, 2, HEADS_PER_GROUP).transpose(0, 2, 1, 3, 4)

    vals = jnp.concatenate([cols(acc), cols(dtc)], axis=-1)
    colp = jnp.stack(_split3(vals), axis=-2).reshape(batch, SSD_GROUPS, t_total, 2 * PIECE_LANES)
    colp = jnp.pad(colp, ((0, 0), (0, 0), (0, 0), (0, LANES - 2 * PIECE_LANES)))

    n_chunks = t_total // CHUNK
    dtr = dtr.reshape(batch, n_chunks, LANES, CHUNK)
    acr = acr.reshape(batch, n_chunks, LANES, CHUNK)

    def per_group(v):
        gx = v[:, :d_inner].reshape(-1, SSD_GROUPS, GROUP_W)
        gb = v[:, d_inner:d_inner + SSD_GROUPS * D_STATE].reshape(-1, SSD_GROUPS, D_STATE)
        gc = v[:, d_inner + SSD_GROUPS * D_STATE:].reshape(-1, SSD_GROUPS, D_STATE)
        return jnp.concatenate([gx, gb, gc], axis=-1).transpose(1, 0, 2)

    cw = per_group(jnp.concatenate([conv_w, conv_b[None], jnp.zeros((2, conv_ch), F32)], axis=0))
    dsk = jnp.repeat((d_skip[0] + d_skip[1]).astype(F32), SSD_HEAD_DIM).reshape(SSD_GROUPS, 1, GROUP_W)
    gnw = norm_w.astype(F32).reshape(SSD_GROUPS, 1, GROUP_W)

    g = _ssd_scan(zx, cw, colp, dtr, acr, dsk, gnw, batch, t_total, n_ctx)
    return _proj_residual(xs, g.reshape(batch * t_total, d_inner), w_out.astype(BF16),
                          jnp.zeros((1, d), F32), mods, 2, tiles)


def _proj_residual_kernel(x_ref, a_ref, w_ref, b_ref, mod_ref, o_ref, *, gate_row):
    y = jnp.dot(a_ref[...], w_ref[...], preferred_element_type=F32) + b_ref[...]
    o_ref[...] = x_ref[...] + mod_ref[0, gate_row:gate_row + 1, :] * y


def _proj_residual(xs, a, w, b, mods, gate_row, tiles):
    m, d = xs.shape
    k = a.shape[1]
    return pl.pallas_call(
        functools.partial(_proj_residual_kernel, gate_row=gate_row),
        out_shape=jax.ShapeDtypeStruct((m, d), F32),
        grid=(tiles.grid,),
        in_specs=[tiles.x_spec(d), tiles.x_spec(k),
                  _resident((k, d), lambda i: (0, 0)),
                  _resident((1, d), lambda i: (0, 0)),
                  tiles.mod_spec(d)],
        out_specs=tiles.x_spec(d),
        input_output_aliases={0: 0},
        compiler_params=_cparams(1, 40),
        name="proj_residual",
    )(xs, a, w, b, mods)


def _transpose_rows(blk):
    return jnp.concatenate([blk[r0:r0 + LANES].T for r0 in range(0, blk.shape[0], LANES)], axis=1)


def _qkv_rope_kernel(x_ref, mod_ref, nw_ref, w_ref, b_ref, cos_ref, sa_ref, sb_ref,
                     qt_ref, k_ref, vt_ref):
    h = _norm_mod(x_ref[...], nw_ref[...], mod_ref[0, 0:1, :], mod_ref[0, 1:2, :]).astype(BF16)
    acc = jnp.dot(h, w_ref[...], preferred_element_type=F32) + b_ref[...]
    cos = cos_ref[...]
    sa = sa_ref[...]
    sb = sb_ref[...]
    half = HEAD_DIM // 4
    q_w = Q_HEADS * HEAD_DIM
    kv_w = KV_HEADS * HEAD_DIM

    def rope(blk):
        return blk * cos + pltpu.roll(blk, half, 1) * sa + pltpu.roll(blk, LANES - half, 1) * sb

    for c0 in range(0, q_w, LANES):
        qt_ref[0, c0:c0 + LANES, :] = _transpose_rows(rope(acc[:, c0:c0 + LANES])).astype(BF16)
    for c0 in range(0, kv_w, LANES):
        k_ref[0, :, c0:c0 + LANES] = rope(acc[:, q_w + c0:q_w + c0 + LANES]).astype(BF16)
        vt_ref[0, c0:c0 + LANES, :] = _transpose_rows(
            acc[:, q_w + kv_w + c0:q_w + kv_w + c0 + LANES]).astype(BF16)


def _qkv_rope(xs, mods, nw, w, b, cos, sa, sb, tiles, batch, t_total):
    m, d = xs.shape
    n = w.shape[1]
    q_w = Q_HEADS * HEAD_DIM
    kv_w = KV_HEADS * HEAD_DIM
    tab = pl.BlockSpec((ROW_TILE, LANES), lambda i: (tiles.split(i)[1], 0))

    def feat_major(width):
        return pl.BlockSpec((1, width, ROW_TILE), lambda i: (tiles.split(i)[0], 0, tiles.split(i)[1]))

    return pl.pallas_call(
        _qkv_rope_kernel,
        out_shape=(jax.ShapeDtypeStruct((batch, q_w, t_total), BF16),
                   jax.ShapeDtypeStruct((batch, t_total, kv_w), BF16),
                   jax.ShapeDtypeStruct((batch, kv_w, t_total), BF16)),
        grid=(tiles.grid,),
        in_specs=[tiles.x_spec(d), tiles.mod_spec(d),
                  _resident((1, d), lambda i: (0, 0)),
                  _resident((d, n), lambda i: (0, 0)),
                  _resident((1, n), lambda i: (0, 0)),
                  tab, tab, tab],
        out_specs=(feat_major(q_w),
                   pl.BlockSpec((1, ROW_TILE, kv_w), lambda i: (tiles.split(i)[0], tiles.split(i)[1], 0)),
                   feat_major(kv_w)),
        compiler_params=_cparams(1, 40),
        name="attn_qkv_rope",
    )(xs, mods, nw, w, b, cos, sa, sb)


def _attention_kernel(qt_ref, kp_ref, kc_ref, kn_ref, kx_ref, vp_ref, vc_ref, vn_ref, vx_ref, sink_ref,
                      o_ref, *, u0, n_chunks, ctx_chunks, n_ctx):
    u = pl.program_id(1) + u0
    jj = lax.broadcasted_iota(jnp.int32, (BLOCK, BLOCK), 0)
    ii = lax.broadcasted_iota(jnp.int32, (BLOCK, BLOCK), 1)
    latent = u >= ctx_chunks
    m_prev = jnp.logical_and(jj >= ii, jnp.logical_and(latent, u - 1 >= ctx_chunks))
    m_cur = jnp.logical_and(ii >= 0, latent)
    m_next = jnp.logical_and(ii >= jj, jnp.logical_and(latent, u + 1 <= n_chunks - 1))
    bias = jnp.concatenate([jnp.where(mk, 0.0, -jnp.inf).astype(F32) for mk in (m_prev, m_cur, m_next)]
                           + [jnp.zeros((n_ctx, BLOCK), F32)], axis=0)
    k_all = jnp.concatenate([kp_ref[0], kc_ref[0], kn_ref[0], kx_ref[0]], axis=0)
    vt_all = jnp.concatenate([vp_ref[0], vc_ref[0], vn_ref[0], vx_ref[0]], axis=1)
    per_kv = Q_HEADS // KV_HEADS
    group_w = per_kv * HEAD_DIM
    kv_w = KV_HEADS * HEAD_DIM
    for g in range(KV_HEADS):
        q_heads = jnp.concatenate([qt_ref[0, g * group_w + r * HEAD_DIM:g * group_w + (r + 1) * HEAD_DIM, :]
                                   for r in range(per_kv)], axis=1)
        pieces = []
        if g > 0:
            pieces.append(jnp.zeros((g * HEAD_DIM, per_kv * BLOCK), BF16))
        pieces.append(q_heads)
        if g < KV_HEADS - 1:
            pieces.append(jnp.zeros((kv_w - (g + 1) * HEAD_DIM, per_kv * BLOCK), BF16))
        s = jnp.dot(k_all, jnp.concatenate(pieces, axis=0), preferred_element_type=F32)
        s = jnp.concatenate([s[:, r * BLOCK:(r + 1) * BLOCK] + bias for r in range(per_kv)], axis=1)
        sink = sink_ref[g]
        mx = jnp.maximum(jnp.max(s, axis=0, keepdims=True), sink)
        p = jnp.exp(s - mx)
        den = jnp.sum(p, axis=0, keepdims=True) + jnp.exp(sink - mx)
        o_t = jnp.dot(vt_all[g * HEAD_DIM:(g + 1) * HEAD_DIM, :], p.astype(BF16),
                      preferred_element_type=F32) / den
        for pair in range(per_kv // 2):
            two = jnp.concatenate([o_t[:, (2 * pair) * BLOCK:(2 * pair + 1) * BLOCK],
                                   o_t[:, (2 * pair + 1) * BLOCK:(2 * pair + 2) * BLOCK]], axis=0)
            c0 = g * group_w + pair * 2 * HEAD_DIM
            o_ref[0, :, c0:c0 + 2 * HEAD_DIM] = two.T.astype(BF16)


def _attention(qt, k, vt, sink, batch, t_total, n_ctx, need_ctx):
    n_chunks = t_total // BLOCK
    ctx_chunks = n_ctx // BLOCK
    u0 = 0 if need_ctx else ctx_chunks
    q_w = Q_HEADS * HEAD_DIM
    kv_w = KV_HEADS * HEAD_DIM
    per_kv = Q_HEADS // KV_HEADS
    assert 2 * HEAD_DIM == LANES and per_kv % 2 == 0
    sink_rows = jnp.repeat(sink.astype(F32).reshape(KV_HEADS, 1, per_kv), BLOCK, axis=2)

    def clipped(n, delta):
        return jnp.clip(n + u0 + delta, ctx_chunks, n_chunks - 1)

    def kwin(delta):
        return pl.BlockSpec((1, BLOCK, kv_w), lambda b, n: (b, clipped(n, delta), 0))

    def vwin(delta):
        return pl.BlockSpec((1, kv_w, BLOCK), lambda b, n: (b, 0, clipped(n, delta)))

    return pl.pallas_call(
        functools.partial(_attention_kernel, u0=u0, n_chunks=n_chunks, ctx_chunks=ctx_chunks, n_ctx=n_ctx),
        out_shape=jax.ShapeDtypeStruct((batch, t_total, q_w), BF16),
        grid=(batch, n_chunks - u0),
        in_specs=[pl.BlockSpec((1, q_w, BLOCK), lambda b, n: (b, 0, n + u0)),
                  kwin(-1), kwin(0), kwin(1),
                  pl.BlockSpec((1, n_ctx, kv_w), lambda b, n: (b, 0, 0)),
                  vwin(-1), vwin(0), vwin(1),
                  pl.BlockSpec((1, kv_w, n_ctx), lambda b, n: (b, 0, 0)),
                  pl.BlockSpec((KV_HEADS, 1, per_kv * BLOCK), lambda b, n: (0, 0, 0))],
        out_specs=pl.BlockSpec((1, BLOCK, q_w), lambda b, n: (b, n + u0, 0)),
        compiler_params=_cparams(2, 40),
        name="attn_core",
    )(qt, k, k, k, k, vt, vt, vt, vt, sink_rows)


def _rope_tables(seq, n_ctx):
    axis_dim = HEAD_DIM // 2
    freqs = axis_dim // 2
    rows = seq // GRID_W
    row_ids = jnp.repeat(jnp.arange(rows), GRID_W).astype(F32)
    col_ids = jnp.tile(jnp.arange(GRID_W), rows).astype(F32)
    inv_freq = ROPE_BASE ** (-jnp.arange(freqs, dtype=F32) * 2.0 / axis_dim)
    ang_r = row_ids[:, None] * inv_freq
    ang_c = col_ids[:, None] * inv_freq
    ang = jnp.concatenate([ang_r, ang_r, ang_c, ang_c], axis=1)
    cos = jnp.cos(ang)
    sin = jnp.sin(ang)
    lane = jnp.arange(HEAD_DIM)
    second_half = (lane % axis_dim) >= freqs
    sa = jnp.where(second_half, sin, 0.0)
    sb = jnp.where(second_half, 0.0, -sin)

    def full(t, ctx_val):
        t = jnp.concatenate([jnp.full((n_ctx, HEAD_DIM), ctx_val, F32), t], axis=0)
        return jnp.tile(t, (1, LANES // HEAD_DIM))

    return full(cos, 1.0), full(sa, 0.0), full(sb, 0.0)


def _swiglu_rows(h, w1_ref, w3_ref, w2_ref, lead, f_chunk):
    f_total = w1_ref.shape[-1]
    out = None
    for f0 in range(0, f_total, f_chunk):
        sl = lead + (slice(None), slice(f0, f0 + f_chunk))
        a = jnp.dot(h, w1_ref[sl], preferred_element_type=F32)
        b = jnp.dot(h, w3_ref[sl], preferred_element_type=F32)
        u = (_silu(a) * b).astype(BF16)
        part = jnp.dot(u, w2_ref[lead + (slice(f0, f0 + f_chunk), slice(None))], preferred_element_type=F32)
        out = part if out is None else out + part
    return out


def _ffn_dense_kernel(x_ref, mod_ref, nw_ref, w1_ref, w3_ref, w2_ref, o_ref, *, f_chunk):
    x = x_ref[...]
    h = _norm_mod(x, nw_ref[...], mod_ref[0, 3:4, :], mod_ref[0, 4:5, :]).astype(BF16)
    f = _swiglu_rows(h, w1_ref, w3_ref, w2_ref, (), f_chunk)
    o_ref[...] = x + mod_ref[0, 5:6, :] * f


def _ffn_dense(xs, mods, nw, w1, w3, w2, tiles):
    m, d = xs.shape
    f = w1.shape[1]
    f_chunk = f // 2 if (f // 2) % LANES == 0 else f
    return pl.pallas_call(
        functools.partial(_ffn_dense_kernel, f_chunk=f_chunk),
        out_shape=jax.ShapeDtypeStruct((m, d), F32),
        grid=(tiles.grid,),
        in_specs=[tiles.x_spec(d), tiles.mod_spec(d),
                  _resident((1, d), lambda i: (0, 0)),
                  _resident((d, f), lambda i: (0, 0)),
                  _resident((d, f), lambda i: (0, 0)),
                  _resident((f, d), lambda i: (0, 0))],
        out_specs=tiles.x_spec(d),
        input_output_aliases={0: 0},
        compiler_params=_cparams(1, 48),
        name="ffn_dense",
    )(xs, mods, nw, w1, w3, w2)


def _moe_router_kernel(x_ref, mod_ref, nw_ref, rt_ref, h_ref, idx_ref, gate_ref):
    h = _norm_mod(x_ref[...], nw_ref[...], mod_ref[0, 3:4, :], mod_ref[0, 4:5, :])
    h_ref[:, 0, :] = h
    logits = lax.dot_general(rt_ref[...], h, (((1,), (1,)), ((), ())),
                             preferred_element_type=F32, precision=HIGHEST)
    e = lax.broadcasted_iota(jnp.int32, logits.shape, 0)
    m1 = jnp.max(logits, axis=0, keepdims=True)
    i1 = jnp.min(jnp.where(logits == m1, e, N_EXPERTS), axis=0, keepdims=True)
    rest = jnp.where(e == i1, -jnp.inf, logits)
    m2 = jnp.max(rest, axis=0, keepdims=True)
    i2 = jnp.min(jnp.where(rest == m2, e, N_EXPERTS), axis=0, keepdims=True)
    t = jnp.exp(m2 - m1)
    idx_ref[...] = jnp.concatenate([i1, i2], axis=0)
    gate_ref[...] = jnp.concatenate([1.0 / (1.0 + t), t / (1.0 + t)], axis=0)


def _moe_router(xs, mods, nw, router_t, tiles):
    m, d = xs.shape
    pair = pl.BlockSpec((TOP_K, ROW_TILE), lambda i: (0, tiles.row(i)))
    return pl.pallas_call(
        _moe_router_kernel,
        out_shape=(jax.ShapeDtypeStruct((m, 1, d), F32),
                   jax.ShapeDtypeStruct((TOP_K, m), jnp.int32),
                   jax.ShapeDtypeStruct((TOP_K, m), F32)),
        grid=(tiles.grid,),
        in_specs=[tiles.x_spec(d), tiles.mod_spec(d),
                  _resident((1, d), lambda i: (0, 0)),
                  _resident((N_EXPERTS, d), lambda i: (0, 0))],
        out_specs=(pl.BlockSpec((ROW_TILE, 1, d), lambda i: (tiles.row(i), 0, 0)), pair, pair),
        compiler_params=_cparams(1, 32),
        name="moe_router",
    )(xs, mods, nw, router_t)


def _moe_experts_kernel(te_ref, nu_ref, rid_ref, rid_next_ref, h_hbm, gate_ref, w1_ref, w3_ref, w2_ref,
                        o_ref, gbuf, sem, *, f_chunk):
    j = pl.program_id(0)
    n = pl.num_programs(0)
    slot = j % 2

    def row_copy(rids, i, slot_):
        return pltpu.make_async_copy(h_hbm.at[rids[0, 0, i]], gbuf.at[slot_, i], sem.at[slot_])

    def gather_start(rids, slot_):
        def body(i, carry):
            row_copy(rids, i, slot_).start()
            return carry
        lax.fori_loop(0, EXPERT_TILE, body, 0)

    @pl.when(j == 0)
    def _():
        gather_start(rid_ref, slot)

    @pl.when(j + 1 < n)
    def _():
        gather_start(rid_next_ref, 1 - slot)

    def wait_body(i, carry):
        row_copy(rid_ref, i, slot).wait()
        return carry
    lax.fori_loop(0, EXPERT_TILE, wait_body, 0)

    @pl.when(j < nu_ref[0])
    def _():
        f = _swiglu_rows(gbuf[slot, :, 0, :].astype(BF16), w1_ref, w3_ref, w2_ref, (0,), f_chunk)
        o_ref[:, 0, :] = gate_ref[...] * f

    @pl.when(j >= nu_ref[0])
    def _():
        o_ref[...] = jnp.zeros_like(o_ref)


def _moe_experts(h, tile_expert, n_used, row_ids, slot_gate, w1, w3, w2):
    m, _, d = h.shape
    n_tiles = tile_expert.shape[0]
    f = w1.shape[2]
    rid3 = row_ids.reshape(n_tiles, 1, EXPERT_TILE)
    grid_spec = pltpu.PrefetchScalarGridSpec(
        num_scalar_prefetch=2,
        grid=(n_tiles,),
        in_specs=[
            pl.BlockSpec((1, 1, EXPERT_TILE), lambda j, te, nu: (j, 0, 0), memory_space=pltpu.SMEM),
            pl.BlockSpec((1, 1, EXPERT_TILE), lambda j, te, nu: (jnp.minimum(j + 1, n_tiles - 1), 0, 0),
                         memory_space=pltpu.SMEM),
            pl.BlockSpec(memory_space=pl.ANY),
            pl.BlockSpec((EXPERT_TILE, 1), lambda j, te, nu: (j, 0)),
            _resident((1, d, f), lambda j, te, nu: (te[j], 0, 0)),
            _resident((1, d, f), lambda j, te, nu: (te[j], 0, 0)),
            _resident((1, f, d), lambda j, te, nu: (te[j], 0, 0)),
        ],
        out_specs=pl.BlockSpec((EXPERT_TILE, 1, d), lambda j, te, nu: (j, 0, 0)),
        scratch_shapes=[pltpu.VMEM((2, EXPERT_TILE, 1, d), F32), pltpu.SemaphoreType.DMA((2,))],
    )
    return pl.pallas_call(
        functools.partial(_moe_experts_kernel, f_chunk=f // 2),
        out_shape=jax.ShapeDtypeStruct((n_tiles * EXPERT_TILE, 1, d), F32),
        grid_spec=grid_spec,
        compiler_params=_cparams(1, 56),
        name="moe_experts",
    )(tile_expert, n_used, rid3, rid3, h, slot_gate, w1, w3, w2)


def _moe_combine_kernel(p0_ref, p1_ref, x_ref, mod_ref, ys_hbm, o_ref, rbuf, sem):
    def row_copy(pref, k, i):
        return pltpu.make_async_copy(ys_hbm.at[pref[0, 0, i]], rbuf.at[k, i], sem.at[k])

    def start_body(i, carry):
        row_copy(p0_ref, 0, i).start()
        row_copy(p1_ref, 1, i).start()
        return carry
    lax.fori_loop(0, ROW_TILE, start_body, 0)

    def wait_body(i, carry):
        row_copy(p0_ref, 0, i).wait()
        row_copy(p1_ref, 1, i).wait()
        return carry
    lax.fori_loop(0, ROW_TILE, wait_body, 0)

    o_ref[...] = x_ref[...] + mod_ref[0, 5:6, :] * (rbuf[0, :, 0, :] + rbuf[1, :, 0, :])


def _moe_combine(xs, mods, ys, pos, tiles):
    m, d = xs.shape
    n_row_tiles = m // ROW_TILE
    p3 = pos.reshape(TOP_K, n_row_tiles, 1, ROW_TILE)
    pspec = pl.BlockSpec((1, 1, ROW_TILE), lambda i: (tiles.row(i), 0, 0), memory_space=pltpu.SMEM)
    return pl.pallas_call(
        _moe_combine_kernel,
        out_shape=jax.ShapeDtypeStruct((m, d), F32),
        grid=(tiles.grid,),
        in_specs=[pspec, pspec, tiles.x_spec(d), tiles.mod_spec(d), pl.BlockSpec(memory_space=pl.ANY)],
        out_specs=tiles.x_spec(d),
        scratch_shapes=[pltpu.VMEM((TOP_K, ROW_TILE, 1, d), F32), pltpu.SemaphoreType.DMA((TOP_K,))],
        input_output_aliases={2: 0},
        compiler_params=_cparams(1, 32),
        name="moe_combine",
    )(p3[0], p3[1], xs, mods, ys)


def _moe_layer(xs, mods, nw, router, w1, w3, w2, tiles, route_tiles):
    m, d = xs.shape
    h, top_i, gates = _moe_router(xs, mods, nw, router.T.astype(F32), route_tiles)

    visited = jnp.repeat((jnp.arange(m // ROW_TILE) % route_tiles.tpb) >= route_tiles.lo, ROW_TILE)
    eid = jnp.where(visited[None, :], top_i, N_EXPERTS).reshape(-1)
    onehot = (eid[:, None] == jnp.arange(N_EXPERTS)[None, :]).astype(jnp.int32)
    csum = jnp.cumsum(onehot, axis=0)
    counts = csum[-1]
    rank = jnp.take_along_axis(csum, jnp.minimum(eid, N_EXPERTS - 1)[:, None], axis=1)[:, 0] - 1
    padded = ((counts + EXPERT_TILE - 1) // EXPERT_TILE) * EXPERT_TILE
    ends = jnp.cumsum(padded)
    starts = ends - padded
    n_slots = TOP_K * route_tiles.grid * ROW_TILE + N_EXPERTS * EXPERT_TILE
    n_tiles = n_slots // EXPERT_TILE
    pos = jnp.where(eid < N_EXPERTS, starts[jnp.minimum(eid, N_EXPERTS - 1)] + rank, n_slots)
    token = jnp.tile(jnp.arange(m, dtype=jnp.int32), TOP_K)
    row_ids = jnp.zeros((n_slots,), jnp.int32).at[pos].set(token, mode="drop")
    slot_gate = jnp.zeros((n_slots,), F32).at[pos].set(gates.reshape(-1), mode="drop")
    tile_start = jnp.arange(n_tiles, dtype=jnp.int32) * EXPERT_TILE
    n_used = (ends[-1] // EXPERT_TILE).astype(jnp.int32)
    tile_expert = jnp.sum(jnp.minimum(tile_start, ends[-1] - 1)[:, None] >= ends[None, :], axis=1)
    tile_expert = jnp.clip(tile_expert, 0, N_EXPERTS - 1).astype(jnp.int32)

    ys = _moe_experts(h, tile_expert, n_used.reshape(1), row_ids, slot_gate.reshape(-1, 1), w1, w3, w2)
    pos = jnp.minimum(pos, n_slots - 1).astype(jnp.int32).reshape(TOP_K, m)
    return _moe_combine(xs, mods, ys, pos, tiles)


def _final_norm_kernel(x_ref, w_ref, o_ref):
    x = x_ref[...]
    o_ref[...] = x * lax.rsqrt(jnp.mean(x * x, axis=-1, keepdims=True) + EPS) * w_ref[...]


def _final_norm(xs, w, tiles):
    d = xs.shape[1]
    return pl.pallas_call(
        _final_norm_kernel,
        out_shape=jax.ShapeDtypeStruct((tiles.grid * ROW_TILE, d), F32),
        grid=(tiles.grid,),
        in_specs=[tiles.x_spec(d), pl.BlockSpec((1, d), lambda i: (0, 0))],
        out_specs=pl.BlockSpec((ROW_TILE, d), lambda i: (i, 0)),
        compiler_params=_cparams(1, 32),
        name="final_norm",
    )(xs, w)


def kernel(x, c, ctx, c_ctx, w_mod, b_mod, norm1_w, norm2_w, ssd_w_in, ssd_conv_w, ssd_conv_b, ssd_dt_bias, ssd_a_log, ssd_d, ssd_norm_w, ssd_w_out, attn_w_qkv, attn_b_qkv, attn_sink, attn_w_o, attn_b_o, ffn_w1, ffn_w3, ffn_w2, moe_router, moe_w1, moe_w3, moe_w2, final_norm_w):
    batch, seq, d = x.shape
    n_ctx = ctx.shape[1]
    t_total = n_ctx + seq
    depth = w_mod.shape[0]
    assert seq % GRID_W == 0 and seq % BLOCK == 0 and n_ctx % ROW_TILE == 0

    all_tiles = _Tiles(batch, t_total, n_ctx, 0)
    lat_tiles = _Tiles(batch, t_total, n_ctx, n_ctx // ROW_TILE)

    xs = jnp.concatenate([ctx, x], axis=1).reshape(batch * t_total, d)
    mods_all = _mod_all(c, c_ctx, w_mod, b_mod)
    cos, sa, sb = _rope_tables(seq, n_ctx)
    q_w = Q_HEADS * HEAD_DIM
    q_scale = jnp.concatenate([jnp.full((q_w,), 1.0 / math.sqrt(HEAD_DIM), F32),
                               jnp.ones((attn_w_qkv.shape[2] - q_w,), F32)])

    for i in range(depth):
        last = i == depth - 1
        j = i // 2
        mods = mods_all[i]
        nw1 = norm1_w[i].reshape(1, d)
        nw2 = norm2_w[i].reshape(1, d)
        upd = lat_tiles if last else all_tiles
        if i % 2 == 0:
            xs = _ssd_mixer(xs, mods, nw1, ssd_w_in[j], ssd_conv_w[j], ssd_conv_b[j], ssd_dt_bias[j],
                            ssd_a_log[j], ssd_d[j], ssd_norm_w[j], ssd_w_out[j],
                            all_tiles, batch, t_total, n_ctx)
        else:
            qt, k, vt = _qkv_rope(xs, mods, nw1, (attn_w_qkv[j] * q_scale).astype(BF16),
                                  (attn_b_qkv[j] * q_scale).reshape(1, -1), cos, sa, sb, all_tiles,
                                  batch, t_total)
            o = _attention(qt, k, vt, attn_sink[j], batch, t_total, n_ctx, not last)
            xs = _proj_residual(xs, o.reshape(batch * t_total, q_w), attn_w_o[j].astype(BF16),
                                attn_b_o[j].reshape(1, d), mods, 2, upd)
        if i % 2 == 0:
            xs = _ffn_dense(xs, mods, nw2, ffn_w1[j].astype(BF16), ffn_w3[j].astype(BF16),
                            ffn_w2[j].astype(BF16), upd)
        else:
            xs = _moe_layer(xs, mods, nw2, moe_router[j], moe_w1[j].astype(BF16), moe_w3[j].astype(BF16),
                            moe_w2[j].astype(BF16), upd, upd)
    out = _final_norm(xs, final_norm_w.reshape(1, d), lat_tiles)
    return out.reshape(batch, seq, d)
```

```python
import functools
import math

import jax
import jax.numpy as jnp
from jax import lax
from jax.experimental import pallas as pl
from jax.experimental.pallas import tpu as pltpu

F32 = jnp.float32
BF16 = jnp.bfloat16
HIGHEST = lax.Precision.HIGHEST

EPS = 1e-6
N_MOD = 6
GRID_W = 64
ROPE_BASE = 10000.0

SSD_HEAD_DIM = 64
SSD_GROUPS = 8
D_STATE = 128
CONV_K = 5
CHUNK = 128
HEADS_PER_GROUP = 4
GROUP_W = HEADS_PER_GROUP * SSD_HEAD_DIM

Q_HEADS = 16
KV_HEADS = 4
HEAD_DIM = 64
BLOCK = 128

N_EXPERTS = 8
TOP_K = 2

ROW_TILE = 256
EXPERT_TILE = 512
LANES = 128
PACK_ROWS_BF16 = 16

MIB = 1024 * 1024


def _cparams(n_axes, vmem_mib):
    return pltpu.CompilerParams(dimension_semantics=("arbitrary",) * n_axes,
                                vmem_limit_bytes=vmem_mib * MIB)


def _resident(block_shape, index_map):
    return pl.BlockSpec(block_shape, index_map, pipeline_mode=pl.Buffered(1))


def _sigmoid(v):
    return 1.0 / (1.0 + jnp.exp(-v))


def _silu(v):
    return v * _sigmoid(v)


def _norm_mod(x, nw, shift, scale):
    ms = jnp.mean(x * x, axis=-1, keepdims=True)
    y = x * lax.rsqrt(ms + EPS) * nw
    return y * (1.0 + scale) + shift


class _Tiles:
    def __init__(self, batch, t_total, n_ctx, lo_tiles):
        assert t_total % ROW_TILE == 0 and n_ctx % ROW_TILE == 0
        self.batch = batch
        self.tpb = t_total // ROW_TILE
        self.ctx_tiles = n_ctx // ROW_TILE
        self.lo = lo_tiles
        self.n_w = self.tpb - lo_tiles
        self.grid = batch * self.n_w

    def split(self, i):
        return i // self.n_w, self.lo + i % self.n_w

    def row(self, i):
        b, w = self.split(i)
        return b * self.tpb + w

    def mod_row(self, i):
        b, w = self.split(i)
        return jnp.where(w < self.ctx_tiles, self.batch, b)

    def x_spec(self, width):
        return pl.BlockSpec((ROW_TILE, width), lambda i: (self.row(i), 0))

    def mod_spec(self, d):
        return pl.BlockSpec((1, N_MOD, d), lambda i: (self.mod_row(i), 0, 0))


def _mod_kernel(c_ref, w_ref, b_ref, o_ref):
    s = _silu(c_ref[...])
    o_ref[0] = jnp.dot(s, w_ref[0], preferred_element_type=F32, precision=HIGHEST) + b_ref[0]


def _mod_all(c, c_ctx, w_mod, b_mod):
    depth, d, n = w_mod.shape
    cc = jnp.concatenate([c, c_ctx[None]], axis=0)
    rows = cc.shape[0]
    tn = 1536
    assert n % tn == 0
    out = pl.pallas_call(
        _mod_kernel,
        out_shape=jax.ShapeDtypeStruct((depth, rows, n), F32),
        grid=(depth, n // tn),
        in_specs=[pl.BlockSpec((rows, d), lambda i, j: (0, 0)),
                  pl.BlockSpec((1, d, tn), lambda i, j: (i, 0, j)),
                  pl.BlockSpec((1, 1, tn), lambda i, j: (i, 0, j))],
        out_specs=pl.BlockSpec((1, rows, tn), lambda i, j: (i, 0, j)),
        compiler_params=_cparams(2, 40),
        name="mod_all",
    )(cc, w_mod, b_mod.reshape(depth, 1, n))
    return out.reshape(depth, rows, N_MOD, d)


def _ssd_inproj_kernel(x_ref, mod_ref, nw_ref, wzx_ref, wdt_ref, zx_ref, dt_ref, *, n_chunk):
    h = _norm_mod(x_ref[...], nw_ref[...], mod_ref[0, 0:1, :], mod_ref[0, 1:2, :]).astype(BF16)
    n = wzx_ref.shape[1]
    for n0 in range(0, n, n_chunk):
        zx_ref[:, n0:n0 + n_chunk] = jnp.dot(
            h, wzx_ref[:, n0:n0 + n_chunk], preferred_element_type=F32).astype(BF16)
    dt_ref[...] = jnp.dot(h, wdt_ref[...], preferred_element_type=F32)


def _ssd_inproj(xs, mods, nw, wzx, wdt, tiles):
    m, d = xs.shape
    n = wzx.shape[1]
    return pl.pallas_call(
        functools.partial(_ssd_inproj_kernel, n_chunk=1536),
        out_shape=(jax.ShapeDtypeStruct((m, n), BF16), jax.ShapeDtypeStruct((m, LANES), F32)),
        grid=(tiles.grid,),
        in_specs=[tiles.x_spec(d), tiles.mod_spec(d),
                  _resident((1, d), lambda i: (0, 0)),
                  _resident((d, n), lambda i: (0, 0)),
                  _resident((d, LANES), lambda i: (0, 0))],
        out_specs=(tiles.x_spec(n), tiles.x_spec(LANES)),
        compiler_params=_cparams(1, 48),
        name="ssd_inproj",
    )(xs, mods, nw, wzx, wdt)


def _dt_prep_kernel(raw_ref, bias_ref, a_ref, dtc_ref, acc_ref, dtr_ref, acr_ref):
    v = raw_ref[...] + bias_ref[...]
    dt = jnp.maximum(v, 0.0) + jnp.log(1.0 + jnp.exp(-jnp.abs(v)))
    a = dt * a_ref[...]
    ii = lax.broadcasted_iota(jnp.int32, (CHUNK, CHUNK), 0)
    jj = lax.broadcasted_iota(jnp.int32, (CHUNK, CHUNK), 1)
    prefix = jnp.dot((ii >= jj).astype(F32), a, preferred_element_type=F32, precision=HIGHEST)
    suffix = jnp.dot((ii <= jj).astype(F32), a, preferred_element_type=F32, precision=HIGHEST)
    col = lax.broadcasted_iota(jnp.int32, (CHUNK, LANES), 1)
    is_fwd = (col % (2 * HEADS_PER_GROUP)) < HEADS_PER_GROUP
    ac = jnp.where(is_fwd, prefix, suffix)
    dtc_ref[...] = dt
    acc_ref[...] = ac
    dtr_ref[0] = dt.T
    acr_ref[0] = ac.T


def _dt_prep(dt_raw, bias, a_neg):
    m = dt_raw.shape[0]
    nchunks = m // CHUNK
    tile = pl.BlockSpec((CHUNK, LANES), lambda i: (i, 0))
    vec = pl.BlockSpec((1, LANES), lambda i: (0, 0))
    sq = pl.BlockSpec((1, LANES, CHUNK), lambda i: (i, 0, 0))
    return pl.pallas_call(
        _dt_prep_kernel,
        out_shape=(jax.ShapeDtypeStruct((m, LANES), F32), jax.ShapeDtypeStruct((m, LANES), F32),
                   jax.ShapeDtypeStruct((nchunks, LANES, CHUNK), F32),
                   jax.ShapeDtypeStruct((nchunks, LANES, CHUNK), F32)),
        grid=(nchunks,),
        in_specs=[tile, vec, vec],
        out_specs=(tile, tile, sq, sq),
        compiler_params=_cparams(1, 32),
        name="ssd_dt_prep",
    )(dt_raw, bias, a_neg)


def _split3(v):
    def top_bits(a):
        return lax.bitcast_convert_type(
            lax.bitcast_convert_type(a, jnp.uint32) & jnp.uint32(0xFFFF0000), F32)

    hi = top_bits(v)
    r1 = v - hi
    mid = top_bits(r1)
    lo = r1 - mid
    return hi.astype(BF16), mid.astype(BF16), lo.astype(BF16)


N_BCAST = 4 * CHUNK + 2 * GROUP_W
PIECE_LANES = 3 * 2 * HEADS_PER_GROUP


def _bcast_selector():
    lane = jnp.arange(LANES)[:, None]
    col = jnp.arange(N_BCAST)[None, :]
    n_q = 2 * HEADS_PER_GROUP
    sel = []
    for d in range(2):
        rel = lane - PIECE_LANES * d
        q = rel % n_q
        tile = (col < 4 * CHUNK) & (col // CHUNK == q)
        acc_exp = (col >= 4 * CHUNK) & (col < 4 * CHUNK + GROUP_W) & ((col - 4 * CHUNK) // SSD_HEAD_DIM == q)
        dt_exp = (col >= 4 * CHUNK + GROUP_W) & (
            (col - 4 * CHUNK - GROUP_W) // SSD_HEAD_DIM + HEADS_PER_GROUP == q)
        in_dir = (rel >= 0) & (rel < PIECE_LANES)
        sel.append(((tile | acc_exp | dt_exp) & in_dir).astype(BF16))
    return jnp.stack(sel)


def _ssd_scan_kernel(z_ref, x_ref, b_ref, c_ref, cw_ref, colp_ref, dtr_ref, acr_ref, e_ref,
                     dsk_ref, nw_ref, o_ref, xc_s, cc_s, bt_s, yf_s, yb_s, sf_s, sb_s, *, n_chunks, ctx_chunks):
    t_total = n_chunks * CHUNK
    halo = PACK_ROWS_BF16
    win = CHUNK + 2 * halo
    pad = CONV_K // 2

    def conv_silu(ref, c, t0, col0, width):
        cur = ref[0, pl.ds(t0, CHUNK), :]
        prev = ref[0, pl.ds(pl.multiple_of(jnp.maximum(t0 - halo, 0), halo), halo), :]
        nxt = ref[0, pl.ds(pl.multiple_of(jnp.minimum(t0 + CHUNK, t_total - halo), halo), halo), :]
        prev_ok = jnp.logical_and(c != 0, c != ctx_chunks)
        next_ok = jnp.logical_and(c != ctx_chunks - 1, c != n_chunks - 1)
        prev = jnp.where(prev_ok, prev, jnp.zeros_like(prev))
        nxt = jnp.where(next_ok, nxt, jnp.zeros_like(nxt))
        w = jnp.concatenate([prev, cur, nxt], axis=0).astype(F32)
        acc = jnp.broadcast_to(cw_ref[0, CONV_K:CONV_K + 1, col0:col0 + width], (CHUNK, width))
        for k in range(CONV_K):
            shifted = w if k == pad else pltpu.roll(w, (pad - k) % win, 0)
            acc = acc + shifted[halo:halo + CHUNK] * cw_ref[0, k:k + 1, col0:col0 + width]
        return _silu(acc)

    def conv_body(c, carry):
        t0 = pl.multiple_of(c * CHUNK, CHUNK)
        xc_s[pl.ds(t0, CHUNK), :] = conv_silu(x_ref, c, t0, 0, GROUP_W).astype(BF16)
        cc_s[pl.ds(t0, CHUNK), :] = conv_silu(c_ref, c, t0, GROUP_W + D_STATE, D_STATE).astype(BF16)
        bt_s[c] = conv_silu(b_ref, c, t0, GROUP_W, D_STATE).T.astype(BF16)
        return carry

    lax.fori_loop(0, n_chunks, conv_body, 0)

    ii = lax.broadcasted_iota(jnp.int32, (CHUNK, CHUNK), 0)
    jj = lax.broadcasted_iota(jnp.int32, (CHUNK, CHUNK), 1)
    lane_w = lax.broadcasted_iota(jnp.int32, (CHUNK, GROUP_W), 1)
    y_s = (yf_s, yb_s)
    st_s = (sf_s, sb_s)
    sf_s[...] = jnp.zeros_like(sf_s)
    sb_s[...] = jnp.zeros_like(sb_s)

    def chunk_step(d, c):
        mask = (ii >= jj) if d == 0 else (ii <= jj)
        last = CHUNK - 1 if d == 0 else 0
        t0 = pl.multiple_of(c * CHUNK, CHUNK)
        xch = xc_s[pl.ds(t0, CHUNK), :]
        cch = cc_s[pl.ds(t0, CHUNK), :]
        bt = bt_s[c]
        dtr = dtr_ref[0, c]
        acr = acr_ref[0, c]
        state = st_s[d][...]

        bc = jnp.dot(colp_ref[0, 0, pl.ds(t0, CHUNK), :], e_ref[d], preferred_element_type=F32)
        acc_exp = bc[:, 4 * CHUNK:4 * CHUNK + GROUP_W]
        dt_exp = bc[:, 4 * CHUNK + GROUP_W:]
        cb = jnp.dot(cch, bt, preferred_element_type=F32)
        ms = []
        xs_ = []
        for r in range(HEADS_PER_GROUP):
            k = 4 * d + r
            seg = bc[:, r * CHUNK:(r + 1) * CHUNK] - acr[k:k + 1, :]
            decay = jnp.exp(jnp.where(mask, seg, -jnp.inf))
            ms.append((cb * decay * dtr[k:k + 1, :]).astype(BF16))
            in_head = jnp.logical_and(lane_w >= r * SSD_HEAD_DIM, lane_w < (r + 1) * SSD_HEAD_DIM)
            xs_.append(jnp.where(in_head, xch, jnp.zeros_like(xch)))
        y = jnp.dot(jnp.concatenate(ms, axis=1), jnp.concatenate(xs_, axis=0), preferred_element_type=F32)
        y = y + jnp.dot(cch, state.astype(BF16), preferred_element_type=F32) * jnp.exp(acc_exp)
        y_s[d][pl.ds(t0, CHUNK), :] = y

        a_last = acc_exp[last:last + 1, :]
        xw = (xch.astype(F32) * (jnp.exp(a_last - acc_exp) * dt_exp)).astype(BF16)
        st_s[d][...] = state * jnp.exp(a_last) + jnp.dot(bt, xw, preferred_element_type=F32)

    def scan_body(s, carry):
        chunk_step(0, s)
        chunk_step(1, jnp.where(s < ctx_chunks, ctx_chunks - 1 - s, n_chunks - 1 - (s - ctx_chunks)))
        return carry

    lax.fori_loop(0, n_chunks, scan_body, 0, unroll=2)

    def out_body(c, carry):
        t0 = pl.multiple_of(c * CHUNK, CHUNK)
        y = (yf_s[pl.ds(t0, CHUNK), :] + yb_s[pl.ds(t0, CHUNK), :]
             + dsk_ref[0] * xc_s[pl.ds(t0, CHUNK), :].astype(F32))
        g = y * _silu(z_ref[0, pl.ds(t0, CHUNK), :].astype(F32))
        g = g * lax.rsqrt(jnp.mean(g * g, axis=-1, keepdims=True) + EPS)
        o_ref[0, pl.ds(t0, CHUNK), :] = (g * nw_ref[0]).astype(BF16)
        return carry

    lax.fori_loop(0, n_chunks, out_body, 0)


def _ssd_scan(zx, cw, colp, dtr, acr, dskip, norm_w, batch, t_total, n_ctx):
    d_inner = SSD_GROUPS * GROUP_W
    n_chunks = t_total // CHUNK
    zx3 = zx.reshape(batch, t_total, zx.shape[-1])
    xoff = d_inner // GROUP_W
    boff = 2 * d_inner // D_STATE
    coff = boff + SSD_GROUPS
    pc = CONV_K + 3
    rows =pl.BlockSpec((1, n_chunks, 2 * HEADS_PER_GROUP, CHUNK), lambda b, g: (b, 0, g, 0))
    vec = pl.BlockSpec((1, 1, GROUP_W), lambda b, g: (g, 0, 0))
    return pl.pallas_call(
        functools.partial(_ssd_scan_kernel, n_chunks=n_chunks, ctx_chunks=n_ctx // CHUNK),
        out_shape=jax.ShapeDtypeStruct((batch, t_total, d_inner), BF16),
        grid=(batch, SSD_GROUPS),
        in_specs=[pl.BlockSpec((1, t_total, GROUP_W), lambda b, g: (b, 0, g)),
                  pl.BlockSpec((1, t_total, GROUP_W), lambda b, g: (b, 0, xoff + g)),
                  pl.BlockSpec((1, t_total, D_STATE), lambda b, g: (b, 0, boff + g)),
                  pl.BlockSpec((1, t_total, D_STATE), lambda b, g: (b, 0, coff + g)),
                  pl.BlockSpec((1, pc, GROUP_W + 2 * D_STATE), lambda b, g: (g, 0, 0)),
                  pl.BlockSpec((1, 1, t_total, LANES), lambda b, g: (b, g, 0, 0)),
                  rows, rows,
                  _resident((2, LANES, N_BCAST), lambda b, g: (0, 0, 0)),
                  vec, vec],
        out_specs=pl.BlockSpec((1, t_total, GROUP_W), lambda b, g: (b, 0, g)),
        scratch_shapes=[pltpu.VMEM((t_total, GROUP_W), BF16),
                        pltpu.VMEM((t_total, D_STATE), BF16),
                        pltpu.VMEM((n_chunks, D_STATE, CHUNK), BF16),
                        pltpu.VMEM((t_total, GROUP_W), F32),
                        pltpu.VMEM((t_total, GROUP_W), F32),
                        pltpu.VMEM((D_STATE, GROUP_W), F32),
                        pltpu.VMEM((D_STATE, GROUP_W), F32)],
        compiler_params=_cparams(2, 56),
        name="ssd_scan",
    )(zx3, zx3, zx3, zx3, cw, colp, dtr, acr, _bcast_selector(), dskip, norm_w)


def _ssd_mixer(xs, mods, nw, w_in, conv_w, conv_b, dt_bias, a_log, d_skip, norm_w, w_out,
               tiles, batch, t_total, n_ctx):
    d = xs.shape[1]
    d_inner = SSD_GROUPS * GROUP_W
    conv_ch = d_inner + 2 * SSD_GROUPS * D_STATE
    heads = SSD_GROUPS * HEADS_PER_GROUP
    n_dt = 2 * heads
    perm = jnp.arange(n_dt).reshape(2, SSD_GROUPS, HEADS_PER_GROUP).transpose(1, 0, 2).reshape(-1)
    wzx = w_in[:, :d_inner + conv_ch].astype(BF16)
    wdt = jnp.zeros((d, LANES), F32).at[:, :n_dt].set(w_in[:, d_inner + conv_ch:][:, perm]).astype(BF16)
    bias = jnp.zeros((1, LANES), F32).at[0, :n_dt].set(dt_bias.reshape(-1)[perm])
    a_neg = jnp.zeros((1, LANES), F32).at[0, :n_dt].set(-jnp.exp(a_log.astype(F32)).reshape(-1)[perm])

    zx, dt_raw = _ssd_inproj(xs, mods, nw, wzx, wdt, tiles)
    dtc, acc, dtr, acr = _dt_prep(dt_raw, bias, a_neg)

    def cols(a):
        return a[:, :n_dt].reshape(batch, t_total, SSD_GROUPS, 2, HEADS_PER_GROUP).transpose(0, 2, 1, 3, 4)

    vals = jnp.concatenate([cols(acc), cols(dtc)], axis=-1)
    colp = jnp.stack(_split3(vals), axis=-2).reshape(batch, SSD_GROUPS, t_total, 2 * PIECE_LANES)
    colp = jnp.pad(colp, ((0, 0), (0, 0), (0, 0), (0, LANES - 2 * PIECE_LANES)))

    n_chunks = t_total // CHUNK
    dtr = dtr.reshape(batch, n_chunks, LANES, CHUNK)
    acr = acr.reshape(batch, n_chunks, LANES, CHUNK)

    def per_group(v):
        gx = v[:, :d_inner].reshape(-1, SSD_GROUPS, GROUP_W)
        gb = v[:, d_inner:d_inner + SSD_GROUPS * D_STATE].reshape(-1, SSD_GROUPS, D_STATE)
        gc = v[:, d_inner + SSD_GROUPS * D_STATE:].reshape(-1, SSD_GROUPS, D_STATE)
        return jnp.concatenate([gx, gb, gc], axis=-1).transpose(1, 0, 2)

    cw = per_group(jnp.concatenate([conv_w, conv_b[None], jnp.zeros((2, conv_ch), F32)], axis=0))
    dsk = jnp.repeat((d_skip[0] + d_skip[1]).astype(F32), SSD_HEAD_DIM).reshape(SSD_GROUPS, 1, GROUP_W)
    gnw = norm_w.astype(F32).reshape(SSD_GROUPS, 1, GROUP_W)

    g = _ssd_scan(zx, cw, colp, dtr, acr, dsk, gnw, batch, t_total, n_ctx)
    return _proj_residual(xs, g.reshape(batch * t_total, d_inner), w_out.astype(BF16),
                          jnp.zeros((1, d), F32), mods, 2, tiles)


def _proj_residual_kernel(x_ref, a_ref, w_ref, b_ref, mod_ref, o_ref, *, gate_row):
    y = jnp.dot(a_ref[...], w_ref[...], preferred_element_type=F32) + b_ref[...]
    o_ref[...] = x_ref[...] + mod_ref[0, gate_row:gate_row + 1, :] * y


def _proj_residual(xs, a, w, b, mods, gate_row, tiles):
    m, d = xs.shape
    k = a.shape[1]
    assert a.shape[0] == tiles.grid * ROW_TILE
    return pl.pallas_call(
        functools.partial(_proj_residual_kernel, gate_row=gate_row),
        out_shape=jax.ShapeDtypeStruct((m, d), F32),
        grid=(tiles.grid,),
        in_specs=[tiles.x_spec(d), pl.BlockSpec((ROW_TILE, k), lambda i: (i, 0)),
                  _resident((k, d), lambda i: (0, 0)),
                  _resident((1, d), lambda i: (0, 0)),
                  tiles.mod_spec(d)],
        out_specs=tiles.x_spec(d),
        input_output_aliases={0: 0},
        compiler_params=_cparams(1, 40),
        name="proj_residual",
    )(xs, a, w, b, mods)


def _transpose_rows(blk):
    return jnp.concatenate([blk[r0:r0 + LANES].T for r0 in range(0, blk.shape[0], LANES)], axis=1)


def _qkv_rope_kernel(x_ref, mod_ref, nw_ref, w_ref, b_ref, cos_ref, sa_ref, sb_ref,
                     qt_ref, k_ref, vt_ref):
    h = _norm_mod(x_ref[...], nw_ref[...], mod_ref[0, 0:1, :], mod_ref[0, 1:2, :]).astype(BF16)
    acc = jnp.dot(h, w_ref[...], preferred_element_type=F32) + b_ref[...]
    cos = cos_ref[...]
    sa = sa_ref[...]
    sb = sb_ref[...]
    half = HEAD_DIM // 4
    q_w = Q_HEADS * HEAD_DIM
    kv_w = KV_HEADS * HEAD_DIM

    def rope(blk):
        return blk * cos + pltpu.roll(blk, half, 1) * sa + pltpu.roll(blk, LANES - half, 1) * sb

    for c0 in range(0, q_w, LANES):
        qt_ref[0, c0:c0 + LANES, :] = _transpose_rows(rope(acc[:, c0:c0 + LANES])).astype(BF16)
    for c0 in range(0, kv_w, LANES):
        k_ref[0, :, c0:c0 + LANES] = rope(acc[:, q_w + c0:q_w + c0 + LANES]).astype(BF16)
        vt_ref[0, c0:c0 + LANES, :] = _transpose_rows(
            acc[:, q_w + kv_w + c0:q_w + kv_w + c0 + LANES]).astype(BF16)


def _qkv_rope(xs, mods, nw, w, b, cos, sa, sb, tiles, batch, t_total):
    m, d = xs.shape
    n = w.shape[1]
    q_w = Q_HEADS * HEAD_DIM
    kv_w = KV_HEADS * HEAD_DIM
    tab = pl.BlockSpec((ROW_TILE, LANES), lambda i: (tiles.split(i)[1], 0))

    def feat_major(width):
        return pl.BlockSpec((1, width, ROW_TILE), lambda i: (tiles.split(i)[0], 0, tiles.split(i)[1]))

    return pl.pallas_call(
        _qkv_rope_kernel,
        out_shape=(jax.ShapeDtypeStruct((batch, q_w, t_total), BF16),
                   jax.ShapeDtypeStruct((batch, t_total, kv_w), BF16),
                   jax.ShapeDtypeStruct((batch, kv_w, t_total), BF16)),
        grid=(tiles.grid,),
        in_specs=[tiles.x_spec(d), tiles.mod_spec(d),
                  _resident((1, d), lambda i: (0, 0)),
                  _resident((d, n), lambda i: (0, 0)),
                  _resident((1, n), lambda i: (0, 0)),
                  tab, tab, tab],
        out_specs=(feat_major(q_w),
                   pl.BlockSpec((1, ROW_TILE, kv_w), lambda i: (tiles.split(i)[0], tiles.split(i)[1], 0)),
                   feat_major(kv_w)),
        compiler_params=_cparams(1, 40),
        name="attn_qkv_rope",
    )(xs, mods, nw, w, b, cos, sa, sb)


def _attention_kernel(qt_ref, kp_ref, kc_ref, kn_ref, kx_ref, vp_ref, vc_ref, vn_ref, vx_ref, sink_ref,
                      o_ref, *, u0, n_chunks, ctx_chunks, n_ctx):
    u = pl.program_id(1) + u0
    jj = lax.broadcasted_iota(jnp.int32, (BLOCK, BLOCK), 0)
    ii = lax.broadcasted_iota(jnp.int32, (BLOCK, BLOCK), 1)
    latent = u >= ctx_chunks
    m_prev = jnp.logical_and(jj >= ii, jnp.logical_and(latent, u - 1 >= ctx_chunks))
    m_cur = jnp.logical_and(ii >= 0, latent)
    m_next = jnp.logical_and(ii >= jj, jnp.logical_and(latent, u + 1 <= n_chunks - 1))
    bias = jnp.concatenate([jnp.where(mk, 0.0, -jnp.inf).astype(F32) for mk in (m_prev, m_cur, m_next)]
                           + [jnp.zeros((n_ctx, BLOCK), F32)], axis=0)
    k_all = jnp.concatenate([kp_ref[0], kc_ref[0], kn_ref[0], kx_ref[0]], axis=0)
    vt_all = jnp.concatenate([vp_ref[0], vc_ref[0], vn_ref[0], vx_ref[0]], axis=1)
    per_kv = Q_HEADS // KV_HEADS
    group_w = per_kv * HEAD_DIM
    kv_w = KV_HEADS * HEAD_DIM
    for g in range(KV_HEADS):
        q_heads = jnp.concatenate([qt_ref[0, g * group_w + r * HEAD_DIM:g * group_w + (r + 1) * HEAD_DIM, :]
                                   for r in range(per_kv)], axis=1)
        pieces = []
        if g > 0:
            pieces.append(jnp.zeros((g * HEAD_DIM, per_kv * BLOCK), BF16))
        pieces.append(q_heads)
        if g < KV_HEADS - 1:
            pieces.append(jnp.zeros((kv_w - (g + 1) * HEAD_DIM, per_kv * BLOCK), BF16))
        s = jnp.dot(k_all, jnp.concatenate(pieces, axis=0), preferred_element_type=F32)
        s = jnp.concatenate([s[:, r * BLOCK:(r + 1) * BLOCK] + bias for r in range(per_kv)], axis=1)
        sink = sink_ref[g]
        mx = jnp.maximum(jnp.max(s, axis=0, keepdims=True), sink)
        p = jnp.exp(s - mx)
        den = jnp.sum(p, axis=0, keepdims=True) + jnp.exp(sink - mx)
        o_t = jnp.dot(vt_all[g * HEAD_DIM:(g + 1) * HEAD_DIM, :], p.astype(BF16),
                      preferred_element_type=F32) / den
        for pair in range(per_kv // 2):
            two = jnp.concatenate([o_t[:, (2 * pair) * BLOCK:(2 * pair + 1) * BLOCK],
                                   o_t[:, (2 * pair + 1) * BLOCK:(2 * pair + 2) * BLOCK]], axis=0)
            c0 = g * group_w + pair * 2 * HEAD_DIM
            o_ref[0, :, c0:c0 + 2 * HEAD_DIM] = two.T.astype(BF16)


def _attention(qt, k, vt, sink, batch, t_total, n_ctx, need_ctx):
    n_chunks = t_total // BLOCK
    ctx_chunks = n_ctx // BLOCK
    u0 = 0 if need_ctx else ctx_chunks
    q_w = Q_HEADS * HEAD_DIM
    kv_w = KV_HEADS * HEAD_DIM
    per_kv = Q_HEADS // KV_HEADS
    assert 2 * HEAD_DIM == LANES and per_kv % 2 == 0
    sink_rows = jnp.repeat(sink.astype(F32).reshape(KV_HEADS, 1, per_kv), BLOCK, axis=2)

    def clipped(n, delta):
        return jnp.clip(n + u0 + delta, ctx_chunks, n_chunks - 1)

    def kwin(delta):
        return pl.BlockSpec((1, BLOCK, kv_w), lambda b, n: (b, clipped(n, delta), 0))

    def vwin(delta):
        return pl.BlockSpec((1, kv_w, BLOCK), lambda b, n: (b, 0, clipped(n, delta)))

    return pl.pallas_call(
        functools.partial(_attention_kernel, u0=u0, n_chunks=n_chunks, ctx_chunks=ctx_chunks, n_ctx=n_ctx),
        out_shape=jax.ShapeDtypeStruct((batch, t_total - u0 * BLOCK, q_w), BF16),
        grid=(batch, n_chunks - u0),
        in_specs=[pl.BlockSpec((1, q_w, BLOCK), lambda b, n: (b, 0, n + u0)),
                  kwin(-1), kwin(0), kwin(1),
                  pl.BlockSpec((1, n_ctx, kv_w), lambda b, n: (b, 0, 0)),
                  vwin(-1), vwin(0), vwin(1),
                  pl.BlockSpec((1, kv_w, n_ctx), lambda b, n: (b, 0, 0)),
                  pl.BlockSpec((KV_HEADS, 1, per_kv * BLOCK), lambda b, n: (0, 0, 0))],
        out_specs=pl.BlockSpec((1, BLOCK, q_w), lambda b, n: (b, n, 0)),
        compiler_params=_cparams(2, 40),
        name="attn_core",
    )(qt, k, k, k, k, vt, vt, vt, vt, sink_rows)


def _rope_tables(seq, n_ctx):
    axis_dim = HEAD_DIM // 2
    freqs = axis_dim // 2
    rows = seq // GRID_W
    row_ids = jnp.repeat(jnp.arange(rows), GRID_W).astype(F32)
    col_ids = jnp.tile(jnp.arange(GRID_W), rows).astype(F32)
    inv_freq = ROPE_BASE ** (-jnp.arange(freqs, dtype=F32) * 2.0 / axis_dim)
    ang_r = row_ids[:, None] * inv_freq
    ang_c = col_ids[:, None] * inv_freq
    ang = jnp.concatenate([ang_r, ang_r, ang_c, ang_c], axis=1)
    cos = jnp.cos(ang)
    sin = jnp.sin(ang)
    lane = jnp.arange(HEAD_DIM)
    second_half = (lane % axis_dim) >= freqs
    sa = jnp.where(second_half, sin, 0.0)
    sb = jnp.where(second_half, 0.0, -sin)

    def full(t, ctx_val):
        t = jnp.concatenate([jnp.full((n_ctx, HEAD_DIM), ctx_val, F32), t], axis=0)
        return jnp.tile(t, (1, LANES // HEAD_DIM))

    return full(cos, 1.0), full(sa, 0.0), full(sb, 0.0)


def _swiglu_rows(h, w1_ref, w3_ref, w2_ref, lead, f_chunk):
    f_total = w1_ref.shape[-1]
    out = None
    for f0 in range(0, f_total, f_chunk):
        sl = lead + (slice(None), slice(f0, f0 + f_chunk))
        a = jnp.dot(h, w1_ref[sl], preferred_element_type=F32)
        b = jnp.dot(h, w3_ref[sl], preferred_element_type=F32)
        u = (_silu(a) * b).astype(BF16)
        part = jnp.dot(u, w2_ref[lead + (slice(f0, f0 + f_chunk), slice(None))], preferred_element_type=F32)
        out = part if out is None else out + part
    return out


def _ffn_dense_kernel(x_ref, mod_ref, nw_ref, w1_ref, w3_ref, w2_ref, o_ref, *, f_chunk):
    x = x_ref[...]
    h = _norm_mod(x, nw_ref[...], mod_ref[0, 3:4, :], mod_ref[0, 4:5, :]).astype(BF16)
    f = _swiglu_rows(h, w1_ref, w3_ref, w2_ref, (), f_chunk)
    o_ref[...] = x + mod_ref[0, 5:6, :] * f


def _ffn_dense(xs, mods, nw, w1, w3, w2, tiles):
    m, d = xs.shape
    f = w1.shape[1]
    f_chunk = f // 2 if (f // 2) % LANES == 0 else f
    return pl.pallas_call(
        functools.partial(_ffn_dense_kernel, f_chunk=f_chunk),
        out_shape=jax.ShapeDtypeStruct((m, d), F32),
        grid=(tiles.grid,),
        in_specs=[tiles.x_spec(d), tiles.mod_spec(d),
                  _resident((1, d), lambda i: (0, 0)),
                  _resident((d, f), lambda i: (0, 0)),
                  _resident((d, f), lambda i: (0, 0)),
                  _resident((f, d), lambda i: (0, 0))],
        out_specs=tiles.x_spec(d),
        input_output_aliases={0: 0},
        compiler_params=_cparams(1, 48),
        name="ffn_dense",
    )(xs, mods, nw, w1, w3, w2)


def _moe_router_kernel(x_ref, mod_ref, nw_ref, rt_ref, h_ref, idx_ref, gate_ref):
    h = _norm_mod(x_ref[...], nw_ref[...], mod_ref[0, 3:4, :], mod_ref[0, 4:5, :])
    h_ref[:, 0, :] = h
    logits = lax.dot_general(rt_ref[...], h, (((1,), (1,)), ((), ())),
                             preferred_element_type=F32, precision=HIGHEST)
    e = lax.broadcasted_iota(jnp.int32, logits.shape, 0)
    m1 = jnp.max(logits, axis=0, keepdims=True)
    i1 = jnp.min(jnp.where(logits == m1, e, N_EXPERTS), axis=0, keepdims=True)
    rest = jnp.where(e == i1, -jnp.inf, logits)
    m2 = jnp.max(rest, axis=0, keepdims=True)
    i2 = jnp.min(jnp.where(rest == m2, e, N_EXPERTS), axis=0, keepdims=True)
    t = jnp.exp(m2 - m1)
    idx_ref[...] = jnp.concatenate([i1, i2], axis=0)
    gate_ref[...] = jnp.concatenate([1.0 / (1.0 + t), t / (1.0 + t)], axis=0)


def _moe_router(xs, mods, nw, router_t, tiles):
    d = xs.shape[1]
    m = tiles.grid * ROW_TILE
    pair = pl.BlockSpec((TOP_K, ROW_TILE), lambda i: (0, i))
    return pl.pallas_call(
        _moe_router_kernel,
        out_shape=(jax.ShapeDtypeStruct((m, 1, d), F32),
                   jax.ShapeDtypeStruct((TOP_K, m), jnp.int32),
                   jax.ShapeDtypeStruct((TOP_K, m), F32)),
        grid=(tiles.grid,),
        in_specs=[tiles.x_spec(d), tiles.mod_spec(d),
                  _resident((1, d), lambda i: (0, 0)),
                  _resident((N_EXPERTS, d), lambda i: (0, 0))],
        out_specs=(pl.BlockSpec((ROW_TILE, 1, d), lambda i: (i, 0, 0)), pair, pair),
        compiler_params=_cparams(1, 32),
        name="moe_router",
    )(xs, mods, nw, router_t)


def _moe_experts_kernel(te_ref, src_ref, src_next_ref, dst_init_ref, dst_prev_ref, dst_ref, h_hbm, gate_ref,
                        w1_ref, w3_ref, w2_ref, y_hbm, gbuf, obuf, gsem, osem, *, f_chunk):
    j = pl.program_id(0)
    n = pl.num_programs(0)
    slot = j % 2
    other = 1 - slot
    tile = EXPERT_TILE

    def gather(ref, i, s):
        return pltpu.make_async_copy(h_hbm.at[ref[0, 0, i]], gbuf.at[s, i], gsem.at[s])

    def scatter(ref, i, s):
        return pltpu.make_async_copy(obuf.at[s, i], y_hbm.at[ref[0, 0, i]], osem.at[s])

    def gather_wait(s):
        pltpu.make_async_copy(h_hbm.at[pl.ds(0, tile)], gbuf.at[s], gsem.at[s]).wait()

    def scatter_wait(s):
        pltpu.make_async_copy(obuf.at[s], y_hbm.at[pl.ds(0, tile)], osem.at[s]).wait()

    @pl.when(j == 0)
    def _():
        obuf[...] = jnp.zeros_like(obuf)

        def body(i, carry):
            gather(src_ref, i, 0).start()
            scatter(dst_init_ref, i, 0).start()
            return carry
        lax.fori_loop(0, tile, body, 0)

    gather_wait(slot)
    for i in range(tile):
        gather(src_next_ref, i, other).start()
    for i in range(tile):
        scatter(dst_prev_ref, i, other).start()
    f = _swiglu_rows(gbuf[slot, :, 0, :].astype(BF16), w1_ref, w3_ref, w2_ref, (0,), f_chunk)
    scatter_wait(slot)
    obuf[slot, :, 0, :] = gate_ref[...] * f

    @pl.when(j == n - 1)
    def _():
        def body(i, carry):
            scatter(dst_ref, i, slot).start()
            return carry
        lax.fori_loop(0, tile, body, 0)
        gather_wait(other)
        scatter_wait(other)
        scatter_wait(slot)


def _moe_experts(h, tile_expert, src_rows, dst_rows, slot_gate, w1, w3, w2, n_out_rows):
    _, _, d = h.shape
    n_tiles = tile_expert.shape[0]
    f = w1.shape[2]
    tile = EXPERT_TILE
    src3 = src_rows.reshape(n_tiles, 1, tile)
    dst3 = dst_rows.reshape(n_tiles + 2, 1, tile)

    def smem(index_map):
        return pl.BlockSpec((1, 1, tile), index_map, memory_space=pltpu.SMEM)

    grid_spec = pltpu.PrefetchScalarGridSpec(
        num_scalar_prefetch=1,
        grid=(n_tiles,),
        in_specs=[
            smem(lambda j, te: (j, 0, 0)),
            smem(lambda j, te: (jnp.minimum(j + 1, n_tiles - 1), 0, 0)),
            smem(lambda j, te: (0, 0, 0)),
            smem(lambda j, te: (j + 1, 0, 0)),
            smem(lambda j, te: (j + 2, 0, 0)),
            pl.BlockSpec(memory_space=pl.ANY),
            pl.BlockSpec((tile, 1), lambda j, te: (j, 0)),
            _resident((1, d, f), lambda j, te: (te[j], 0, 0)),
            _resident((1, d, f), lambda j, te: (te[j], 0, 0)),
            _resident((1, f, d), lambda j, te: (te[j], 0, 0)),
        ],
        out_specs=pl.BlockSpec(memory_space=pl.ANY),
        scratch_shapes=[pltpu.VMEM((2, tile, 1, d), F32), pltpu.VMEM((2, tile, 1, d), F32),
                        pltpu.SemaphoreType.DMA((2,)), pltpu.SemaphoreType.DMA((2,))],
    )
    return pl.pallas_call(
        functools.partial(_moe_experts_kernel, f_chunk=f // 2),
        out_shape=jax.ShapeDtypeStruct((n_out_rows, 1, d), F32),
        grid_spec=grid_spec,
        compiler_params=_cparams(1, 56),
        name="moe_experts",
    )(tile_expert, src3, src3, dst3, dst3, dst3, h, slot_gate, w1, w3, w2)


def _moe_combine_kernel(x_ref, mod_ref, y0_ref, y1_ref, o_ref):
    o_ref[...] = x_ref[...] + mod_ref[0, 5:6, :] * (y0_ref[:, 0, :] + y1_ref[:, 0, :])


def _moe_combine(xs, mods, y, tiles):
    m, d = xs.shape
    k1 = tiles.grid
    return pl.pallas_call(
        _moe_combine_kernel,
        out_shape=jax.ShapeDtypeStruct((m, d), F32),
        grid=(tiles.grid,),
        in_specs=[tiles.x_spec(d), tiles.mod_spec(d),
                  pl.BlockSpec((ROW_TILE, 1, d), lambda i: (i, 0, 0)),
                  pl.BlockSpec((ROW_TILE, 1, d), lambda i: (k1 + i, 0, 0))],
        out_specs=tiles.x_spec(d),
        input_output_aliases={0: 0},
        compiler_params=_cparams(1, 32),
        name="moe_combine",
    )(xs, mods, y, y)


def _moe_layer(xs, mods, nw, router, w1, w3, w2, tiles):
    h, top_i, gates = _moe_router(xs, mods, nw, router.T.astype(F32), tiles)

    tile = EXPERT_TILE
    m = tiles.grid * ROW_TILE
    n_assign = TOP_K * m
    eid = top_i.reshape(-1)
    onehot = (eid[:, None] == jnp.arange(N_EXPERTS)[None, :]).astype(jnp.int32)
    csum = jnp.cumsum(onehot, axis=0)
    counts = csum[-1]
    rank = jnp.take_along_axis(csum, eid[:, None], axis=1)[:, 0] - 1
    padded = ((counts + tile - 1) // tile) * tile
    ends = jnp.cumsum(padded)
    starts = ends - padded
    n_slots = n_assign + N_EXPERTS * tile
    n_tiles = n_slots // tile
    pos = starts[eid] + rank
    slot_assign = jnp.zeros((n_slots,), jnp.int32).at[pos].set(jnp.arange(1, n_assign + 1, dtype=jnp.int32))
    valid = slot_assign > 0
    assign = jnp.maximum(slot_assign - 1, 0)
    src_rows = jnp.where(valid, jnp.where(assign >= m, assign - m, assign), 0)
    spare = n_assign + jnp.cumsum(jnp.logical_not(valid).astype(jnp.int32)) - 1
    dst_rows = jnp.where(valid, assign, spare)
    slot_gate = jnp.where(valid, gates.reshape(-1)[assign], 0.0)
    n_out_rows = n_slots + 2 * tile
    prime_rows = n_slots + jnp.arange(2 * tile, dtype=jnp.int32)
    tile_start = jnp.arange(n_tiles, dtype=jnp.int32) * tile
    tile_expert = jnp.sum(jnp.minimum(tile_start, ends[-1] - 1)[:, None] >= ends[None, :], axis=1)
    tile_expert = jnp.clip(tile_expert, 0, N_EXPERTS - 1).astype(jnp.int32)

    y = _moe_experts(h, tile_expert, src_rows.astype(jnp.int32),
                     jnp.concatenate([prime_rows, dst_rows.astype(jnp.int32)]),
                     slot_gate.reshape(-1, 1), w1, w3, w2, n_out_rows)
    return _moe_combine(xs, mods, y, tiles)


def _final_norm_kernel(x_ref, w_ref, o_ref):
    x = x_ref[...]
    o_ref[...] = x * lax.rsqrt(jnp.mean(x * x, axis=-1, keepdims=True) + EPS) * w_ref[...]


def _final_norm(xs, w, tiles):
    d = xs.shape[1]
    return pl.pallas_call(
        _final_norm_kernel,
        out_shape=jax.ShapeDtypeStruct((tiles.grid * ROW_TILE, d), F32),
        grid=(tiles.grid,),
        in_specs=[tiles.x_spec(d), pl.BlockSpec((1, d), lambda i: (0, 0))],
        out_specs=pl.BlockSpec((ROW_TILE, d), lambda i: (i, 0)),
        compiler_params=_cparams(1, 32),
        name="final_norm",
    )(xs, w)


def kernel(x, c, ctx, c_ctx, w_mod, b_mod, norm1_w, norm2_w, ssd_w_in, ssd_conv_w, ssd_conv_b, ssd_dt_bias, ssd_a_log, ssd_d, ssd_norm_w, ssd_w_out, attn_w_qkv, attn_b_qkv, attn_sink, attn_w_o, attn_b_o, ffn_w1, ffn_w3, ffn_w2, moe_router, moe_w1, moe_w3, moe_w2, final_norm_w):
    batch, seq, d = x.shape
    n_ctx = ctx.shape[1]
    t_total = n_ctx + seq
    depth = w_mod.shape[0]
    assert seq % GRID_W == 0 and seq % BLOCK == 0 and n_ctx % ROW_TILE == 0

    all_tiles = _Tiles(batch, t_total, n_ctx, 0)
    lat_tiles = _Tiles(batch, t_total, n_ctx, n_ctx // ROW_TILE)

    xs = jnp.concatenate([ctx, x], axis=1).reshape(batch * t_total, d)
    mods_all = _mod_all(c, c_ctx, w_mod, b_mod)
    cos, sa, sb = _rope_tables(seq, n_ctx)
    q_w = Q_HEADS * HEAD_DIM
    q_scale = jnp.concatenate([jnp.full((q_w,), 1.0 / math.sqrt(HEAD_DIM), F32),
                               jnp.ones((attn_w_qkv.shape[2] - q_w,), F32)])

    for i in range(depth):
        last = i == depth - 1
        j = i // 2
        mods = mods_all[i]
        nw1 = norm1_w[i].reshape(1, d)
        nw2 = norm2_w[i].reshape(1, d)
        upd = lat_tiles if last else all_tiles
        if i % 2 == 0:
            xs = _ssd_mixer(xs, mods, nw1, ssd_w_in[j], ssd_conv_w[j], ssd_conv_b[j], ssd_dt_bias[j],
                            ssd_a_log[j], ssd_d[j], ssd_norm_w[j], ssd_w_out[j],
                            all_tiles, batch, t_total, n_ctx)
        else:
            qt, k, vt = _qkv_rope(xs, mods, nw1, (attn_w_qkv[j] * q_scale).astype(BF16),
                                  (attn_b_qkv[j] * q_scale).reshape(1, -1), cos, sa, sb, all_tiles,
                                  batch, t_total)
            o = _attention(qt, k, vt, attn_sink[j], batch, t_total, n_ctx, not last)
            xs = _proj_residual(xs, o.reshape(-1, q_w), attn_w_o[j].astype(BF16),
                                attn_b_o[j].reshape(1, d), mods, 2, upd)
        if i % 2 == 0:
            xs = _ffn_dense(xs, mods, nw2, ffn_w1[j].astype(BF16), ffn_w3[j].astype(BF16),
                            ffn_w2[j].astype(BF16), upd)
        else:
            xs = _moe_layer(xs, mods, nw2, moe_router[j], moe_w1[j].astype(BF16), moe_w3[j].astype(BF16),
                            moe_w2[j].astype(BF16), upd)
    out = _final_norm(xs, final_norm_w.reshape(1, d), lat_tiles)
    return out.reshape(batch, seq, d)
```

```python
import functools
import math

import jax
import jax.numpy as jnp
from jax import lax
from jax.experimental import pallas as pl
from jax.experimental.pallas import tpu as pltpu

F32 = jnp.float32
BF16 = jnp.bfloat16
HIGHEST = lax.Precision.HIGHEST

EPS = 1e-6
N_MOD = 6
GRID_W = 64
ROPE_BASE = 10000.0

SSD_HEAD_DIM = 64
SSD_GROUPS = 8
D_STATE = 128
CONV_K = 5
CHUNK = 128
HEADS_PER_GROUP = 4
GROUP_W = HEADS_PER_GROUP * SSD_HEAD_DIM

Q_HEADS = 16
KV_HEADS = 4
HEAD_DIM = 64
BLOCK = 128

N_EXPERTS = 8
TOP_K = 2

ROW_TILE = 256
EXPERT_TILE = 512
LANES = 128
PACK_ROWS_BF16 = 16

MIB = 1024 * 1024


def _cparams(n_axes, vmem_mib):
    return pltpu.CompilerParams(dimension_semantics=("arbitrary",) * n_axes,
                                vmem_limit_bytes=vmem_mib * MIB)


def _resident(block_shape, index_map):
    return pl.BlockSpec(block_shape, index_map, pipeline_mode=pl.Buffered(1))


def _sigmoid(v):
    return 1.0 / (1.0 + jnp.exp(-v))


def _silu(v):
    return v * _sigmoid(v)


def _norm_mod(x, nw, shift, scale):
    ms = jnp.mean(x * x, axis=-1, keepdims=True)
    y = x * lax.rsqrt(ms + EPS) * nw
    return y * (1.0 + scale) + shift


class _Tiles:
    def __init__(self, batch, t_total, n_ctx, lo_tiles):
        assert t_total % ROW_TILE == 0 and n_ctx % ROW_TILE == 0
        self.batch = batch
        self.tpb = t_total // ROW_TILE
        self.ctx_tiles = n_ctx // ROW_TILE
        self.lo = lo_tiles
        self.n_w = self.tpb - lo_tiles
        self.grid = batch * self.n_w

    def split(self, i):
        return i // self.n_w, self.lo + i % self.n_w

    def row(self, i):
        b, w = self.split(i)
        return b * self.tpb + w

    def mod_row(self, i):
        b, w = self.split(i)
        return jnp.where(w < self.ctx_tiles, self.batch, b)

    def x_spec(self, width):
        return pl.BlockSpec((ROW_TILE, width), lambda i: (self.row(i), 0))

    def mod_spec(self, d):
        return pl.BlockSpec((1, N_MOD, d), lambda i: (self.mod_row(i), 0, 0))


def _mod_kernel(c_ref, w_ref, b_ref, o_ref):
    s = _silu(c_ref[...])
    o_ref[0] = jnp.dot(s, w_ref[0], preferred_element_type=F32, precision=HIGHEST) + b_ref[0]


def _mod_all(c, c_ctx, w_mod, b_mod):
    depth, d, n = w_mod.shape
    cc = jnp.concatenate([c, c_ctx[None]], axis=0)
    rows = cc.shape[0]
    tn = 1536
    assert n % tn == 0
    out = pl.pallas_call(
        _mod_kernel,
        out_shape=jax.ShapeDtypeStruct((depth, rows, n), F32),
        grid=(depth, n // tn),
        in_specs=[pl.BlockSpec((rows, d), lambda i, j: (0, 0)),
                  pl.BlockSpec((1, d, tn), lambda i, j: (i, 0, j)),
                  pl.BlockSpec((1, 1, tn), lambda i, j: (i, 0, j))],
        out_specs=pl.BlockSpec((1, rows, tn), lambda i, j: (i, 0, j)),
        compiler_params=_cparams(2, 40),
        name="mod_all",
    )(cc, w_mod, b_mod.reshape(depth, 1, n))
    return out.reshape(depth, rows, N_MOD, d)


def _ssd_inproj_kernel(x_ref, mod_ref, nw_ref, wzx_ref, wdt_ref, zx_ref, dt_ref, *, n_chunk):
    h = _norm_mod(x_ref[...], nw_ref[...], mod_ref[0, 0:1, :], mod_ref[0, 1:2, :]).astype(BF16)
    n = wzx_ref.shape[1]
    for n0 in range(0, n, n_chunk):
        zx_ref[:, n0:n0 + n_chunk] = jnp.dot(
            h, wzx_ref[:, n0:n0 + n_chunk], preferred_element_type=F32).astype(BF16)
    dt_ref[...] = jnp.dot(h, wdt_ref[...], preferred_element_type=F32)


def _ssd_inproj(xs, mods, nw, wzx, wdt, tiles):
    m, d = xs.shape
    n = wzx.shape[1]
    return pl.pallas_call(
        functools.partial(_ssd_inproj_kernel, n_chunk=1536),
        out_shape=(jax.ShapeDtypeStruct((m, n), BF16), jax.ShapeDtypeStruct((m, LANES), F32)),
        grid=(tiles.grid,),
        in_specs=[tiles.x_spec(d), tiles.mod_spec(d),
                  _resident((1, d), lambda i: (0, 0)),
                  _resident((d, n), lambda i: (0, 0)),
                  _resident((d, LANES), lambda i: (0, 0))],
        out_specs=(tiles.x_spec(n), tiles.x_spec(LANES)),
        compiler_params=_cparams(1, 48),
        name="ssd_inproj",
    )(xs, mods, nw, wzx, wdt)


def _dt_prep_kernel(raw_ref, bias_ref, a_ref, dtc_ref, acc_ref, dtr_ref, acr_ref):
    v = raw_ref[...] + bias_ref[...]
    dt = jnp.maximum(v, 0.0) + jnp.log(1.0 + jnp.exp(-jnp.abs(v)))
    a = dt * a_ref[...]
    ii = lax.broadcasted_iota(jnp.int32, (CHUNK, CHUNK), 0)
    jj = lax.broadcasted_iota(jnp.int32, (CHUNK, CHUNK), 1)
    prefix = jnp.dot((ii >= jj).astype(F32), a, preferred_element_type=F32, precision=HIGHEST)
    suffix = jnp.dot((ii <= jj).astype(F32), a, preferred_element_type=F32, precision=HIGHEST)
    col = lax.broadcasted_iota(jnp.int32, (CHUNK, LANES), 1)
    is_fwd = (col % (2 * HEADS_PER_GROUP)) < HEADS_PER_GROUP
    ac = jnp.where(is_fwd, prefix, suffix)
    dtc_ref[...] = dt
    acc_ref[...] = ac
    dtr_ref[0] = dt.T
    acr_ref[0] = ac.T


def _dt_prep(dt_raw, bias, a_neg):
    m = dt_raw.shape[0]
    nchunks = m // CHUNK
    tile = pl.BlockSpec((CHUNK, LANES), lambda i: (i, 0))
    vec = pl.BlockSpec((1, LANES), lambda i: (0, 0))
    sq = pl.BlockSpec((1, LANES, CHUNK), lambda i: (i, 0, 0))
    return pl.pallas_call(
        _dt_prep_kernel,
        out_shape=(jax.ShapeDtypeStruct((m, LANES), F32), jax.ShapeDtypeStruct((m, LANES), F32),
                   jax.ShapeDtypeStruct((nchunks, LANES, CHUNK), F32),
                   jax.ShapeDtypeStruct((nchunks, LANES, CHUNK), F32)),
        grid=(nchunks,),
        in_specs=[tile, vec, vec],
        out_specs=(tile, tile, sq, sq),
        compiler_params=_cparams(1, 32),
        name="ssd_dt_prep",
    )(dt_raw, bias, a_neg)


def _split3(v):
    def top_bits(a):
        return lax.bitcast_convert_type(
            lax.bitcast_convert_type(a, jnp.uint32) & jnp.uint32(0xFFFF0000), F32)

    hi = top_bits(v)
    r1 = v - hi
    mid = top_bits(r1)
    lo = r1 - mid
    return hi.astype(BF16), mid.astype(BF16), lo.astype(BF16)


N_BCAST = 4 * CHUNK + 2 * GROUP_W
PIECE_LANES = 3 * 2 * HEADS_PER_GROUP


def _bcast_selector():
    lane = jnp.arange(LANES)[:, None]
    col = jnp.arange(N_BCAST)[None, :]
    n_q = 2 * HEADS_PER_GROUP
    sel = []
    for d in range(2):
        rel = lane - PIECE_LANES * d
        q = rel % n_q
        tile = (col < 4 * CHUNK) & (col // CHUNK == q)
        acc_exp = (col >= 4 * CHUNK) & (col < 4 * CHUNK + GROUP_W) & ((col - 4 * CHUNK) // SSD_HEAD_DIM == q)
        dt_exp = (col >= 4 * CHUNK + GROUP_W) & (
            (col - 4 * CHUNK - GROUP_W) // SSD_HEAD_DIM + HEADS_PER_GROUP == q)
        in_dir = (rel >= 0) & (rel < PIECE_LANES)
        sel.append(((tile | acc_exp | dt_exp) & in_dir).astype(BF16))
    return jnp.stack(sel)


CONV_HALO = 64
CONV_WIN = CHUNK + 2 * CONV_HALO


def _shift_selector():
    pad = CONV_K // 2
    taps = jnp.asarray([k for k in range(CONV_K) if k != pad])
    row = jnp.arange((CONV_K - 1) * CHUNK)
    src = CONV_HALO + row % CHUNK + taps[row // CHUNK] - pad
    return (src[:, None] == jnp.arange(CONV_WIN)[None, :]).astype(BF16)


def _ssd_scan_kernel(z_ref, x_ref, b_ref, c_ref, cw_ref, colp_ref, dtr_ref, acr_ref, e_ref, shift_ref,
                     dsk_ref, nw_ref, o_ref, xc_s, cc_s, bt_s, yf_s, yb_s, sf_s, sb_s, *, n_chunks, ctx_chunks):
    t_total = n_chunks * CHUNK
    halo = CONV_HALO
    pad = CONV_K // 2
    taps = [k for k in range(CONV_K) if k != pad]

    def window(ref, c, t0):
        cur = ref[0, pl.ds(t0, CHUNK), :]
        prev = ref[0, pl.ds(pl.multiple_of(jnp.maximum(t0 - halo, 0), halo), halo), :]
        nxt = ref[0, pl.ds(pl.multiple_of(jnp.minimum(t0 + CHUNK, t_total - halo), halo), halo), :]
        prev_ok = jnp.logical_and(c != 0, c != ctx_chunks)
        next_ok = jnp.logical_and(c != ctx_chunks - 1, c != n_chunks - 1)
        prev = jnp.where(prev_ok, prev, jnp.zeros_like(prev))
        nxt = jnp.where(next_ok, nxt, jnp.zeros_like(nxt))
        return jnp.concatenate([prev, cur, nxt], axis=0)

    def conv_body(c, carry):
        t0 = pl.multiple_of(c * CHUNK, CHUNK)
        w = jnp.concatenate([window(x_ref, c, t0), window(b_ref, c, t0), window(c_ref, c, t0)], axis=1)
        shifted = jnp.dot(shift_ref[...], w, preferred_element_type=F32)
        acc = cw_ref[0, CONV_K:CONV_K + 1, :] + w[halo:halo + CHUNK].astype(F32) * cw_ref[0, pad:pad + 1, :]
        for n, k in enumerate(taps):
            acc = acc + shifted[n * CHUNK:(n + 1) * CHUNK] * cw_ref[0, k:k + 1, :]
        v = _silu(acc)
        xc_s[pl.ds(t0, CHUNK), :] = v[:, :GROUP_W].astype(BF16)
        bt_s[c] = v[:, GROUP_W:GROUP_W + D_STATE].T.astype(BF16)
        cc_s[pl.ds(t0, CHUNK), :] = v[:, GROUP_W + D_STATE:].astype(BF16)
        return carry

    lax.fori_loop(0, n_chunks, conv_body, 0, unroll=2)

    ii = lax.broadcasted_iota(jnp.int32, (CHUNK, CHUNK), 0)
    jj = lax.broadcasted_iota(jnp.int32, (CHUNK, CHUNK), 1)
    lane_w = lax.broadcasted_iota(jnp.int32, (CHUNK, GROUP_W), 1)
    y_s = (yf_s, yb_s)
    st_s = (sf_s, sb_s)
    sf_s[...] = jnp.zeros_like(sf_s)
    sb_s[...] = jnp.zeros_like(sb_s)

    def chunk_step(d, c):
        mask = (ii >= jj) if d == 0 else (ii <= jj)
        last = CHUNK - 1 if d == 0 else 0
        t0 = pl.multiple_of(c * CHUNK, CHUNK)
        xch = xc_s[pl.ds(t0, CHUNK), :]
        cch = cc_s[pl.ds(t0, CHUNK), :]
        bt = bt_s[c]
        dtr = dtr_ref[0, c]
        acr = acr_ref[0, c]
        state = st_s[d][...]

        bc = jnp.dot(colp_ref[0, 0, pl.ds(t0, CHUNK), :], e_ref[d], preferred_element_type=F32)
        acc_exp = bc[:, 4 * CHUNK:4 * CHUNK + GROUP_W]
        dt_exp = bc[:, 4 * CHUNK + GROUP_W:]
        cb = jnp.dot(cch, bt, preferred_element_type=F32)
        ms = []
        xs_ = []
        for r in range(HEADS_PER_GROUP):
            k = 4 * d + r
            seg = bc[:, r * CHUNK:(r + 1) * CHUNK] - acr[k:k + 1, :]
            decay = jnp.exp(jnp.where(mask, seg, -jnp.inf))
            ms.append((cb * decay * dtr[k:k + 1, :]).astype(BF16))
            in_head = jnp.logical_and(lane_w >= r * SSD_HEAD_DIM, lane_w < (r + 1) * SSD_HEAD_DIM)
            xs_.append(jnp.where(in_head, xch, jnp.zeros_like(xch)))
        y = jnp.dot(jnp.concatenate(ms, axis=1), jnp.concatenate(xs_, axis=0), preferred_element_type=F32)
        y = y + jnp.dot(cch, state.astype(BF16), preferred_element_type=F32) * jnp.exp(acc_exp)
        y_s[d][pl.ds(t0, CHUNK), :] = y

        a_last = acc_exp[last:last + 1, :]
        xw = (xch.astype(F32) * (jnp.exp(a_last - acc_exp) * dt_exp)).astype(BF16)
        st_s[d][...] = state * jnp.exp(a_last) + jnp.dot(bt, xw, preferred_element_type=F32)

    def scan_body(s, carry):
        chunk_step(0, s)
        chunk_step(1, jnp.where(s < ctx_chunks, ctx_chunks - 1 - s, n_chunks - 1 - (s - ctx_chunks)))
        return carry

    lax.fori_loop(0, n_chunks, scan_body, 0, unroll=2)

    def out_body(c, carry):
        t0 = pl.multiple_of(c * CHUNK, CHUNK)
        y = (yf_s[pl.ds(t0, CHUNK), :] + yb_s[pl.ds(t0, CHUNK), :]
             + dsk_ref[0] * xc_s[pl.ds(t0, CHUNK), :].astype(F32))
        g = y * _silu(z_ref[0, pl.ds(t0, CHUNK), :].astype(F32))
        g = g * lax.rsqrt(jnp.mean(g * g, axis=-1, keepdims=True) + EPS)
        o_ref[0, pl.ds(t0, CHUNK), :] = (g * nw_ref[0]).astype(BF16)
        return carry

    lax.fori_loop(0, n_chunks, out_body, 0)


def _ssd_scan(zx, cw, colp, dtr, acr, dskip, norm_w, batch, t_total, n_ctx):
    d_inner = SSD_GROUPS * GROUP_W
    n_chunks = t_total // CHUNK
    zx3 = zx.reshape(batch, t_total, zx.shape[-1])
    xoff = d_inner // GROUP_W
    boff = 2 * d_inner // D_STATE
    coff = boff + SSD_GROUPS
    pc = CONV_K + 3
    rows =pl.BlockSpec((1, n_chunks, 2 * HEADS_PER_GROUP, CHUNK), lambda b, g: (b, 0, g, 0))
    vec = pl.BlockSpec((1, 1, GROUP_W), lambda b, g: (g, 0, 0))
    return pl.pallas_call(
        functools.partial(_ssd_scan_kernel, n_chunks=n_chunks, ctx_chunks=n_ctx // CHUNK),
        out_shape=jax.ShapeDtypeStruct((batch, t_total, d_inner), BF16),
        grid=(batch, SSD_GROUPS),
        in_specs=[pl.BlockSpec((1, t_total, GROUP_W), lambda b, g: (b, 0, g)),
                  pl.BlockSpec((1, t_total, GROUP_W), lambda b, g: (b, 0, xoff + g)),
                  pl.BlockSpec((1, t_total, D_STATE), lambda b, g: (b, 0, boff + g)),
                  pl.BlockSpec((1, t_total, D_STATE), lambda b, g: (b, 0, coff + g)),
                  pl.BlockSpec((1, pc, GROUP_W + 2 * D_STATE), lambda b, g: (g, 0, 0)),
                  pl.BlockSpec((1, 1, t_total, LANES), lambda b, g: (b, g, 0, 0)),
                  rows, rows,
                  _resident((2, LANES, N_BCAST), lambda b, g: (0, 0, 0)),
                  _resident(((CONV_K - 1) * CHUNK, CONV_WIN), lambda b, g: (0, 0)),
                  vec, vec],
        out_specs=pl.BlockSpec((1, t_total, GROUP_W), lambda b, g: (b, 0, g)),
        scratch_shapes=[pltpu.VMEM((t_total, GROUP_W), BF16),
                        pltpu.VMEM((t_total, D_STATE), BF16),
                        pltpu.VMEM((n_chunks, D_STATE, CHUNK), BF16),
                        pltpu.VMEM((t_total, GROUP_W), F32),
                        pltpu.VMEM((t_total, GROUP_W), F32),
                        pltpu.VMEM((D_STATE, GROUP_W), F32),
                        pltpu.VMEM((D_STATE, GROUP_W), F32)],
        compiler_params=_cparams(2, 56),
        name="ssd_scan",
    )(zx3, zx3, zx3, zx3, cw, colp, dtr, acr, _bcast_selector(), _shift_selector(), dskip, norm_w)


def _ssd_mixer(xs, mods, nw, w_in, conv_w, conv_b, dt_bias, a_log, d_skip, norm_w, w_out,
               tiles, batch, t_total, n_ctx):
    d = xs.shape[1]
    d_inner = SSD_GROUPS * GROUP_W
    conv_ch = d_inner + 2 * SSD_GROUPS * D_STATE
    heads = SSD_GROUPS * HEADS_PER_GROUP
    n_dt = 2 * heads
    perm = jnp.arange(n_dt).reshape(2, SSD_GROUPS, HEADS_PER_GROUP).transpose(1, 0, 2).reshape(-1)
    wzx = w_in[:, :d_inner + conv_ch].astype(BF16)
    wdt = jnp.zeros((d, LANES), F32).at[:, :n_dt].set(w_in[:, d_inner + conv_ch:][:, perm]).astype(BF16)
    bias = jnp.zeros((1, LANES), F32).at[0, :n_dt].set(dt_bias.reshape(-1)[perm])
    a_neg = jnp.zeros((1, LANES), F32).at[0, :n_dt].set(-jnp.exp(a_log.astype(F32)).reshape(-1)[perm])

    zx, dt_raw = _ssd_inproj(xs, mods, nw, wzx, wdt, tiles)
    dtc, acc, dtr, acr = _dt_prep(dt_raw, bias, a_neg)

    def cols(a):
        return a[:, :n_dt].reshape(batch, t_total, SSD_GROUPS, 2, HEADS_PER_GROUP).transpose(0, 2, 1, 3, 4)

    vals = jnp.concatenate([cols(acc), cols(dtc)], axis=-1)
    colp = jnp.stack(_split3(vals), axis=-2).reshape(batch, SSD_GROUPS, t_total, 2 * PIECE_LANES)
    colp = jnp.pad(colp, ((0, 0), (0, 0), (0, 0), (0, LANES - 2 * PIECE_LANES)))

    n_chunks = t_total // CHUNK
    dtr = dtr.reshape(batch, n_chunks, LANES, CHUNK)
    acr = acr.reshape(batch, n_chunks, LANES, CHUNK)

    def per_group(v):
        gx = v[:, :d_inner].reshape(-1, SSD_GROUPS, GROUP_W)
        gb = v[:, d_inner:d_inner + SSD_GROUPS * D_STATE].reshape(-1, SSD_GROUPS, D_STATE)
        gc = v[:, d_inner + SSD_GROUPS * D_STATE:].reshape(-1, SSD_GROUPS, D_STATE)
        return jnp.concatenate([gx, gb, gc], axis=-1).transpose(1, 0, 2)

    cw = per_group(jnp.concatenate([conv_w, conv_b[None], jnp.zeros((2, conv_ch), F32)], axis=0))
    dsk = jnp.repeat((d_skip[0] + d_skip[1]).astype(F32), SSD_HEAD_DIM).reshape(SSD_GROUPS, 1, GROUP_W)
    gnw = norm_w.astype(F32).reshape(SSD_GROUPS, 1, GROUP_W)

    g = _ssd_scan(zx, cw, colp, dtr, acr, dsk, gnw, batch, t_total, n_ctx)
    return _proj_residual(xs, g.reshape(batch * t_total, d_inner), w_out.astype(BF16),
                          jnp.zeros((1, d), F32), mods, 2, tiles)


def _proj_residual_kernel(x_ref, a_ref, w_ref, b_ref, mod_ref, o_ref, *, gate_row):
    y = jnp.dot(a_ref[...], w_ref[...], preferred_element_type=F32) + b_ref[...]
    o_ref[...] = x_ref[...] + mod_ref[0, gate_row:gate_row + 1, :] * y


def _proj_residual(xs, a, w, b, mods, gate_row, tiles):
    m, d = xs.shape
    k = a.shape[1]
    assert a.shape[0] == tiles.grid * ROW_TILE
    return pl.pallas_call(
        functools.partial(_proj_residual_kernel, gate_row=gate_row),
        out_shape=jax.ShapeDtypeStruct((m, d), F32),
        grid=(tiles.grid,),
        in_specs=[tiles.x_spec(d), pl.BlockSpec((ROW_TILE, k), lambda i: (i, 0)),
                  _resident((k, d), lambda i: (0, 0)),
                  _resident((1, d), lambda i: (0, 0)),
                  tiles.mod_spec(d)],
        out_specs=tiles.x_spec(d),
        input_output_aliases={0: 0},
        compiler_params=_cparams(1, 40),
        name="proj_residual",
    )(xs, a, w, b, mods)


def _transpose_rows(blk):
    return jnp.concatenate([blk[r0:r0 + LANES].T for r0 in range(0, blk.shape[0], LANES)], axis=1)


def _qkv_rope_kernel(x_ref, mod_ref, nw_ref, w_ref, b_ref, cos_ref, sa_ref, sb_ref,
                     qt_ref, k_ref, vt_ref):
    h = _norm_mod(x_ref[...], nw_ref[...], mod_ref[0, 0:1, :], mod_ref[0, 1:2, :]).astype(BF16)
    acc = jnp.dot(h, w_ref[...], preferred_element_type=F32) + b_ref[...]
    cos = cos_ref[...]
    sa = sa_ref[...]
    sb = sb_ref[...]
    half = HEAD_DIM // 4
    q_w = Q_HEADS * HEAD_DIM
    kv_w = KV_HEADS * HEAD_DIM

    def rope(blk):
        return blk * cos + pltpu.roll(blk, half, 1) * sa + pltpu.roll(blk, LANES - half, 1) * sb

    for c0 in range(0, q_w, LANES):
        qt_ref[0, c0:c0 + LANES, :] = _transpose_rows(rope(acc[:, c0:c0 + LANES])).astype(BF16)
    for c0 in range(0, kv_w, LANES):
        k_ref[0, :, c0:c0 + LANES] = rope(acc[:, q_w + c0:q_w + c0 + LANES]).astype(BF16)
        vt_ref[0, c0:c0 + LANES, :] = _transpose_rows(
            acc[:, q_w + kv_w + c0:q_w + kv_w + c0 + LANES]).astype(BF16)


def _qkv_rope(xs, mods, nw, w, b, cos, sa, sb, tiles, batch, t_total):
    m, d = xs.shape
    n = w.shape[1]
    q_w = Q_HEADS * HEAD_DIM
    kv_w = KV_HEADS * HEAD_DIM
    tab = pl.BlockSpec((ROW_TILE, LANES), lambda i: (tiles.split(i)[1], 0))

    def feat_major(width):
        return pl.BlockSpec((1, width, ROW_TILE), lambda i: (tiles.split(i)[0], 0, tiles.split(i)[1]))

    return pl.pallas_call(
        _qkv_rope_kernel,
        out_shape=(jax.ShapeDtypeStruct((batch, q_w, t_total), BF16),
                   jax.ShapeDtypeStruct((batch, t_total, kv_w), BF16),
                   jax.ShapeDtypeStruct((batch, kv_w, t_total), BF16)),
        grid=(tiles.grid,),
        in_specs=[tiles.x_spec(d), tiles.mod_spec(d),
                  _resident((1, d), lambda i: (0, 0)),
                  _resident((d, n), lambda i: (0, 0)),
                  _resident((1, n), lambda i: (0, 0)),
                  tab, tab, tab],
        out_specs=(feat_major(q_w),
                   pl.BlockSpec((1, ROW_TILE, kv_w), lambda i: (tiles.split(i)[0], tiles.split(i)[1], 0)),
                   feat_major(kv_w)),
        compiler_params=_cparams(1, 40),
        name="attn_qkv_rope",
    )(xs, mods, nw, w, b, cos, sa, sb)


def _attention_kernel(qt_ref, kp_ref, kc_ref, kn_ref, kx_ref, vp_ref, vc_ref, vn_ref, vx_ref, sink_ref,
                      o_ref, *, u0, n_chunks, ctx_chunks, n_ctx):
    u = pl.program_id(1) + u0
    jj = lax.broadcasted_iota(jnp.int32, (BLOCK, BLOCK), 0)
    ii = lax.broadcasted_iota(jnp.int32, (BLOCK, BLOCK), 1)
    latent = u >= ctx_chunks
    m_prev = jnp.logical_and(jj >= ii, jnp.logical_and(latent, u - 1 >= ctx_chunks))
    m_cur = jnp.logical_and(ii >= 0, latent)
    m_next = jnp.logical_and(ii >= jj, jnp.logical_and(latent, u + 1 <= n_chunks - 1))
    bias = jnp.concatenate([jnp.where(mk, 0.0, -jnp.inf).astype(F32) for mk in (m_prev, m_cur, m_next)]
                           + [jnp.zeros((n_ctx, BLOCK), F32)], axis=0)
    k_all = jnp.concatenate([kp_ref[0], kc_ref[0], kn_ref[0], kx_ref[0]], axis=0)
    vt_all = jnp.concatenate([vp_ref[0], vc_ref[0], vn_ref[0], vx_ref[0]], axis=1)
    per_kv = Q_HEADS // KV_HEADS
    group_w = per_kv * HEAD_DIM
    kv_w = KV_HEADS * HEAD_DIM
    for g in range(KV_HEADS):
        q_heads = jnp.concatenate([qt_ref[0, g * group_w + r * HEAD_DIM:g * group_w + (r + 1) * HEAD_DIM, :]
                                   for r in range(per_kv)], axis=1)
        pieces = []
        if g > 0:
            pieces.append(jnp.zeros((g * HEAD_DIM, per_kv * BLOCK), BF16))
        pieces.append(q_heads)
        if g < KV_HEADS - 1:
            pieces.append(jnp.zeros((kv_w - (g + 1) * HEAD_DIM, per_kv * BLOCK), BF16))
        s = jnp.dot(k_all, jnp.concatenate(pieces, axis=0), preferred_element_type=F32)
        s = jnp.concatenate([s[:, r * BLOCK:(r + 1) * BLOCK] + bias for r in range(per_kv)], axis=1)
        sink = sink_ref[g]
        mx = jnp.maximum(jnp.max(s, axis=0, keepdims=True), sink)
        p = jnp.exp(s - mx)
        den = jnp.sum(p, axis=0, keepdims=True) + jnp.exp(sink - mx)
        o_t = jnp.dot(vt_all[g * HEAD_DIM:(g + 1) * HEAD_DIM, :], p.astype(BF16),
                      preferred_element_type=F32) / den
        for pair in range(per_kv // 2):
            two = jnp.concatenate([o_t[:, (2 * pair) * BLOCK:(2 * pair + 1) * BLOCK],
                                   o_t[:, (2 * pair + 1) * BLOCK:(2 * pair + 2) * BLOCK]], axis=0)
            c0 = g * group_w + pair * 2 * HEAD_DIM
            o_ref[0, :, c0:c0 + 2 * HEAD_DIM] = two.T.astype(BF16)


def _attention(qt, k, vt, sink, batch, t_total, n_ctx, need_ctx):
    n_chunks = t_total // BLOCK
    ctx_chunks = n_ctx // BLOCK
    u0 = 0 if need_ctx else ctx_chunks
    q_w = Q_HEADS * HEAD_DIM
    kv_w = KV_HEADS * HEAD_DIM
    per_kv = Q_HEADS // KV_HEADS
    assert 2 * HEAD_DIM == LANES and per_kv % 2 == 0
    sink_rows = jnp.repeat(sink.astype(F32).reshape(KV_HEADS, 1, per_kv), BLOCK, axis=2)

    def clipped(n, delta):
        return jnp.clip(n + u0 + delta, ctx_chunks, n_chunks - 1)

    def kwin(delta):
        return pl.BlockSpec((1, BLOCK, kv_w), lambda b, n: (b, clipped(n, delta), 0))

    def vwin(delta):
        return pl.BlockSpec((1, kv_w, BLOCK), lambda b, n: (b, 0, clipped(n, delta)))

    return pl.pallas_call(
        functools.partial(_attention_kernel, u0=u0, n_chunks=n_chunks, ctx_chunks=ctx_chunks, n_ctx=n_ctx),
        out_shape=jax.ShapeDtypeStruct((batch, t_total - u0 * BLOCK, q_w), BF16),
        grid=(batch, n_chunks - u0),
        in_specs=[pl.BlockSpec((1, q_w, BLOCK), lambda b, n: (b, 0, n + u0)),
                  kwin(-1), kwin(0), kwin(1),
                  pl.BlockSpec((1, n_ctx, kv_w), lambda b, n: (b, 0, 0)),
                  vwin(-1), vwin(0), vwin(1),
                  pl.BlockSpec((1, kv_w, n_ctx), lambda b, n: (b, 0, 0)),
                  pl.BlockSpec((KV_HEADS, 1, per_kv * BLOCK), lambda b, n: (0, 0, 0))],
        out_specs=pl.BlockSpec((1, BLOCK, q_w), lambda b, n: (b, n, 0)),
        compiler_params=_cparams(2, 40),
        name="attn_core",
    )(qt, k, k, k, k, vt, vt, vt, vt, sink_rows)


def _rope_tables(seq, n_ctx):
    axis_dim = HEAD_DIM // 2
    freqs = axis_dim // 2
    rows = seq // GRID_W
    row_ids = jnp.repeat(jnp.arange(rows), GRID_W).astype(F32)
    col_ids = jnp.tile(jnp.arange(GRID_W), rows).astype(F32)
    inv_freq = ROPE_BASE ** (-jnp.arange(freqs, dtype=F32) * 2.0 / axis_dim)
    ang_r = row_ids[:, None] * inv_freq
    ang_c = col_ids[:, None] * inv_freq
    ang = jnp.concatenate([ang_r, ang_r, ang_c, ang_c], axis=1)
    cos = jnp.cos(ang)
    sin = jnp.sin(ang)
    lane = jnp.arange(HEAD_DIM)
    second_half = (lane % axis_dim) >= freqs
    sa = jnp.where(second_half, sin, 0.0)
    sb = jnp.where(second_half, 0.0, -sin)

    def full(t, ctx_val):
        t = jnp.concatenate([jnp.full((n_ctx, HEAD_DIM), ctx_val, F32), t], axis=0)
        return jnp.tile(t, (1, LANES // HEAD_DIM))

    return full(cos, 1.0), full(sa, 0.0), full(sb, 0.0)


def _swiglu_rows(h, w1_ref, w3_ref, w2_ref, lead, f_chunk, after_first_chunk=None):
    f_total = w1_ref.shape[-1]
    out = None
    for f0 in range(0, f_total, f_chunk):
        sl = lead + (slice(None), slice(f0, f0 + f_chunk))
        a = jnp.dot(h, w1_ref[sl], preferred_element_type=F32)
        b = jnp.dot(h, w3_ref[sl], preferred_element_type=F32)
        u = (_silu(a) * b).astype(BF16)
        part = jnp.dot(u, w2_ref[lead + (slice(f0, f0 + f_chunk), slice(None))], preferred_element_type=F32)
        out = part if out is None else out + part
        if f0 == 0 and after_first_chunk is not None:
            after_first_chunk()
    return out


def _ffn_dense_kernel(x_ref, mod_ref, nw_ref, w1_ref, w3_ref, w2_ref, o_ref, *, f_chunk):
    x = x_ref[...]
    h = _norm_mod(x, nw_ref[...], mod_ref[0, 3:4, :], mod_ref[0, 4:5, :]).astype(BF16)
    f = _swiglu_rows(h, w1_ref, w3_ref, w2_ref, (), f_chunk)
    o_ref[...] = x + mod_ref[0, 5:6, :] * f


def _ffn_dense(xs, mods, nw, w1, w3, w2, tiles):
    m, d = xs.shape
    f = w1.shape[1]
    f_chunk = f // 2 if (f // 2) % LANES == 0 else f
    return pl.pallas_call(
        functools.partial(_ffn_dense_kernel, f_chunk=f_chunk),
        out_shape=jax.ShapeDtypeStruct((m, d), F32),
        grid=(tiles.grid,),
        in_specs=[tiles.x_spec(d), tiles.mod_spec(d),
                  _resident((1, d), lambda i: (0, 0)),
                  _resident((d, f), lambda i: (0, 0)),
                  _resident((d, f), lambda i: (0, 0)),
                  _resident((f, d), lambda i: (0, 0))],
        out_specs=tiles.x_spec(d),
        input_output_aliases={0: 0},
        compiler_params=_cparams(1, 48),
        name="ffn_dense",
    )(xs, mods, nw, w1, w3, w2)


ROW_BLOCKS = 8


def _rows_to_tiles(ref, lead, val):
    rows = val.shape[0]
    for s in range(ROW_BLOCKS):
        ref[lead + (pl.ds(s, rows, stride=ROW_BLOCKS), slice(None))] = val[:, s * LANES:(s + 1) * LANES]


def _tiles_to_rows(ref, lead, rows):
    return jnp.concatenate([ref[lead + (pl.ds(s, rows, stride=ROW_BLOCKS), slice(None))]
                            for s in range(ROW_BLOCKS)], axis=1)


def _moe_router_kernel(x_ref, mod_ref, nw_ref, rt_ref, h_ref, idx_ref, gate_ref):
    h = _norm_mod(x_ref[...], nw_ref[...], mod_ref[0, 3:4, :], mod_ref[0, 4:5, :])
    _rows_to_tiles(h_ref, (), h)
    logits = lax.dot_general(rt_ref[...], h, (((1,), (1,)), ((), ())),
                             preferred_element_type=F32, precision=HIGHEST)
    e = lax.broadcasted_iota(jnp.int32, logits.shape, 0)
    m1 = jnp.max(logits, axis=0, keepdims=True)
    i1 = jnp.min(jnp.where(logits == m1, e, N_EXPERTS), axis=0, keepdims=True)
    rest = jnp.where(e == i1, -jnp.inf, logits)
    m2 = jnp.max(rest, axis=0, keepdims=True)
    i2 = jnp.min(jnp.where(rest == m2, e, N_EXPERTS), axis=0, keepdims=True)
    t = jnp.exp(m2 - m1)
    idx_ref[...] = jnp.concatenate([i1, i2], axis=0)
    gate_ref[...] = jnp.concatenate([1.0 / (1.0 + t), t / (1.0 + t)], axis=0)


def _moe_router(xs, mods, nw, router_t, tiles):
    d = xs.shape[1]
    m = tiles.grid * ROW_TILE
    pair = pl.BlockSpec((TOP_K, ROW_TILE), lambda i: (0, i))
    return pl.pallas_call(
        _moe_router_kernel,
        out_shape=(jax.ShapeDtypeStruct((m * ROW_BLOCKS, LANES), F32),
                   jax.ShapeDtypeStruct((TOP_K, m), jnp.int32),
                   jax.ShapeDtypeStruct((TOP_K, m), F32)),
        grid=(tiles.grid,),
        in_specs=[tiles.x_spec(d), tiles.mod_spec(d),
                  _resident((1, d), lambda i: (0, 0)),
                  _resident((N_EXPERTS, d), lambda i: (0, 0))],
        out_specs=(pl.BlockSpec((ROW_TILE * ROW_BLOCKS, LANES), lambda i: (i, 0)), pair, pair),
        compiler_params=_cparams(1, 32),
        name="moe_router",
    )(xs, mods, nw, router_t)


def _moe_experts_kernel(te_ref, src_ref, src_next_ref, dst_init_ref, dst_prev_ref, dst_ref, h_hbm, gate_ref,
                        w1_ref, w3_ref, w2_ref, y_hbm, gbuf, obuf, gsem, osem, *, f_chunk):
    j = pl.program_id(0)
    n = pl.num_programs(0)
    slot = j % 2
    other = 1 - slot
    tile = EXPERT_TILE

    def tile_rows(idx):
        return pl.ds(pl.multiple_of(idx, ROW_BLOCKS), ROW_BLOCKS)

    def gather(ref, i, s):
        return pltpu.make_async_copy(h_hbm.at[tile_rows(ref[0, 0, i])],
                                     gbuf.at[s, pl.ds(i * ROW_BLOCKS, ROW_BLOCKS)], gsem.at[s])

    def scatter(ref, i, s):
        return pltpu.make_async_copy(obuf.at[s, pl.ds(i * ROW_BLOCKS, ROW_BLOCKS)],
                                     y_hbm.at[tile_rows(ref[0, 0, i])], osem.at[s])

    def gather_wait(s):
        pltpu.make_async_copy(h_hbm.at[pl.ds(0, tile * ROW_BLOCKS)], gbuf.at[s], gsem.at[s]).wait()

    def scatter_wait(s):
        pltpu.make_async_copy(obuf.at[s], y_hbm.at[pl.ds(0, tile * ROW_BLOCKS)], osem.at[s]).wait()

    @pl.when(j == 0)
    def _():
        obuf[...] = jnp.zeros_like(obuf)

        def body(i, carry):
            gather(src_ref, i, 0).start()
            scatter(dst_init_ref, i, 0).start()
            return carry
        lax.fori_loop(0, tile, body, 0)

    gather_wait(slot)
    x = _tiles_to_rows(gbuf, (slot,), tile).astype(BF16)
    for i in range(tile):
        gather(src_next_ref, i, other).start()
    for i in range(tile):
        scatter(dst_prev_ref, i, other).start()
    f = _swiglu_rows(x, w1_ref, w3_ref, w2_ref, (0,), f_chunk)
    scatter_wait(slot)
    _rows_to_tiles(obuf, (slot,), gate_ref[...] * f)

    @pl.when(j == n - 1)
    def _():
        def body(i, carry):
            scatter(dst_ref, i, slot).start()
            return carry
        lax.fori_loop(0, tile, body, 0)
        gather_wait(other)
        scatter_wait(other)
        scatter_wait(slot)


def _moe_experts(h, tile_expert, src_rows, dst_rows, slot_gate, w1, w3, w2, n_out_rows):
    d = ROW_BLOCKS * LANES
    assert w1.shape[1] == d
    n_tiles = tile_expert.shape[0]
    f = w1.shape[2]
    tile = EXPERT_TILE
    src3 = src_rows.reshape(n_tiles, 1, tile)
    dst3 = dst_rows.reshape(n_tiles + 2, 1, tile)

    def smem(index_map):
        return pl.BlockSpec((1, 1, tile), index_map, memory_space=pltpu.SMEM)

    grid_spec = pltpu.PrefetchScalarGridSpec(
        num_scalar_prefetch=1,
        grid=(n_tiles,),
        in_specs=[
            smem(lambda j, te: (j, 0, 0)),
            smem(lambda j, te: (jnp.minimum(j + 1, n_tiles - 1), 0, 0)),
            smem(lambda j, te: (0, 0, 0)),
            smem(lambda j, te: (j + 1, 0, 0)),
            smem(lambda j, te: (j + 2, 0, 0)),
            pl.BlockSpec(memory_space=pl.ANY),
            pl.BlockSpec((tile, 1), lambda j, te: (j, 0)),
            _resident((1, d, f), lambda j, te: (te[j], 0, 0)),
            _resident((1, d, f), lambda j, te: (te[j], 0, 0)),
            _resident((1, f, d), lambda j, te: (te[j], 0, 0)),
        ],
        out_specs=pl.BlockSpec(memory_space=pl.ANY),
        scratch_shapes=[pltpu.VMEM((2, tile * ROW_BLOCKS, LANES), F32),
                        pltpu.VMEM((2, tile * ROW_BLOCKS, LANES), F32),
                        pltpu.SemaphoreType.DMA((2,)), pltpu.SemaphoreType.DMA((2,))],
    )
    return pl.pallas_call(
        functools.partial(_moe_experts_kernel, f_chunk=f // 2),
        out_shape=jax.ShapeDtypeStruct((n_out_rows * ROW_BLOCKS, LANES), F32),
        grid_spec=grid_spec,
        compiler_params=_cparams(1, 56),
        name="moe_experts",
    )(tile_expert, src3, src3, dst3, dst3, dst3, h, slot_gate, w1, w3, w2)


def _moe_combine_kernel(x_ref, mod_ref, y0_ref, y1_ref, o_ref):
    y = _tiles_to_rows(y0_ref, (), ROW_TILE) + _tiles_to_rows(y1_ref, (), ROW_TILE)
    o_ref[...] = x_ref[...] + mod_ref[0, 5:6, :] * y


def _moe_combine(xs, mods, y, tiles):
    m, d = xs.shape
    k1 = tiles.grid
    return pl.pallas_call(
        _moe_combine_kernel,
        out_shape=jax.ShapeDtypeStruct((m, d), F32),
        grid=(tiles.grid,),
        in_specs=[tiles.x_spec(d), tiles.mod_spec(d),
                  pl.BlockSpec((ROW_TILE * ROW_BLOCKS, LANES), lambda i: (i, 0)),
                  pl.BlockSpec((ROW_TILE * ROW_BLOCKS, LANES), lambda i: (k1 + i, 0))],
        out_specs=tiles.x_spec(d),
        input_output_aliases={0: 0},
        compiler_params=_cparams(1, 32),
        name="moe_combine",
    )(xs, mods, y, y)


def _moe_layer(xs, mods, nw, router, w1, w3, w2, tiles):
    h, top_i, gates = _moe_router(xs, mods, nw, router.T.astype(F32), tiles)

    tile = EXPERT_TILE
    m = tiles.grid * ROW_TILE
    n_assign = TOP_K * m
    eid = top_i.reshape(-1)
    onehot = (eid[:, None] == jnp.arange(N_EXPERTS)[None, :]).astype(jnp.int32)
    csum = jnp.cumsum(onehot, axis=0)
    counts = csum[-1]
    rank = jnp.take_along_axis(csum, eid[:, None], axis=1)[:, 0] - 1
    padded = ((counts + tile - 1) // tile) * tile
    ends = jnp.cumsum(padded)
    starts = ends - padded
    n_slots = n_assign + N_EXPERTS * tile
    n_tiles = n_slots // tile
    pos = starts[eid] + rank
    slot_assign = jnp.zeros((n_slots,), jnp.int32).at[pos].set(jnp.arange(1, n_assign + 1, dtype=jnp.int32))
    valid = slot_assign > 0
    assign = jnp.maximum(slot_assign - 1, 0)
    src_rows = jnp.where(valid, jnp.where(assign >= m, assign - m, assign), 0)
    spare = n_assign + jnp.cumsum(jnp.logical_not(valid).astype(jnp.int32)) - 1
    dst_rows = jnp.where(valid, assign, spare)
    slot_gate = jnp.where(valid, gates.reshape(-1)[assign], 0.0)
    n_out_rows = n_slots + 2 * tile
    prime_rows = n_slots + jnp.arange(2 * tile, dtype=jnp.int32)
    tile_start = jnp.arange(n_tiles, dtype=jnp.int32) * tile
    tile_expert = jnp.sum(jnp.minimum(tile_start, ends[-1] - 1)[:, None] >= ends[None, :], axis=1)
    tile_expert = jnp.clip(tile_expert, 0, N_EXPERTS - 1).astype(jnp.int32)

    y = _moe_experts(h, tile_expert, src_rows.astype(jnp.int32) * ROW_BLOCKS,
                     jnp.concatenate([prime_rows, dst_rows.astype(jnp.int32)]) * ROW_BLOCKS,
                     slot_gate.reshape(-1, 1), w1, w3, w2, n_out_rows)
    return _moe_combine(xs, mods, y, tiles)


def _final_norm_kernel(x_ref, w_ref, o_ref):
    x = x_ref[...]
    o_ref[...] = x * lax.rsqrt(jnp.mean(x * x, axis=-1, keepdims=True) + EPS) * w_ref[...]


def _final_norm(xs, w, tiles):
    d = xs.shape[1]
    return pl.pallas_call(
        _final_norm_kernel,
        out_shape=jax.ShapeDtypeStruct((tiles.grid * ROW_TILE, d), F32),
        grid=(tiles.grid,),
        in_specs=[tiles.x_spec(d), pl.BlockSpec((1, d), lambda i: (0, 0))],
        out_specs=pl.BlockSpec((ROW_TILE, d), lambda i: (i, 0)),
        compiler_params=_cparams(1, 32),
        name="final_norm",
    )(xs, w)


def kernel(x, c, ctx, c_ctx, w_mod, b_mod, norm1_w, norm2_w, ssd_w_in, ssd_conv_w, ssd_conv_b, ssd_dt_bias, ssd_a_log, ssd_d, ssd_norm_w, ssd_w_out, attn_w_qkv, attn_b_qkv, attn_sink, attn_w_o, attn_b_o, ffn_w1, ffn_w3, ffn_w2, moe_router, moe_w1, moe_w3, moe_w2, final_norm_w):
    batch, seq, d = x.shape
    n_ctx = ctx.shape[1]
    t_total = n_ctx + seq
    depth = w_mod.shape[0]
    assert seq % GRID_W == 0 and seq % BLOCK == 0 and n_ctx % ROW_TILE == 0

    all_tiles = _Tiles(batch, t_total, n_ctx, 0)
    lat_tiles = _Tiles(batch, t_total, n_ctx, n_ctx // ROW_TILE)

    xs = jnp.concatenate([ctx, x], axis=1).reshape(batch * t_total, d)
    mods_all = _mod_all(c, c_ctx, w_mod, b_mod)
    cos, sa, sb = _rope_tables(seq, n_ctx)
    q_w = Q_HEADS * HEAD_DIM
    q_scale = jnp.concatenate([jnp.full((q_w,), 1.0 / math.sqrt(HEAD_DIM), F32),
                               jnp.ones((attn_w_qkv.shape[2] - q_w,), F32)])

    for i in range(depth):
        last = i == depth - 1
        j = i // 2
        mods = mods_all[i]
        nw1 = norm1_w[i].reshape(1, d)
        nw2 = norm2_w[i].reshape(1, d)
        upd = lat_tiles if last else all_tiles
        if i % 2 == 0:
            xs = _ssd_mixer(xs, mods, nw1, ssd_w_in[j], ssd_conv_w[j], ssd_conv_b[j], ssd_dt_bias[j],
                            ssd_a_log[j], ssd_d[j], ssd_norm_w[j], ssd_w_out[j],
                            all_tiles, batch, t_total, n_ctx)
        else:
            qt, k, vt = _qkv_rope(xs, mods, nw1, (attn_w_qkv[j] * q_scale).astype(BF16),
                                  (attn_b_qkv[j] * q_scale).reshape(1, -1), cos, sa, sb, all_tiles,
                                  batch, t_total)
            o = _attention(qt, k, vt, attn_sink[j], batch, t_total, n_ctx, not last)
            xs = _proj_residual(xs, o.reshape(-1, q_w), attn_w_o[j].astype(BF16),
                                attn_b_o[j].reshape(1, d), mods, 2, upd)
        if i % 2 == 0:
            xs = _ffn_dense(xs, mods, nw2, ffn_w1[j].astype(BF16), ffn_w3[j].astype(BF16),
                            ffn_w2[j].astype(BF16), upd)
        else:
            xs = _moe_layer(xs, mods, nw2, moe_router[j], moe_w1[j].astype(BF16), moe_w3[j].astype(BF16),
                            moe_w2[j].astype(BF16), upd)
    out = _final_norm(xs, final_norm_w.reshape(1, d), lat_tiles)
    return out.reshape(batch, seq, d)
```

```python
import functools
import math

import jax
import jax.numpy as jnp
from jax import lax
from jax.experimental import pallas as pl
from jax.experimental.pallas import tpu as pltpu

F32 = jnp.float32
BF16 = jnp.bfloat16
HIGHEST = lax.Precision.HIGHEST

EPS = 1e-6
N_MOD = 6
GRID_W = 64
ROPE_BASE = 10000.0

SSD_HEAD_DIM = 64
SSD_GROUPS = 8
D_STATE = 128
CONV_K = 5
CHUNK = 128
HEADS_PER_GROUP = 4
GROUP_W = HEADS_PER_GROUP * SSD_HEAD_DIM

Q_HEADS = 16
KV_HEADS = 4
HEAD_DIM = 64
BLOCK = 128

N_EXPERTS = 8
TOP_K = 2

ROW_TILE = 256
EXPERT_TILE = 512
LANES = 128
PACK_ROWS_BF16 = 16

MIB = 1024 * 1024


def _cparams(n_axes, vmem_mib):
    return pltpu.CompilerParams(dimension_semantics=("arbitrary",) * n_axes,
                                vmem_limit_bytes=vmem_mib * MIB)


def _resident(block_shape, index_map):
    return pl.BlockSpec(block_shape, index_map, pipeline_mode=pl.Buffered(1))


def _sigmoid(v):
    return 1.0 / (1.0 + jnp.exp(-v))


def _silu(v):
    return v * _sigmoid(v)


def _norm_mod(x, nw, shift, scale):
    ms = jnp.mean(x * x, axis=-1, keepdims=True)
    y = x * lax.rsqrt(ms + EPS) * nw
    return y * (1.0 + scale) + shift


class _Tiles:
    def __init__(self, batch, t_total, n_ctx, lo_tiles):
        assert t_total % ROW_TILE == 0 and n_ctx % ROW_TILE == 0
        self.batch = batch
        self.tpb = t_total // ROW_TILE
        self.ctx_tiles = n_ctx // ROW_TILE
        self.lo = lo_tiles
        self.n_w = self.tpb - lo_tiles
        self.grid = batch * self.n_w

    def split(self, i):
        return i // self.n_w, self.lo + i % self.n_w

    def row(self, i):
        b, w = self.split(i)
        return b * self.tpb + w

    def mod_row(self, i):
        b, w = self.split(i)
        return jnp.where(w < self.ctx_tiles, self.batch, b)

    def x_spec(self, width):
        return pl.BlockSpec((ROW_TILE, width), lambda i: (self.row(i), 0))

    def mod_spec(self, d):
        return pl.BlockSpec((1, N_MOD, d), lambda i: (self.mod_row(i), 0, 0))


def _mod_kernel(c_ref, w_ref, b_ref, o_ref):
    s = _silu(c_ref[...])
    o_ref[0] = jnp.dot(s, w_ref[0], preferred_element_type=F32, precision=HIGHEST) + b_ref[0]


def _mod_all(c, c_ctx, w_mod, b_mod):
    depth, d, n = w_mod.shape
    cc = jnp.concatenate([c, c_ctx[None]], axis=0)
    rows = cc.shape[0]
    tn = 1536
    assert n % tn == 0
    out = pl.pallas_call(
        _mod_kernel,
        out_shape=jax.ShapeDtypeStruct((depth, rows, n), F32),
        grid=(depth, n // tn),
        in_specs=[pl.BlockSpec((rows, d), lambda i, j: (0, 0)),
                  pl.BlockSpec((1, d, tn), lambda i, j: (i, 0, j)),
                  pl.BlockSpec((1, 1, tn), lambda i, j: (i, 0, j))],
        out_specs=pl.BlockSpec((1, rows, tn), lambda i, j: (i, 0, j)),
        compiler_params=_cparams(2, 40),
        name="mod_all",
    )(cc, w_mod, b_mod.reshape(depth, 1, n))
    return out.reshape(depth, rows, N_MOD, d)


def _ssd_inproj_kernel(x_ref, mod_ref, nw_ref, wzx_ref, wdt_ref, zx_ref, dt_ref, *, n_chunk):
    h = _norm_mod(x_ref[...], nw_ref[...], mod_ref[0, 0:1, :], mod_ref[0, 1:2, :]).astype(BF16)
    n = wzx_ref.shape[1]
    for n0 in range(0, n, n_chunk):
        zx_ref[:, n0:n0 + n_chunk] = jnp.dot(
            h, wzx_ref[:, n0:n0 + n_chunk], preferred_element_type=F32).astype(BF16)
    dt_ref[...] = jnp.dot(h, wdt_ref[...], preferred_element_type=F32)


def _ssd_inproj(xs, mods, nw, wzx, wdt, tiles):
    m, d = xs.shape
    n = wzx.shape[1]
    return pl.pallas_call(
        functools.partial(_ssd_inproj_kernel, n_chunk=1536),
        out_shape=(jax.ShapeDtypeStruct((m, n), BF16), jax.ShapeDtypeStruct((m, LANES), F32)),
        grid=(tiles.grid,),
        in_specs=[tiles.x_spec(d), tiles.mod_spec(d),
                  _resident((1, d), lambda i: (0, 0)),
                  _resident((d, n), lambda i: (0, 0)),
                  _resident((d, LANES), lambda i: (0, 0))],
        out_specs=(tiles.x_spec(n), tiles.x_spec(LANES)),
        compiler_params=_cparams(1, 48),
        name="ssd_inproj",
    )(xs, mods, nw, wzx, wdt)


def _dt_prep_kernel(raw_ref, bias_ref, a_ref, dtc_ref, acc_ref, dtr_ref, acr_ref):
    ii = lax.broadcasted_iota(jnp.int32, (CHUNK, CHUNK), 0)
    jj = lax.broadcasted_iota(jnp.int32, (CHUNK, CHUNK), 1)
    lower = (ii >= jj).astype(F32)
    upper = (ii <= jj).astype(F32)
    col = lax.broadcasted_iota(jnp.int32, (CHUNK, LANES), 1)
    is_fwd = (col % (2 * HEADS_PER_GROUP)) < HEADS_PER_GROUP
    for c in range(raw_ref.shape[0] // CHUNK):
        rows = pl.ds(c * CHUNK, CHUNK)
        v = raw_ref[rows, :] + bias_ref[...]
        dt = jnp.maximum(v, 0.0) + jnp.log(1.0 + jnp.exp(-jnp.abs(v)))
        a = dt * a_ref[...]
        prefix = jnp.dot(lower, a, preferred_element_type=F32, precision=HIGHEST)
        suffix = jnp.dot(upper, a, preferred_element_type=F32, precision=HIGHEST)
        ac = jnp.where(is_fwd, prefix, suffix)
        dtc_ref[rows, :] = dt
        acc_ref[rows, :] = ac
        dtr_ref[c] = dt.T
        acr_ref[c] = ac.T


DT_PREP_CHUNKS = 4


def _dt_prep(dt_raw, bias, a_neg):
    m = dt_raw.shape[0]
    nchunks = m // CHUNK
    per = DT_PREP_CHUNKS
    assert nchunks % per == 0
    tile = pl.BlockSpec((per * CHUNK, LANES), lambda i: (i, 0))
    vec = pl.BlockSpec((1, LANES), lambda i: (0, 0))
    sq = pl.BlockSpec((per, LANES, CHUNK), lambda i: (i, 0, 0))
    return pl.pallas_call(
        _dt_prep_kernel,
        out_shape=(jax.ShapeDtypeStruct((m, LANES), F32), jax.ShapeDtypeStruct((m, LANES), F32),
                   jax.ShapeDtypeStruct((nchunks, LANES, CHUNK), F32),
                   jax.ShapeDtypeStruct((nchunks, LANES, CHUNK), F32)),
        grid=(nchunks // per,),
        in_specs=[tile, vec, vec],
        out_specs=(tile, tile, sq, sq),
        compiler_params=_cparams(1, 32),
        name="ssd_dt_prep",
    )(dt_raw, bias, a_neg)


def _split3(v):
    def top_bits(a):
        return lax.bitcast_convert_type(
            lax.bitcast_convert_type(a, jnp.uint32) & jnp.uint32(0xFFFF0000), F32)

    hi = top_bits(v)
    r1 = v - hi
    mid = top_bits(r1)
    lo = r1 - mid
    return hi.astype(BF16), mid.astype(BF16), lo.astype(BF16)


N_BCAST = 4 * CHUNK + 2 * GROUP_W
PIECE_LANES = 3 * 2 * HEADS_PER_GROUP


def _bcast_selector():
    lane = jnp.arange(LANES)[:, None]
    col = jnp.arange(N_BCAST)[None, :]
    n_q = 2 * HEADS_PER_GROUP
    sel = []
    for d in range(2):
        rel = lane - PIECE_LANES * d
        q = rel % n_q
        tile = (col < 4 * CHUNK) & (col // CHUNK == q)
        acc_exp = (col >= 4 * CHUNK) & (col < 4 * CHUNK + GROUP_W) & ((col - 4 * CHUNK) // SSD_HEAD_DIM == q)
        dt_exp = (col >= 4 * CHUNK + GROUP_W) & (
            (col - 4 * CHUNK - GROUP_W) // SSD_HEAD_DIM + HEADS_PER_GROUP == q)
        in_dir = (rel >= 0) & (rel < PIECE_LANES)
        sel.append(((tile | acc_exp | dt_exp) & in_dir).astype(BF16))
    return jnp.stack(sel)


CONV_HALO = 64
CONV_WIN = CHUNK + 2 * CONV_HALO


def _shift_selector():
    pad = CONV_K // 2
    taps = jnp.asarray([k for k in range(CONV_K) if k != pad])
    row = jnp.arange((CONV_K - 1) * CHUNK)
    src = CONV_HALO + row % CHUNK + taps[row // CHUNK] - pad
    return (src[:, None] == jnp.arange(CONV_WIN)[None, :]).astype(BF16)


def _ssd_scan_kernel(z_ref, x_ref, b_ref, c_ref, cw_ref, colp_ref, dtr_ref, acr_ref, e_ref, shift_ref,
                     dsk_ref, nw_ref, o_ref, xc_s, cc_s, bt_s, yf_s, yb_s, sf_s, sb_s, *, n_chunks, ctx_chunks):
    t_total = n_chunks * CHUNK
    halo = CONV_HALO
    pad = CONV_K // 2
    taps = [k for k in range(CONV_K) if k != pad]

    def window(ref, c, t0):
        cur = ref[0, pl.ds(t0, CHUNK), :]
        prev = ref[0, pl.ds(pl.multiple_of(jnp.maximum(t0 - halo, 0), halo), halo), :]
        nxt = ref[0, pl.ds(pl.multiple_of(jnp.minimum(t0 + CHUNK, t_total - halo), halo), halo), :]
        prev_ok = jnp.logical_and(c != 0, c != ctx_chunks)
        next_ok = jnp.logical_and(c != ctx_chunks - 1, c != n_chunks - 1)
        prev = jnp.where(prev_ok, prev, jnp.zeros_like(prev))
        nxt = jnp.where(next_ok, nxt, jnp.zeros_like(nxt))
        return jnp.concatenate([prev, cur, nxt], axis=0)

    def conv_body(c, carry):
        t0 = pl.multiple_of(c * CHUNK, CHUNK)
        w = jnp.concatenate([window(x_ref, c, t0), window(b_ref, c, t0), window(c_ref, c, t0)], axis=1)
        shifted = jnp.dot(shift_ref[...], w, preferred_element_type=F32)
        acc = cw_ref[0, CONV_K:CONV_K + 1, :] + w[halo:halo + CHUNK].astype(F32) * cw_ref[0, pad:pad + 1, :]
        for n, k in enumerate(taps):
            acc = acc + shifted[n * CHUNK:(n + 1) * CHUNK] * cw_ref[0, k:k + 1, :]
        v = _silu(acc)
        xc_s[pl.ds(t0, CHUNK), :] = v[:, :GROUP_W].astype(BF16)
        bt_s[c] = v[:, GROUP_W:GROUP_W + D_STATE].T.astype(BF16)
        cc_s[pl.ds(t0, CHUNK), :] = v[:, GROUP_W + D_STATE:].astype(BF16)
        return carry

    lax.fori_loop(0, n_chunks, conv_body, 0, unroll=2)

    ii = lax.broadcasted_iota(jnp.int32, (CHUNK, CHUNK), 0)
    jj = lax.broadcasted_iota(jnp.int32, (CHUNK, CHUNK), 1)
    lane_w = lax.broadcasted_iota(jnp.int32, (CHUNK, GROUP_W), 1)
    y_s = (yf_s, yb_s)
    st_s = (sf_s, sb_s)
    sf_s[...] = jnp.zeros_like(sf_s)
    sb_s[...] = jnp.zeros_like(sb_s)

    def chunk_step(d, c):
        mask = (ii >= jj) if d == 0 else (ii <= jj)
        last = CHUNK - 1 if d == 0 else 0
        t0 = pl.multiple_of(c * CHUNK, CHUNK)
        xch = xc_s[pl.ds(t0, CHUNK), :]
        cch = cc_s[pl.ds(t0, CHUNK), :]
        bt = bt_s[c]
        dtr = dtr_ref[0, c]
        acr = acr_ref[0, c]
        state = st_s[d][...]

        bc = jnp.dot(colp_ref[0, 0, pl.ds(t0, CHUNK), :], e_ref[d], preferred_element_type=F32)
        acc_exp = bc[:, 4 * CHUNK:4 * CHUNK + GROUP_W]
        dt_exp = bc[:, 4 * CHUNK + GROUP_W:]
        cb = jnp.dot(cch, bt, preferred_element_type=F32)
        ms = []
        xs_ = []
        for r in range(HEADS_PER_GROUP):
            k = 4 * d + r
            seg = bc[:, r * CHUNK:(r + 1) * CHUNK] - acr[k:k + 1, :]
            decay = jnp.exp(jnp.where(mask, seg, -jnp.inf))
            ms.append((cb * decay * dtr[k:k + 1, :]).astype(BF16))
            in_head = jnp.logical_and(lane_w >= r * SSD_HEAD_DIM, lane_w < (r + 1) * SSD_HEAD_DIM)
            xs_.append(jnp.where(in_head, xch, jnp.zeros_like(xch)))
        y = jnp.dot(jnp.concatenate(ms, axis=1), jnp.concatenate(xs_, axis=0), preferred_element_type=F32)
        y = y + jnp.dot(cch, state.astype(BF16), preferred_element_type=F32) * jnp.exp(acc_exp)
        y_s[d][pl.ds(t0, CHUNK), :] = y

        a_last = acc_exp[last:last + 1, :]
        xw = (xch.astype(F32) * (jnp.exp(a_last - acc_exp) * dt_exp)).astype(BF16)
        st_s[d][...] = state * jnp.exp(a_last) + jnp.dot(bt, xw, preferred_element_type=F32)

    def scan_body(s, carry):
        chunk_step(0, s)
        chunk_step(1, jnp.where(s < ctx_chunks, ctx_chunks - 1 - s, n_chunks - 1 - (s - ctx_chunks)))
        return carry

    lax.fori_loop(0, n_chunks, scan_body, 0, unroll=2)

    def out_body(c, carry):
        t0 = pl.multiple_of(c * CHUNK, CHUNK)
        y = (yf_s[pl.ds(t0, CHUNK), :] + yb_s[pl.ds(t0, CHUNK), :]
             + dsk_ref[0] * xc_s[pl.ds(t0, CHUNK), :].astype(F32))
        g = y * _silu(z_ref[0, pl.ds(t0, CHUNK), :].astype(F32))
        g = g * lax.rsqrt(jnp.mean(g * g, axis=-1, keepdims=True) + EPS)
        o_ref[0, pl.ds(t0, CHUNK), :] = (g * nw_ref[0]).astype(BF16)
        return carry

    lax.fori_loop(0, n_chunks, out_body, 0)


def _ssd_scan(zx, cw, colp, dtr, acr, dskip, norm_w, batch, t_total, n_ctx):
    d_inner = SSD_GROUPS * GROUP_W
    n_chunks = t_total // CHUNK
    zx3 = zx.reshape(batch, t_total, zx.shape[-1])
    xoff = d_inner // GROUP_W
    boff = 2 * d_inner // D_STATE
    coff = boff + SSD_GROUPS
    pc = CONV_K + 3
    rows =pl.BlockSpec((1, n_chunks, 2 * HEADS_PER_GROUP, CHUNK), lambda b, g: (b, 0, g, 0))
    vec = pl.BlockSpec((1, 1, GROUP_W), lambda b, g: (g, 0, 0))
    return pl.pallas_call(
        functools.partial(_ssd_scan_kernel, n_chunks=n_chunks, ctx_chunks=n_ctx // CHUNK),
        out_shape=jax.ShapeDtypeStruct((batch, t_total, d_inner), BF16),
        grid=(batch, SSD_GROUPS),
        in_specs=[pl.BlockSpec((1, t_total, GROUP_W), lambda b, g: (b, 0, g)),
                  pl.BlockSpec((1, t_total, GROUP_W), lambda b, g: (b, 0, xoff + g)),
                  pl.BlockSpec((1, t_total, D_STATE), lambda b, g: (b, 0, boff + g)),
                  pl.BlockSpec((1, t_total, D_STATE), lambda b, g: (b, 0, coff + g)),
                  pl.BlockSpec((1, pc, GROUP_W + 2 * D_STATE), lambda b, g: (g, 0, 0)),
                  pl.BlockSpec((1, 1, t_total, LANES), lambda b, g: (b, g, 0, 0)),
                  rows, rows,
                  _resident((2, LANES, N_BCAST), lambda b, g: (0, 0, 0)),
                  _resident(((CONV_K - 1) * CHUNK, CONV_WIN), lambda b, g: (0, 0)),
                  vec, vec],
        out_specs=pl.BlockSpec((1, t_total, GROUP_W), lambda b, g: (b, 0, g)),
        scratch_shapes=[pltpu.VMEM((t_total, GROUP_W), BF16),
                        pltpu.VMEM((t_total, D_STATE), BF16),
                        pltpu.VMEM((n_chunks, D_STATE, CHUNK), BF16),
                        pltpu.VMEM((t_total, GROUP_W), F32),
                        pltpu.VMEM((t_total, GROUP_W), F32),
                        pltpu.VMEM((D_STATE, GROUP_W), F32),
                        pltpu.VMEM((D_STATE, GROUP_W), F32)],
        compiler_params=_cparams(2, 56),
        name="ssd_scan",
    )(zx3, zx3, zx3, zx3, cw, colp, dtr, acr, _bcast_selector(), _shift_selector(), dskip, norm_w)


def _ssd_mixer(xs, mods, nw, w_in, conv_w, conv_b, dt_bias, a_log, d_skip, norm_w,
               tiles, batch, t_total, n_ctx):
    d = xs.shape[1]
    d_inner = SSD_GROUPS * GROUP_W
    conv_ch = d_inner + 2 * SSD_GROUPS * D_STATE
    heads = SSD_GROUPS * HEADS_PER_GROUP
    n_dt = 2 * heads
    perm = jnp.arange(n_dt).reshape(2, SSD_GROUPS, HEADS_PER_GROUP).transpose(1, 0, 2).reshape(-1)
    wzx = w_in[:, :d_inner + conv_ch].astype(BF16)
    wdt = jnp.zeros((d, LANES), F32).at[:, :n_dt].set(w_in[:, d_inner + conv_ch:][:, perm]).astype(BF16)
    bias = jnp.zeros((1, LANES), F32).at[0, :n_dt].set(dt_bias.reshape(-1)[perm])
    a_neg = jnp.zeros((1, LANES), F32).at[0, :n_dt].set(-jnp.exp(a_log.astype(F32)).reshape(-1)[perm])

    zx, dt_raw = _ssd_inproj(xs, mods, nw, wzx, wdt, tiles)
    dtc, acc, dtr, acr = _dt_prep(dt_raw, bias, a_neg)

    def cols(a):
        return a[:, :n_dt].reshape(batch, t_total, SSD_GROUPS, 2, HEADS_PER_GROUP).transpose(0, 2, 1, 3, 4)

    vals = jnp.concatenate([cols(acc), cols(dtc)], axis=-1)
    colp = jnp.stack(_split3(vals), axis=-2).reshape(batch, SSD_GROUPS, t_total, 2 * PIECE_LANES)
    colp = jnp.pad(colp, ((0, 0), (0, 0), (0, 0), (0, LANES - 2 * PIECE_LANES)))

    n_chunks = t_total // CHUNK
    dtr = dtr.reshape(batch, n_chunks, LANES, CHUNK)
    acr = acr.reshape(batch, n_chunks, LANES, CHUNK)

    def per_group(v):
        gx = v[:, :d_inner].reshape(-1, SSD_GROUPS, GROUP_W)
        gb = v[:, d_inner:d_inner + SSD_GROUPS * D_STATE].reshape(-1, SSD_GROUPS, D_STATE)
        gc = v[:, d_inner + SSD_GROUPS * D_STATE:].reshape(-1, SSD_GROUPS, D_STATE)
        return jnp.concatenate([gx, gb, gc], axis=-1).transpose(1, 0, 2)

    cw = per_group(jnp.concatenate([conv_w, conv_b[None], jnp.zeros((2, conv_ch), F32)], axis=0))
    dsk = jnp.repeat((d_skip[0] + d_skip[1]).astype(F32), SSD_HEAD_DIM).reshape(SSD_GROUPS, 1, GROUP_W)
    gnw = norm_w.astype(F32).reshape(SSD_GROUPS, 1, GROUP_W)

    g = _ssd_scan(zx, cw, colp, dtr, acr, dsk, gnw, batch, t_total, n_ctx)
    return g.reshape(batch * t_total, d_inner)


def _transpose_rows(blk):
    return jnp.concatenate([blk[r0:r0 + LANES].T for r0 in range(0, blk.shape[0], LANES)], axis=1)


def _qkv_rope_kernel(x_ref, mod_ref, nw_ref, w_ref, b_ref, cos_ref, sa_ref, sb_ref,
                     qt_ref, k_ref, vt_ref):
    h = _norm_mod(x_ref[...], nw_ref[...], mod_ref[0, 0:1, :], mod_ref[0, 1:2, :]).astype(BF16)
    acc = jnp.dot(h, w_ref[...], preferred_element_type=F32) + b_ref[...]
    cos = cos_ref[...]
    sa = sa_ref[...]
    sb = sb_ref[...]
    half = HEAD_DIM // 4
    q_w = Q_HEADS * HEAD_DIM
    kv_w = KV_HEADS * HEAD_DIM

    def rope(blk):
        return blk * cos + pltpu.roll(blk, half, 1) * sa + pltpu.roll(blk, LANES - half, 1) * sb

    for c0 in range(0, q_w, LANES):
        qt_ref[0, c0:c0 + LANES, :] = _transpose_rows(rope(acc[:, c0:c0 + LANES])).astype(BF16)
    for c0 in range(0, kv_w, LANES):
        k_ref[0, :, c0:c0 + LANES] = rope(acc[:, q_w + c0:q_w + c0 + LANES]).astype(BF16)
        vt_ref[0, c0:c0 + LANES, :] = _transpose_rows(
            acc[:, q_w + kv_w + c0:q_w + kv_w + c0 + LANES]).astype(BF16)


def _qkv_rope(xs, mods, nw, w, b, cos, sa, sb, tiles, batch, t_total):
    m, d = xs.shape
    n = w.shape[1]
    q_w = Q_HEADS * HEAD_DIM
    kv_w = KV_HEADS * HEAD_DIM
    tab = pl.BlockSpec((ROW_TILE, LANES), lambda i: (tiles.split(i)[1], 0))

    def feat_major(width):
        return pl.BlockSpec((1, width, ROW_TILE), lambda i: (tiles.split(i)[0], 0, tiles.split(i)[1]))

    return pl.pallas_call(
        _qkv_rope_kernel,
        out_shape=(jax.ShapeDtypeStruct((batch, q_w, t_total), BF16),
                   jax.ShapeDtypeStruct((batch, t_total, kv_w), BF16),
                   jax.ShapeDtypeStruct((batch, kv_w, t_total), BF16)),
        grid=(tiles.grid,),
        in_specs=[tiles.x_spec(d), tiles.mod_spec(d),
                  _resident((1, d), lambda i: (0, 0)),
                  _resident((d, n), lambda i: (0, 0)),
                  _resident((1, n), lambda i: (0, 0)),
                  tab, tab, tab],
        out_specs=(feat_major(q_w),
                   pl.BlockSpec((1, ROW_TILE, kv_w), lambda i: (tiles.split(i)[0], tiles.split(i)[1], 0)),
                   feat_major(kv_w)),
        compiler_params=_cparams(1, 40),
        name="attn_qkv_rope",
    )(xs, mods, nw, w, b, cos, sa, sb)


def _attention_kernel(qt_ref, kp_ref, kc_ref, kn_ref, kx_ref, vp_ref, vc_ref, vn_ref, vx_ref, sink_ref,
                      o_ref, *, u0, n_chunks, ctx_chunks, n_ctx):
    u = pl.program_id(1) + u0
    jj = lax.broadcasted_iota(jnp.int32, (BLOCK, BLOCK), 0)
    ii = lax.broadcasted_iota(jnp.int32, (BLOCK, BLOCK), 1)
    latent = u >= ctx_chunks
    m_prev = jnp.logical_and(jj >= ii, jnp.logical_and(latent, u - 1 >= ctx_chunks))
    m_cur = jnp.logical_and(ii >= 0, latent)
    m_next = jnp.logical_and(ii >= jj, jnp.logical_and(latent, u + 1 <= n_chunks - 1))
    bias = jnp.concatenate([jnp.where(mk, 0.0, -jnp.inf).astype(F32) for mk in (m_prev, m_cur, m_next)]
                           + [jnp.zeros((n_ctx, BLOCK), F32)], axis=0)
    k_all = jnp.concatenate([kp_ref[0], kc_ref[0], kn_ref[0], kx_ref[0]], axis=0)
    vt_all = jnp.concatenate([vp_ref[0], vc_ref[0], vn_ref[0], vx_ref[0]], axis=1)
    per_kv = Q_HEADS // KV_HEADS
    group_w = per_kv * HEAD_DIM
    kv_w = KV_HEADS * HEAD_DIM
    for g in range(KV_HEADS):
        q_heads = jnp.concatenate([qt_ref[0, g * group_w + r * HEAD_DIM:g * group_w + (r + 1) * HEAD_DIM, :]
                                   for r in range(per_kv)], axis=1)
        pieces = []
        if g > 0:
            pieces.append(jnp.zeros((g * HEAD_DIM, per_kv * BLOCK), BF16))
        pieces.append(q_heads)
        if g < KV_HEADS - 1:
            pieces.append(jnp.zeros((kv_w - (g + 1) * HEAD_DIM, per_kv * BLOCK), BF16))
        s = jnp.dot(k_all, jnp.concatenate(pieces, axis=0), preferred_element_type=F32)
        s = jnp.concatenate([s[:, r * BLOCK:(r + 1) * BLOCK] + bias for r in range(per_kv)], axis=1)
        sink = sink_ref[g]
        mx = jnp.maximum(jnp.max(s, axis=0, keepdims=True), sink)
        p = jnp.exp(s - mx)
        den = jnp.sum(p, axis=0, keepdims=True) + jnp.exp(sink - mx)
        o_t = jnp.dot(vt_all[g * HEAD_DIM:(g + 1) * HEAD_DIM, :], p.astype(BF16),
                      preferred_element_type=F32) / den
        for pair in range(per_kv // 2):
            two = jnp.concatenate([o_t[:, (2 * pair) * BLOCK:(2 * pair + 1) * BLOCK],
                                   o_t[:, (2 * pair + 1) * BLOCK:(2 * pair + 2) * BLOCK]], axis=0)
            c0 = g * group_w + pair * 2 * HEAD_DIM
            o_ref[0, :, c0:c0 + 2 * HEAD_DIM] = two.T.astype(BF16)


def _attention(qt, k, vt, sink, batch, t_total, n_ctx, need_ctx):
    n_chunks = t_total // BLOCK
    ctx_chunks = n_ctx // BLOCK
    u0 = 0 if need_ctx else ctx_chunks
    q_w = Q_HEADS * HEAD_DIM
    kv_w = KV_HEADS * HEAD_DIM
    per_kv = Q_HEADS // KV_HEADS
    assert 2 * HEAD_DIM == LANES and per_kv % 2 == 0
    sink_rows = jnp.repeat(sink.astype(F32).reshape(KV_HEADS, 1, per_kv), BLOCK, axis=2)

    def clipped(n, delta):
        return jnp.clip(n + u0 + delta, ctx_chunks, n_chunks - 1)

    def kwin(delta):
        return pl.BlockSpec((1, BLOCK, kv_w), lambda b, n: (b, clipped(n, delta), 0))

    def vwin(delta):
        return pl.BlockSpec((1, kv_w, BLOCK), lambda b, n: (b, 0, clipped(n, delta)))

    return pl.pallas_call(
        functools.partial(_attention_kernel, u0=u0, n_chunks=n_chunks, ctx_chunks=ctx_chunks, n_ctx=n_ctx),
        out_shape=jax.ShapeDtypeStruct((batch, t_total - u0 * BLOCK, q_w), BF16),
        grid=(batch, n_chunks - u0),
        in_specs=[pl.BlockSpec((1, q_w, BLOCK), lambda b, n: (b, 0, n + u0)),
                  kwin(-1), kwin(0), kwin(1),
                  pl.BlockSpec((1, n_ctx, kv_w), lambda b, n: (b, 0, 0)),
                  vwin(-1), vwin(0), vwin(1),
                  pl.BlockSpec((1, kv_w, n_ctx), lambda b, n: (b, 0, 0)),
                  pl.BlockSpec((KV_HEADS, 1, per_kv * BLOCK), lambda b, n: (0, 0, 0))],
        out_specs=pl.BlockSpec((1, BLOCK, q_w), lambda b, n: (b, n, 0)),
        compiler_params=_cparams(2, 40),
        name="attn_core",
    )(qt, k, k, k, k, vt, vt, vt, vt, sink_rows)


def _rope_tables(seq, n_ctx):
    axis_dim = HEAD_DIM // 2
    freqs = axis_dim // 2
    rows = seq // GRID_W
    row_ids = jnp.repeat(jnp.arange(rows), GRID_W).astype(F32)
    col_ids = jnp.tile(jnp.arange(GRID_W), rows).astype(F32)
    inv_freq = ROPE_BASE ** (-jnp.arange(freqs, dtype=F32) * 2.0 / axis_dim)
    ang_r = row_ids[:, None] * inv_freq
    ang_c = col_ids[:, None] * inv_freq
    ang = jnp.concatenate([ang_r, ang_r, ang_c, ang_c], axis=1)
    cos = jnp.cos(ang)
    sin = jnp.sin(ang)
    lane = jnp.arange(HEAD_DIM)
    second_half = (lane % axis_dim) >= freqs
    sa = jnp.where(second_half, sin, 0.0)
    sb = jnp.where(second_half, 0.0, -sin)

    def full(t, ctx_val):
        t = jnp.concatenate([jnp.full((n_ctx, HEAD_DIM), ctx_val, F32), t], axis=0)
        return jnp.tile(t, (1, LANES // HEAD_DIM))

    return full(cos, 1.0), full(sa, 0.0), full(sb, 0.0)


def _swiglu_rows(h, w1_ref, w3_ref, w2_ref, lead, f_chunk, after_first_chunk=None):
    f_total = w1_ref.shape[-1]
    out = None
    for f0 in range(0, f_total, f_chunk):
        sl = lead + (slice(None), slice(f0, f0 + f_chunk))
        a = jnp.dot(h, w1_ref[sl], preferred_element_type=F32)
        b = jnp.dot(h, w3_ref[sl], preferred_element_type=F32)
        u = (_silu(a) * b).astype(BF16)
        part = jnp.dot(u, w2_ref[lead + (slice(f0, f0 + f_chunk), slice(None))], preferred_element_type=F32)
        out = part if out is None else out + part
        if f0 == 0 and after_first_chunk is not None:
            after_first_chunk()
    return out


def _mixer_residual(x_ref, a_ref, wp_ref, bp_ref, mod_ref):
    y = jnp.dot(a_ref[...], wp_ref[...], preferred_element_type=F32) + bp_ref[...]
    return x_ref[...] + mod_ref[0, 2:3, :] * y


def _ffn_dense_kernel(x_ref, a_ref, wp_ref, bp_ref, mod_ref, nw_ref, w1_ref, w3_ref, w2_ref, o_ref, *, f_chunk):
    x = _mixer_residual(x_ref, a_ref, wp_ref, bp_ref, mod_ref)
    h = _norm_mod(x, nw_ref[...], mod_ref[0, 3:4, :], mod_ref[0, 4:5, :]).astype(BF16)
    f = _swiglu_rows(h, w1_ref, w3_ref, w2_ref, (), f_chunk)
    o_ref[...] = x + mod_ref[0, 5:6, :] * f


def _ffn_dense(xs, a, w_proj, b_proj, mods, nw, w1, w3, w2, tiles):
    m, d = xs.shape
    k = a.shape[1]
    f = w1.shape[1]
    f_chunk = f // 2 if (f // 2) % LANES == 0 else f
    assert a.shape[0] == tiles.grid * ROW_TILE
    return pl.pallas_call(
        functools.partial(_ffn_dense_kernel, f_chunk=f_chunk),
        out_shape=jax.ShapeDtypeStruct((m, d), F32),
        grid=(tiles.grid,),
        in_specs=[tiles.x_spec(d), pl.BlockSpec((ROW_TILE, k), lambda i: (i, 0)),
                  _resident((k, d), lambda i: (0, 0)),
                  _resident((1, d), lambda i: (0, 0)),
                  tiles.mod_spec(d),
                  _resident((1, d), lambda i: (0, 0)),
                  _resident((d, f), lambda i: (0, 0)),
                  _resident((d, f), lambda i: (0, 0)),
                  _resident((f, d), lambda i: (0, 0))],
        out_specs=tiles.x_spec(d),
        input_output_aliases={0: 0},
        compiler_params=_cparams(1, 52),
        name="ffn_dense",
    )(xs, a, w_proj, b_proj, mods, nw, w1, w3, w2)


ROW_BLOCKS = 8


def _rows_to_tiles(ref, lead, val):
    rows = val.shape[0]
    for s in range(ROW_BLOCKS):
        ref[lead + (pl.ds(s, rows, stride=ROW_BLOCKS), slice(None))] = val[:, s * LANES:(s + 1) * LANES]


def _tiles_to_rows(ref, lead, rows):
    return jnp.concatenate([ref[lead + (pl.ds(s, rows, stride=ROW_BLOCKS), slice(None))]
                            for s in range(ROW_BLOCKS)], axis=1)


def _moe_router_kernel(x_ref, a_ref, wp_ref, bp_ref, mod_ref, nw_ref, rt_ref, xo_ref, h_ref, idx_ref, gate_ref):
    x = _mixer_residual(x_ref, a_ref, wp_ref, bp_ref, mod_ref)
    xo_ref[...] = x
    h = _norm_mod(x, nw_ref[...], mod_ref[0, 3:4, :], mod_ref[0, 4:5, :])
    _rows_to_tiles(h_ref, (), h)
    logits = lax.dot_general(rt_ref[...], h, (((1,), (1,)), ((), ())),
                             preferred_element_type=F32, precision=HIGHEST)
    e = lax.broadcasted_iota(jnp.int32, logits.shape, 0)
    m1 = jnp.max(logits, axis=0, keepdims=True)
    i1 = jnp.min(jnp.where(logits == m1, e, N_EXPERTS), axis=0, keepdims=True)
    rest = jnp.where(e == i1, -jnp.inf, logits)
    m2 = jnp.max(rest, axis=0, keepdims=True)
    i2 = jnp.min(jnp.where(rest == m2, e, N_EXPERTS), axis=0, keepdims=True)
    t = jnp.exp(m2 - m1)
    idx_ref[...] = jnp.concatenate([i1, i2], axis=0)
    gate_ref[...] = jnp.concatenate([1.0 / (1.0 + t), t / (1.0 + t)], axis=0)


def _moe_router(xs, a, w_proj, b_proj, mods, nw, router_t, tiles):
    m_all, d = xs.shape
    k = a.shape[1]
    m = tiles.grid * ROW_TILE
    assert a.shape[0] == m
    pair = pl.BlockSpec((TOP_K, ROW_TILE), lambda i: (0, i))
    return pl.pallas_call(
        _moe_router_kernel,
        out_shape=(jax.ShapeDtypeStruct((m_all, d), F32),
                   jax.ShapeDtypeStruct((m * ROW_BLOCKS, LANES), F32),
                   jax.ShapeDtypeStruct((TOP_K, m), jnp.int32),
                   jax.ShapeDtypeStruct((TOP_K, m), F32)),
        grid=(tiles.grid,),
        in_specs=[tiles.x_spec(d), pl.BlockSpec((ROW_TILE, k), lambda i: (i, 0)),
                  _resident((k, d), lambda i: (0, 0)),
                  _resident((1, d), lambda i: (0, 0)),
                  tiles.mod_spec(d),
                  _resident((1, d), lambda i: (0, 0)),
                  _resident((N_EXPERTS, d), lambda i: (0, 0))],
        out_specs=(tiles.x_spec(d),
                   pl.BlockSpec((ROW_TILE * ROW_BLOCKS, LANES), lambda i: (i, 0)), pair, pair),
        input_output_aliases={0: 0},
        compiler_params=_cparams(1, 40),
        name="moe_router",
    )(xs, a, w_proj, b_proj, mods, nw, router_t)


def _moe_experts_kernel(te_ref, src_ref, src_next_ref, dst_init_ref, dst_prev_ref, dst_ref, h_hbm, gate_ref,
                        w1_ref, w3_ref, w2_ref, y_hbm, gbuf, obuf, gsem, osem, *, f_chunk):
    j = pl.program_id(0)
    n = pl.num_programs(0)
    slot = j % 2
    other = 1 - slot
    tile = EXPERT_TILE

    def tile_rows(idx):
        return pl.ds(pl.multiple_of(idx, ROW_BLOCKS), ROW_BLOCKS)

    def gather(ref, i, s):
        return pltpu.make_async_copy(h_hbm.at[tile_rows(ref[0, 0, i])],
                                     gbuf.at[s, pl.ds(i * ROW_BLOCKS, ROW_BLOCKS)], gsem.at[s])

    def scatter(ref, i, s):
        return pltpu.make_async_copy(obuf.at[s, pl.ds(i * ROW_BLOCKS, ROW_BLOCKS)],
                                     y_hbm.at[tile_rows(ref[0, 0, i])], osem.at[s])

    def gather_wait(s):
        pltpu.make_async_copy(h_hbm.at[pl.ds(0, tile * ROW_BLOCKS)], gbuf.at[s], gsem.at[s]).wait()

    def scatter_wait(s):
        pltpu.make_async_copy(obuf.at[s], y_hbm.at[pl.ds(0, tile * ROW_BLOCKS)], osem.at[s]).wait()

    @pl.when(j == 0)
    def _():
        obuf[...] = jnp.zeros_like(obuf)

        def body(i, carry):
            gather(src_ref, i, 0).start()
            scatter(dst_init_ref, i, 0).start()
            return carry
        lax.fori_loop(0, tile, body, 0)

    gather_wait(slot)
    x = _tiles_to_rows(gbuf, (slot,), tile).astype(BF16)
    for i in range(tile):
        gather(src_next_ref, i, other).start()
    for i in range(tile):
        scatter(dst_prev_ref, i, other).start()
    f = _swiglu_rows(x, w1_ref, w3_ref, w2_ref, (0,), f_chunk)
    scatter_wait(slot)
    _rows_to_tiles(obuf, (slot,), gate_ref[...] * f)

    @pl.when(j == n - 1)
    def _():
        def body(i, carry):
            scatter(dst_ref, i, slot).start()
            return carry
        lax.fori_loop(0, tile, body, 0)
        gather_wait(other)
        scatter_wait(other)
        scatter_wait(slot)


def _moe_experts(h, tile_expert, src_rows, dst_rows, slot_gate, w1, w3, w2, n_out_rows):
    d = ROW_BLOCKS * LANES
    assert w1.shape[1] == d
    n_tiles = tile_expert.shape[0]
    f = w1.shape[2]
    tile = EXPERT_TILE
    src3 = src_rows.reshape(n_tiles, 1, tile)
    dst3 = dst_rows.reshape(n_tiles + 2, 1, tile)

    def smem(index_map):
        return pl.BlockSpec((1, 1, tile), index_map, memory_space=pltpu.SMEM)

    grid_spec = pltpu.PrefetchScalarGridSpec(
        num_scalar_prefetch=1,
        grid=(n_tiles,),
        in_specs=[
            smem(lambda j, te: (j, 0, 0)),
            smem(lambda j, te: (jnp.minimum(j + 1, n_tiles - 1), 0, 0)),
            smem(lambda j, te: (0, 0, 0)),
            smem(lambda j, te: (j + 1, 0, 0)),
            smem(lambda j, te: (j + 2, 0, 0)),
            pl.BlockSpec(memory_space=pl.ANY),
            pl.BlockSpec((tile, 1), lambda j, te: (j, 0)),
            _resident((1, d, f), lambda j, te: (te[j], 0, 0)),
            _resident((1, d, f), lambda j, te: (te[j], 0, 0)),
            _resident((1, f, d), lambda j, te: (te[j], 0, 0)),
        ],
        out_specs=pl.BlockSpec(memory_space=pl.ANY),
        scratch_shapes=[pltpu.VMEM((2, tile * ROW_BLOCKS, LANES), F32),
                        pltpu.VMEM((2, tile * ROW_BLOCKS, LANES), F32),
                        pltpu.SemaphoreType.DMA((2,)), pltpu.SemaphoreType.DMA((2,))],
    )
    return pl.pallas_call(
        functools.partial(_moe_experts_kernel, f_chunk=f // 2),
        out_shape=jax.ShapeDtypeStruct((n_out_rows * ROW_BLOCKS, LANES), F32),
        grid_spec=grid_spec,
        compiler_params=_cparams(1, 56),
        name="moe_experts",
    )(tile_expert, src3, src3, dst3, dst3, dst3, h, slot_gate, w1, w3, w2)


def _moe_combine_kernel(x_ref, mod_ref, y0_ref, y1_ref, fw_ref, o_ref, *, final):
    y = _tiles_to_rows(y0_ref, (), ROW_TILE) + _tiles_to_rows(y1_ref, (), ROW_TILE)
    x = x_ref[...] + mod_ref[0, 5:6, :] * y
    if final:
        x = x * lax.rsqrt(jnp.mean(x * x, axis=-1, keepdims=True) + EPS) * fw_ref[...]
    o_ref[...] = x


def _moe_combine(xs, mods, y, tiles, final_w=None):
    m, d = xs.shape
    k1 = tiles.grid
    final = final_w is not None
    if final:
        out_shape = jax.ShapeDtypeStruct((tiles.grid * ROW_TILE, d), F32)
        out_spec = pl.BlockSpec((ROW_TILE, d), lambda i: (i, 0))
    else:
        out_shape = jax.ShapeDtypeStruct((m, d), F32)
        out_spec = tiles.x_spec(d)
        final_w = jnp.ones((1, d), F32)
    return pl.pallas_call(
        functools.partial(_moe_combine_kernel, final=final),
        out_shape=out_shape,
        grid=(tiles.grid,),
        in_specs=[tiles.x_spec(d), tiles.mod_spec(d),
                  pl.BlockSpec((ROW_TILE * ROW_BLOCKS, LANES), lambda i: (i, 0)),
                  pl.BlockSpec((ROW_TILE * ROW_BLOCKS, LANES), lambda i: (k1 + i, 0)),
                  pl.BlockSpec((1, d), lambda i: (0, 0))],
        out_specs=out_spec,
        input_output_aliases={} if final else {0: 0},
        compiler_params=_cparams(1, 32),
        name="moe_combine",
    )(xs, mods, y, y, final_w)


def _moe_layer(xs, a, w_proj, b_proj, mods, nw, router, w1, w3, w2, tiles, final_w=None):
    xs, h, top_i, gates = _moe_router(xs, a, w_proj, b_proj, mods, nw, router.T.astype(F32), tiles)

    tile = EXPERT_TILE
    m = tiles.grid * ROW_TILE
    n_assign = TOP_K * m
    eid = top_i.reshape(-1)
    onehot = (eid[:, None] == jnp.arange(N_EXPERTS)[None, :]).astype(jnp.int32)
    csum = jnp.cumsum(onehot, axis=0)
    counts = csum[-1]
    rank = jnp.take_along_axis(csum, eid[:, None], axis=1)[:, 0] - 1
    padded = ((counts + tile - 1) // tile) * tile
    ends = jnp.cumsum(padded)
    starts = ends - padded
    n_slots = n_assign + N_EXPERTS * tile
    n_tiles = n_slots // tile
    pos = starts[eid] + rank
    slot_assign = jnp.zeros((n_slots,), jnp.int32).at[pos].set(jnp.arange(1, n_assign + 1, dtype=jnp.int32))
    valid = slot_assign > 0
    assign = jnp.maximum(slot_assign - 1, 0)
    src_rows = jnp.where(valid, jnp.where(assign >= m, assign - m, assign), 0)
    spare = n_assign + jnp.cumsum(jnp.logical_not(valid).astype(jnp.int32)) - 1
    dst_rows = jnp.where(valid, assign, spare)
    slot_gate = jnp.where(valid, gates.reshape(-1)[assign], 0.0)
    n_out_rows = n_slots + 2 * tile
    prime_rows = n_slots + jnp.arange(2 * tile, dtype=jnp.int32)
    tile_start = jnp.arange(n_tiles, dtype=jnp.int32) * tile
    tile_expert = jnp.sum(jnp.minimum(tile_start, ends[-1] - 1)[:, None] >= ends[None, :], axis=1)
    tile_expert = jnp.clip(tile_expert, 0, N_EXPERTS - 1).astype(jnp.int32)

    y = _moe_experts(h, tile_expert, src_rows.astype(jnp.int32) * ROW_BLOCKS,
                     jnp.concatenate([prime_rows, dst_rows.astype(jnp.int32)]) * ROW_BLOCKS,
                     slot_gate.reshape(-1, 1), w1, w3, w2, n_out_rows)
    return _moe_combine(xs, mods, y, tiles, final_w)


def kernel(x, c, ctx, c_ctx, w_mod, b_mod, norm1_w, norm2_w, ssd_w_in, ssd_conv_w, ssd_conv_b, ssd_dt_bias, ssd_a_log, ssd_d, ssd_norm_w, ssd_w_out, attn_w_qkv, attn_b_qkv, attn_sink, attn_w_o, attn_b_o, ffn_w1, ffn_w3, ffn_w2, moe_router, moe_w1, moe_w3, moe_w2, final_norm_w):
    batch, seq, d = x.shape
    n_ctx = ctx.shape[1]
    t_total = n_ctx + seq
    depth = w_mod.shape[0]
    assert seq % GRID_W == 0 and seq % BLOCK == 0 and n_ctx % ROW_TILE == 0 and depth % 2 == 0

    all_tiles = _Tiles(batch, t_total, n_ctx, 0)
    lat_tiles = _Tiles(batch, t_total, n_ctx, n_ctx // ROW_TILE)

    xs = jnp.concatenate([ctx, x], axis=1).reshape(batch * t_total, d)
    mods_all = _mod_all(c, c_ctx, w_mod, b_mod)
    cos, sa, sb = _rope_tables(seq, n_ctx)
    q_w = Q_HEADS * HEAD_DIM
    q_scale = jnp.concatenate([jnp.full((q_w,), 1.0 / math.sqrt(HEAD_DIM), F32),
                               jnp.ones((attn_w_qkv.shape[2] - q_w,), F32)])

    for i in range(depth):
        last = i == depth - 1
        j = i // 2
        mods = mods_all[i]
        nw1 = norm1_w[i].reshape(1, d)
        nw2 = norm2_w[i].reshape(1, d)
        upd = lat_tiles if last else all_tiles
        if i % 2 == 0:
            g = _ssd_mixer(xs, mods, nw1, ssd_w_in[j], ssd_conv_w[j], ssd_conv_b[j], ssd_dt_bias[j],
                           ssd_a_log[j], ssd_d[j], ssd_norm_w[j], all_tiles, batch, t_total, n_ctx)
            xs = _ffn_dense(xs, g, ssd_w_out[j].astype(BF16), jnp.zeros((1, d), F32), mods, nw2,
                            ffn_w1[j].astype(BF16), ffn_w3[j].astype(BF16), ffn_w2[j].astype(BF16), all_tiles)
        else:
            qt, k, vt = _qkv_rope(xs, mods, nw1, (attn_w_qkv[j] * q_scale).astype(BF16),
                                  (attn_b_qkv[j] * q_scale).reshape(1, -1), cos, sa, sb, all_tiles,
                                  batch, t_total)
            o = _attention(qt, k, vt, attn_sink[j], batch, t_total, n_ctx, not last)
            xs = _moe_layer(xs, o.reshape(-1, q_w), attn_w_o[j].astype(BF16), attn_b_o[j].reshape(1, d),
                            mods, nw2, moe_router[j], moe_w1[j].astype(BF16), moe_w3[j].astype(BF16),
                            moe_w2[j].astype(BF16), upd,
                            final_w=final_norm_w.reshape(1, d) if last else None)
    return xs.reshape(batch, seq, d)
```

```python
import functools
import math

import jax
import jax.numpy as jnp
from jax import lax
from jax.experimental import pallas as pl
from jax.experimental.pallas import tpu as pltpu

F32 = jnp.float32
BF16 = jnp.bfloat16
HIGHEST = lax.Precision.HIGHEST

EPS = 1e-6
LOG2_E = math.log2(math.e)
N_MOD = 6
GRID_W = 64
ROPE_BASE = 10000.0

SSD_HEAD_DIM = 64
SSD_GROUPS = 8
D_STATE = 128
CONV_K = 5
CHUNK = 128
HEADS_PER_GROUP = 4
GROUP_W = HEADS_PER_GROUP * SSD_HEAD_DIM

Q_HEADS = 16
KV_HEADS = 4
HEAD_DIM = 64
BLOCK = 128

N_EXPERTS = 8
TOP_K = 2

ROW_TILE = 256
EXPERT_TILE = 512
LANES = 128
PACK_ROWS_BF16 = 16

MIB = 1024 * 1024


def _cparams(n_axes, vmem_mib):
    return pltpu.CompilerParams(dimension_semantics=("arbitrary",) * n_axes,
                                vmem_limit_bytes=vmem_mib * MIB)


def _resident(block_shape, index_map):
    return pl.BlockSpec(block_shape, index_map, pipeline_mode=pl.Buffered(1))


def _sigmoid(v):
    return 1.0 / (1.0 + jnp.exp(-v))


def _silu(v):
    return v * _sigmoid(v)


def _norm_mod(x, nw, shift, scale):
    ms = jnp.mean(x * x, axis=-1, keepdims=True)
    y = x * lax.rsqrt(ms + EPS) * nw
    return y * (1.0 + scale) + shift


class _Tiles:
    def __init__(self, batch, t_total, n_ctx, lo_tiles):
        assert t_total % ROW_TILE == 0 and n_ctx % ROW_TILE == 0
        self.batch = batch
        self.tpb = t_total // ROW_TILE
        self.ctx_tiles = n_ctx // ROW_TILE
        self.lo = lo_tiles
        self.n_w = self.tpb - lo_tiles
        self.grid = batch * self.n_w

    def split(self, i):
        return i // self.n_w, self.lo + i % self.n_w

    def row(self, i):
        b, w = self.split(i)
        return b * self.tpb + w

    def mod_row(self, i):
        b, w = self.split(i)
        return jnp.where(w < self.ctx_tiles, self.batch, b)

    def x_spec(self, width):
        return pl.BlockSpec((ROW_TILE, width), lambda i: (self.row(i), 0))

    def mod_spec(self, d):
        return pl.BlockSpec((1, N_MOD, d), lambda i: (self.mod_row(i), 0, 0))


def _mod_kernel(c_ref, w_ref, b_ref, o_ref):
    s = _silu(c_ref[...])
    o_ref[0] = jnp.dot(s, w_ref[0], preferred_element_type=F32, precision=HIGHEST) + b_ref[0]


def _mod_all(c, c_ctx, w_mod, b_mod):
    depth, d, n = w_mod.shape
    cc = jnp.concatenate([c, c_ctx[None]], axis=0)
    rows = cc.shape[0]
    tn = 1536
    assert n % tn == 0
    out = pl.pallas_call(
        _mod_kernel,
        out_shape=jax.ShapeDtypeStruct((depth, rows, n), F32),
        grid=(depth, n // tn),
        in_specs=[pl.BlockSpec((rows, d), lambda i, j: (0, 0)),
                  pl.BlockSpec((1, d, tn), lambda i, j: (i, 0, j)),
                  pl.BlockSpec((1, 1, tn), lambda i, j: (i, 0, j))],
        out_specs=pl.BlockSpec((1, rows, tn), lambda i, j: (i, 0, j)),
        compiler_params=_cparams(2, 40),
        name="mod_all",
    )(cc, w_mod, b_mod.reshape(depth, 1, n))
    return out.reshape(depth, rows, N_MOD, d)


def _ssd_inproj_kernel(x_ref, mod_ref, nw_ref, wzx_ref, wdt_ref, zx_ref, dt_ref, *, n_chunk):
    h = _norm_mod(x_ref[...], nw_ref[...], mod_ref[0, 0:1, :], mod_ref[0, 1:2, :]).astype(BF16)
    n = wzx_ref.shape[1]
    for n0 in range(0, n, n_chunk):
        zx_ref[:, n0:n0 + n_chunk] = jnp.dot(
            h, wzx_ref[:, n0:n0 + n_chunk], preferred_element_type=F32).astype(BF16)
    dt_ref[...] = jnp.dot(h, wdt_ref[...], preferred_element_type=F32)


def _ssd_inproj(xs, mods, nw, wzx, wdt, tiles):
    m, d = xs.shape
    n = wzx.shape[1]
    return pl.pallas_call(
        functools.partial(_ssd_inproj_kernel, n_chunk=1536),
        out_shape=(jax.ShapeDtypeStruct((m, n), BF16), jax.ShapeDtypeStruct((m, LANES), F32)),
        grid=(tiles.grid,),
        in_specs=[tiles.x_spec(d), tiles.mod_spec(d),
                  _resident((1, d), lambda i: (0, 0)),
                  _resident((d, n), lambda i: (0, 0)),
                  _resident((d, LANES), lambda i: (0, 0))],
        out_specs=(tiles.x_spec(n), tiles.x_spec(LANES)),
        compiler_params=_cparams(1, 48),
        name="ssd_inproj",
    )(xs, mods, nw, wzx, wdt)


def _dt_prep_kernel(raw_ref, bias_ref, a_ref, dtc_ref, acc_ref, dtr_ref, acr_ref):
    ii = lax.broadcasted_iota(jnp.int32, (CHUNK, CHUNK), 0)
    jj = lax.broadcasted_iota(jnp.int32, (CHUNK, CHUNK), 1)
    lower = (ii >= jj).astype(F32)
    upper = (ii <= jj).astype(F32)
    col = lax.broadcasted_iota(jnp.int32, (CHUNK, LANES), 1)
    is_fwd = (col % (2 * HEADS_PER_GROUP)) < HEADS_PER_GROUP
    for c in range(raw_ref.shape[0] // CHUNK):
        rows = pl.ds(c * CHUNK, CHUNK)
        v = raw_ref[rows, :] + bias_ref[...]
        dt = jnp.maximum(v, 0.0) + jnp.log(1.0 + jnp.exp(-jnp.abs(v)))
        a = dt * a_ref[...]
        prefix = jnp.dot(lower, a, preferred_element_type=F32, precision=HIGHEST)
        suffix = jnp.dot(upper, a, preferred_element_type=F32, precision=HIGHEST)
        ac = jnp.where(is_fwd, prefix, suffix)
        dtc_ref[rows, :] = dt
        acc_ref[rows, :] = ac
        dtr_ref[c] = dt.T
        acr_ref[c] = ac.T


DT_PREP_CHUNKS = 4


def _dt_prep(dt_raw, bias, a_neg):
    m = dt_raw.shape[0]
    nchunks = m // CHUNK
    per = DT_PREP_CHUNKS
    assert nchunks % per == 0
    tile = pl.BlockSpec((per * CHUNK, LANES), lambda i: (i, 0))
    vec = pl.BlockSpec((1, LANES), lambda i: (0, 0))
    sq = pl.BlockSpec((per, LANES, CHUNK), lambda i: (i, 0, 0))
    return pl.pallas_call(
        _dt_prep_kernel,
        out_shape=(jax.ShapeDtypeStruct((m, LANES), F32), jax.ShapeDtypeStruct((m, LANES), F32),
                   jax.ShapeDtypeStruct((nchunks, LANES, CHUNK), F32),
                   jax.ShapeDtypeStruct((nchunks, LANES, CHUNK), F32)),
        grid=(nchunks // per,),
        in_specs=[tile, vec, vec],
        out_specs=(tile, tile, sq, sq),
        compiler_params=_cparams(1, 32),
        name="ssd_dt_prep",
    )(dt_raw, bias, a_neg)


def _split3(v):
    def top_bits(a):
        return lax.bitcast_convert_type(
            lax.bitcast_convert_type(a, jnp.uint32) & jnp.uint32(0xFFFF0000), F32)

    hi = top_bits(v)
    r1 = v - hi
    mid = top_bits(r1)
    lo = r1 - mid
    return hi.astype(BF16), mid.astype(BF16), lo.astype(BF16)


N_BCAST = 4 * CHUNK + 2 * GROUP_W
PIECE_LANES = 3 * 2 * HEADS_PER_GROUP


def _bcast_selector():
    lane = jnp.arange(LANES)[:, None]
    col = jnp.arange(N_BCAST)[None, :]
    n_q = 2 * HEADS_PER_GROUP
    sel = []
    for d in range(2):
        rel = lane - PIECE_LANES * d
        q = rel % n_q
        tile = (col < 4 * CHUNK) & (col // CHUNK == q)
        acc_exp = (col >= 4 * CHUNK) & (col < 4 * CHUNK + GROUP_W) & ((col - 4 * CHUNK) // SSD_HEAD_DIM == q)
        dt_exp = (col >= 4 * CHUNK + GROUP_W) & (
            (col - 4 * CHUNK - GROUP_W) // SSD_HEAD_DIM + HEADS_PER_GROUP == q)
        in_dir = (rel >= 0) & (rel < PIECE_LANES)
        sel.append(((tile | acc_exp | dt_exp) & in_dir).astype(BF16))
    return jnp.stack(sel)


CONV_HALO = 64
CONV_WIN = CHUNK + 2 * CONV_HALO


def _shift_selector():
    pad = CONV_K // 2
    taps = jnp.asarray([k for k in range(CONV_K) if k != pad])
    row = jnp.arange((CONV_K - 1) * CHUNK)
    src = CONV_HALO + row % CHUNK + taps[row // CHUNK] - pad
    return (src[:, None] == jnp.arange(CONV_WIN)[None, :]).astype(BF16)


def _ssd_scan_kernel(z_ref, x_ref, b_ref, c_ref, cw_ref, colp_ref, dtr_ref, acr_ref, e_ref, shift_ref,
                     dsk_ref, nw_ref, o_ref, xc_s, cc_s, bt_s, yf_s, yb_s, sf_s, sb_s, *, n_chunks, ctx_chunks):
    t_total = n_chunks * CHUNK
    halo = CONV_HALO
    pad = CONV_K // 2
    taps = [k for k in range(CONV_K) if k != pad]

    def window(ref, c, t0):
        cur = ref[0, pl.ds(t0, CHUNK), :]
        prev = ref[0, pl.ds(pl.multiple_of(jnp.maximum(t0 - halo, 0), halo), halo), :]
        nxt = ref[0, pl.ds(pl.multiple_of(jnp.minimum(t0 + CHUNK, t_total - halo), halo), halo), :]
        prev_ok = jnp.logical_and(c != 0, c != ctx_chunks)
        next_ok = jnp.logical_and(c != ctx_chunks - 1, c != n_chunks - 1)
        prev = jnp.where(prev_ok, prev, jnp.zeros_like(prev))
        nxt = jnp.where(next_ok, nxt, jnp.zeros_like(nxt))
        return jnp.concatenate([prev, cur, nxt], axis=0)

    def conv_body(c, carry):
        t0 = pl.multiple_of(c * CHUNK, CHUNK)
        w = jnp.concatenate([window(x_ref, c, t0), window(b_ref, c, t0), window(c_ref, c, t0)], axis=1)
        shifted = jnp.dot(shift_ref[...], w, preferred_element_type=F32)
        acc = cw_ref[0, CONV_K:CONV_K + 1, :] + w[halo:halo + CHUNK].astype(F32) * cw_ref[0, pad:pad + 1, :]
        for n, k in enumerate(taps):
            acc = acc + shifted[n * CHUNK:(n + 1) * CHUNK] * cw_ref[0, k:k + 1, :]
        v = _silu(acc)
        xc_s[pl.ds(t0, CHUNK), :] = v[:, :GROUP_W].astype(BF16)
        bt_s[c] = v[:, GROUP_W:GROUP_W + D_STATE].T.astype(BF16)
        cc_s[pl.ds(t0, CHUNK), :] = v[:, GROUP_W + D_STATE:].astype(BF16)
        return carry

    lax.fori_loop(0, n_chunks, conv_body, 0, unroll=2)

    ii = lax.broadcasted_iota(jnp.int32, (CHUNK, CHUNK), 0)
    jj = lax.broadcasted_iota(jnp.int32, (CHUNK, CHUNK), 1)
    lane_w = lax.broadcasted_iota(jnp.int32, (CHUNK, GROUP_W), 1)
    y_s = (yf_s, yb_s)
    st_s = (sf_s, sb_s)
    sf_s[...] = jnp.zeros_like(sf_s)
    sb_s[...] = jnp.zeros_like(sb_s)

    def chunk_step(d, c, bcast):
        mask = (ii >= jj) if d == 0 else (ii <= jj)
        last = CHUNK - 1 if d == 0 else 0
        t0 = pl.multiple_of(c * CHUNK, CHUNK)
        xch = xc_s[pl.ds(t0, CHUNK), :]
        cch = cc_s[pl.ds(t0, CHUNK), :]
        bt = bt_s[c]
        dtr = dtr_ref[0, c]
        acr = acr_ref[0, c]
        state = st_s[d][...]

        acc_exp = bcast(4 * CHUNK, GROUP_W)
        dt_exp = bcast(4 * CHUNK + GROUP_W, GROUP_W)
        cb = jnp.dot(cch, bt, preferred_element_type=F32)
        ms = []
        xs_ = []
        for r in range(HEADS_PER_GROUP):
            k = 4 * d + r
            if r % 2 == 0:
                acc_pair = bcast(r * CHUNK, 2 * CHUNK)
            seg = acc_pair[:, (r % 2) * CHUNK:(r % 2 + 1) * CHUNK] - acr[k:k + 1, :]
            decay = jnp.exp(jnp.where(mask, seg, -jnp.inf))
            ms.append((cb * decay * dtr[k:k + 1, :]).astype(BF16))
            in_head = jnp.logical_and(lane_w >= r * SSD_HEAD_DIM, lane_w < (r + 1) * SSD_HEAD_DIM)
            xs_.append(jnp.where(in_head, xch, jnp.zeros_like(xch)))
        y = jnp.dot(jnp.concatenate(ms, axis=1), jnp.concatenate(xs_, axis=0), preferred_element_type=F32)
        y = y + jnp.dot(cch, state.astype(BF16), preferred_element_type=F32) * jnp.exp(acc_exp)
        y_s[d][pl.ds(t0, CHUNK), :] = y

        a_last = acc_exp[last:last + 1, :]
        xw = (xch.astype(F32) * (jnp.exp(a_last - acc_exp) * dt_exp)).astype(BF16)
        st_s[d][...] = state * jnp.exp(a_last) + jnp.dot(bt, xw, preferred_element_type=F32)

    def pair_step(d, c_lo, first_half):
        pieces = colp_ref[0, 0, pl.ds(pl.multiple_of(c_lo * CHUNK, CHUNK), 2 * CHUNK), :]
        products = {}

        def bcast(half, c0, width):
            if (c0, width) not in products:
                products[(c0, width)] = jnp.dot(pieces, e_ref[d, :, c0:c0 + width], preferred_element_type=F32)
            return products[(c0, width)][half * CHUNK:(half + 1) * CHUNK]

        for half in (first_half, 1 - first_half):
            chunk_step(d, c_lo + half, functools.partial(bcast, half))

    def scan_body(p, carry):
        s = 2 * p
        pair_step(0, s, 0)
        c_bwd = jnp.where(s < ctx_chunks, ctx_chunks - 1 - s, n_chunks - 1 - (s - ctx_chunks))
        pair_step(1, c_bwd - 1, 1)
        return carry

    assert n_chunks % 2 == 0 and ctx_chunks % 2 == 0
    lax.fori_loop(0, n_chunks // 2, scan_body, 0)

    def out_body(c, carry):
        t0 = pl.multiple_of(c * CHUNK, CHUNK)
        y = (yf_s[pl.ds(t0, CHUNK), :] + yb_s[pl.ds(t0, CHUNK), :]
             + dsk_ref[0] * xc_s[pl.ds(t0, CHUNK), :].astype(F32))
        g = y * _silu(z_ref[0, pl.ds(t0, CHUNK), :].astype(F32))
        g = g * lax.rsqrt(jnp.mean(g * g, axis=-1, keepdims=True) + EPS)
        o_ref[0, pl.ds(t0, CHUNK), :] = (g * nw_ref[0]).astype(BF16)
        return carry

    lax.fori_loop(0, n_chunks, out_body, 0, unroll=2)


def _ssd_scan(zx, cw, colp, dtr, acr, dskip, norm_w, batch, t_total, n_ctx):
    d_inner = SSD_GROUPS * GROUP_W
    n_chunks = t_total // CHUNK
    zx3 = zx.reshape(batch, t_total, zx.shape[-1])
    xoff = d_inner // GROUP_W
    boff = 2 * d_inner // D_STATE
    coff = boff + SSD_GROUPS
    pc = CONV_K + 3
    rows =pl.BlockSpec((1, n_chunks, 2 * HEADS_PER_GROUP, CHUNK), lambda b, g: (b, 0, g, 0))
    vec = pl.BlockSpec((1, 1, GROUP_W), lambda b, g: (g, 0, 0))
    return pl.pallas_call(
        functools.partial(_ssd_scan_kernel, n_chunks=n_chunks, ctx_chunks=n_ctx // CHUNK),
        out_shape=jax.ShapeDtypeStruct((batch, t_total, d_inner), BF16),
        grid=(batch, SSD_GROUPS),
        in_specs=[pl.BlockSpec((1, t_total, GROUP_W), lambda b, g: (b, 0, g)),
                  pl.BlockSpec((1, t_total, GROUP_W), lambda b, g: (b, 0, xoff + g)),
                  pl.BlockSpec((1, t_total, D_STATE), lambda b, g: (b, 0, boff + g)),
                  pl.BlockSpec((1, t_total, D_STATE), lambda b, g: (b, 0, coff + g)),
                  pl.BlockSpec((1, pc, GROUP_W + 2 * D_STATE), lambda b, g: (g, 0, 0)),
                  pl.BlockSpec((1, 1, t_total, LANES), lambda b, g: (b, g, 0, 0)),
                  rows, rows,
                  _resident((2, LANES, N_BCAST), lambda b, g: (0, 0, 0)),
                  _resident(((CONV_K - 1) * CHUNK, CONV_WIN), lambda b, g: (0, 0)),
                  vec, vec],
        out_specs=pl.BlockSpec((1, t_total, GROUP_W), lambda b, g: (b, 0, g)),
        scratch_shapes=[pltpu.VMEM((t_total, GROUP_W), BF16),
                        pltpu.VMEM((t_total, D_STATE), BF16),
                        pltpu.VMEM((n_chunks, D_STATE, CHUNK), BF16),
                        pltpu.VMEM((t_total, GROUP_W), F32),
                        pltpu.VMEM((t_total, GROUP_W), F32),
                        pltpu.VMEM((D_STATE, GROUP_W), F32),
                        pltpu.VMEM((D_STATE, GROUP_W), F32)],
        compiler_params=_cparams(2, 56),
        name="ssd_scan",
    )(zx3, zx3, zx3, zx3, cw, colp, dtr, acr, _bcast_selector(), _shift_selector(), dskip, norm_w)


def _ssd_mixer(xs, mods, nw, w_in, conv_w, conv_b, dt_bias, a_log, d_skip, norm_w,
               tiles, batch, t_total, n_ctx):
    d = xs.shape[1]
    d_inner = SSD_GROUPS * GROUP_W
    conv_ch = d_inner + 2 * SSD_GROUPS * D_STATE
    heads = SSD_GROUPS * HEADS_PER_GROUP
    n_dt = 2 * heads
    perm = jnp.arange(n_dt).reshape(2, SSD_GROUPS, HEADS_PER_GROUP).transpose(1, 0, 2).reshape(-1)
    wzx = w_in[:, :d_inner + conv_ch].astype(BF16)
    wdt = jnp.zeros((d, LANES), F32).at[:, :n_dt].set(w_in[:, d_inner + conv_ch:][:, perm]).astype(BF16)
    bias = jnp.zeros((1, LANES), F32).at[0, :n_dt].set(dt_bias.reshape(-1)[perm])
    a_neg = jnp.zeros((1, LANES), F32).at[0, :n_dt].set(-jnp.exp(a_log.astype(F32)).reshape(-1)[perm])

    zx, dt_raw = _ssd_inproj(xs, mods, nw, wzx, wdt, tiles)
    dtc, acc, dtr, acr = _dt_prep(dt_raw, bias, a_neg)

    def cols(a):
        return a[:, :n_dt].reshape(batch, t_total, SSD_GROUPS, 2, HEADS_PER_GROUP).transpose(0, 2, 1, 3, 4)

    vals = jnp.concatenate([cols(acc), cols(dtc)], axis=-1)
    colp = jnp.stack(_split3(vals), axis=-2).reshape(batch, SSD_GROUPS, t_total, 2 * PIECE_LANES)
    colp = jnp.pad(colp, ((0, 0), (0, 0), (0, 0), (0, LANES - 2 * PIECE_LANES)))

    n_chunks = t_total // CHUNK
    dtr = dtr.reshape(batch, n_chunks, LANES, CHUNK)
    acr = acr.reshape(batch, n_chunks, LANES, CHUNK)

    def per_group(v):
        gx = v[:, :d_inner].reshape(-1, SSD_GROUPS, GROUP_W)
        gb = v[:, d_inner:d_inner + SSD_GROUPS * D_STATE].reshape(-1, SSD_GROUPS, D_STATE)
        gc = v[:, d_inner + SSD_GROUPS * D_STATE:].reshape(-1, SSD_GROUPS, D_STATE)
        return jnp.concatenate([gx, gb, gc], axis=-1).transpose(1, 0, 2)

    cw = per_group(jnp.concatenate([conv_w, conv_b[None], jnp.zeros((2, conv_ch), F32)], axis=0))
    dsk = jnp.repeat((d_skip[0] + d_skip[1]).astype(F32), SSD_HEAD_DIM).reshape(SSD_GROUPS, 1, GROUP_W)
    gnw = norm_w.astype(F32).reshape(SSD_GROUPS, 1, GROUP_W)

    g = _ssd_scan(zx, cw, colp, dtr, acr, dsk, gnw, batch, t_total, n_ctx)
    return g.reshape(batch * t_total, d_inner)


def _transpose_rows(blk):
    return jnp.concatenate([blk[r0:r0 + LANES].T for r0 in range(0, blk.shape[0], LANES)], axis=1)


def _qkv_rope_kernel(x_ref, mod_ref, nw_ref, w_ref, b_ref, cos_ref, sa_ref, sb_ref,
                     qt_ref, k_ref, vt_ref):
    h = _norm_mod(x_ref[...], nw_ref[...], mod_ref[0, 0:1, :], mod_ref[0, 1:2, :]).astype(BF16)
    acc = jnp.dot(h, w_ref[...], preferred_element_type=F32) + b_ref[...]
    cos = cos_ref[...]
    sa = sa_ref[...]
    sb = sb_ref[...]
    half = HEAD_DIM // 4
    q_w = Q_HEADS * HEAD_DIM
    kv_w = KV_HEADS * HEAD_DIM

    def rope(blk):
        return blk * cos + pltpu.roll(blk, half, 1) * sa + pltpu.roll(blk, LANES - half, 1) * sb

    for c0 in range(0, q_w, LANES):
        qt_ref[0, c0:c0 + LANES, :] = _transpose_rows(rope(acc[:, c0:c0 + LANES] * LOG2_E)).astype(BF16)
    for c0 in range(0, kv_w, LANES):
        k_ref[0, :, c0:c0 + LANES] = rope(acc[:, q_w + c0:q_w + c0 + LANES]).astype(BF16)
        vt_ref[0, c0:c0 + LANES, :] = _transpose_rows(
            acc[:, q_w + kv_w + c0:q_w + kv_w + c0 + LANES]).astype(BF16)


def _qkv_rope(xs, mods, nw, w, b, cos, sa, sb, tiles, batch, t_total):
    m, d = xs.shape
    n = w.shape[1]
    q_w = Q_HEADS * HEAD_DIM
    kv_w = KV_HEADS * HEAD_DIM
    tab = pl.BlockSpec((ROW_TILE, LANES), lambda i: (tiles.split(i)[1], 0))

    def feat_major(width):
        return pl.BlockSpec((1, width, ROW_TILE), lambda i: (tiles.split(i)[0], 0, tiles.split(i)[1]))

    return pl.pallas_call(
        _qkv_rope_kernel,
        out_shape=(jax.ShapeDtypeStruct((batch, q_w, t_total), BF16),
                   jax.ShapeDtypeStruct((batch, t_total, kv_w), BF16),
                   jax.ShapeDtypeStruct((batch, kv_w, t_total), BF16)),
        grid=(tiles.grid,),
        in_specs=[tiles.x_spec(d), tiles.mod_spec(d),
                  _resident((1, d), lambda i: (0, 0)),
                  _resident((d, n), lambda i: (0, 0)),
                  _resident((1, n), lambda i: (0, 0)),
                  tab, tab, tab],
        out_specs=(feat_major(q_w),
                   pl.BlockSpec((1, ROW_TILE, kv_w), lambda i: (tiles.split(i)[0], tiles.split(i)[1], 0)),
                   feat_major(kv_w)),
        compiler_params=_cparams(1, 40),
        name="attn_qkv_rope",
    )(xs, mods, nw, w, b, cos, sa, sb)


def _attention_kernel(qt_ref, kp_ref, kc_ref, kn_ref, kx_ref, vp_ref, vc_ref, vn_ref, vx_ref, sink_ref,
                      o_ref, *, u0, n_chunks, ctx_chunks, n_ctx):
    u = pl.program_id(1) + u0
    jj = lax.broadcasted_iota(jnp.int32, (BLOCK, BLOCK), 0)
    ii = lax.broadcasted_iota(jnp.int32, (BLOCK, BLOCK), 1)
    latent = u >= ctx_chunks
    m_prev = jnp.logical_and(jj >= ii, jnp.logical_and(latent, u - 1 >= ctx_chunks))
    m_cur = jnp.logical_and(ii >= 0, latent)
    m_next = jnp.logical_and(ii >= jj, jnp.logical_and(latent, u + 1 <= n_chunks - 1))
    bias = jnp.concatenate([jnp.where(mk, 0.0, -jnp.inf).astype(F32) for mk in (m_prev, m_cur, m_next)]
                           + [jnp.zeros((n_ctx, BLOCK), F32)], axis=0)
    k_all = jnp.concatenate([kp_ref[0], kc_ref[0], kn_ref[0], kx_ref[0]], axis=0)
    vt_all = jnp.concatenate([vp_ref[0], vc_ref[0], vn_ref[0], vx_ref[0]], axis=1)
    per_kv = Q_HEADS // KV_HEADS
    group_w = per_kv * HEAD_DIM
    kv_w = KV_HEADS * HEAD_DIM
    for g in range(KV_HEADS):
        q_heads = jnp.concatenate([qt_ref[0, g * group_w + r * HEAD_DIM:g * group_w + (r + 1) * HEAD_DIM, :]
                                   for r in range(per_kv)], axis=1)
        pieces = []
        if g > 0:
            pieces.append(jnp.zeros((g * HEAD_DIM, per_kv * BLOCK), BF16))
        pieces.append(q_heads)
        if g < KV_HEADS - 1:
            pieces.append(jnp.zeros((kv_w - (g + 1) * HEAD_DIM, per_kv * BLOCK), BF16))
        s = jnp.dot(k_all, jnp.concatenate(pieces, axis=0), preferred_element_type=F32)
        s = jnp.concatenate([s[:, r * BLOCK:(r + 1) * BLOCK] + bias for r in range(per_kv)], axis=1)
        sink = sink_ref[g]
        mx = jnp.maximum(jnp.max(s, axis=0, keepdims=True), sink)
        p = jnp.exp2(s - mx)
        den = jnp.sum(p, axis=0, keepdims=True) + jnp.exp2(sink - mx)
        o_t = jnp.dot(vt_all[g * HEAD_DIM:(g + 1) * HEAD_DIM, :], p.astype(BF16),
                      preferred_element_type=F32) / den
        for pair in range(per_kv // 2):
            two = jnp.concatenate([o_t[:, (2 * pair) * BLOCK:(2 * pair + 1) * BLOCK],
                                   o_t[:, (2 * pair + 1) * BLOCK:(2 * pair + 2) * BLOCK]], axis=0)
            c0 = g * group_w + pair * 2 * HEAD_DIM
            o_ref[0, :, c0:c0 + 2 * HEAD_DIM] = two.T.astype(BF16)


def _attention(qt, k, vt, sink, batch, t_total, n_ctx, need_ctx):
    n_chunks = t_total // BLOCK
    ctx_chunks = n_ctx // BLOCK
    u0 = 0 if need_ctx else ctx_chunks
    q_w = Q_HEADS * HEAD_DIM
    kv_w = KV_HEADS * HEAD_DIM
    per_kv = Q_HEADS // KV_HEADS
    assert 2 * HEAD_DIM == LANES and per_kv % 2 == 0
    sink_rows = jnp.repeat((sink.astype(F32) * LOG2_E).reshape(KV_HEADS, 1, per_kv), BLOCK, axis=2)

    def clipped(n, delta):
        return jnp.clip(n + u0 + delta, ctx_chunks, n_chunks - 1)

    def kwin(delta):
        return pl.BlockSpec((1, BLOCK, kv_w), lambda b, n: (b, clipped(n, delta), 0))

    def vwin(delta):
        return pl.BlockSpec((1, kv_w, BLOCK), lambda b, n: (b, 0, clipped(n, delta)))

    return pl.pallas_call(
        functools.partial(_attention_kernel, u0=u0, n_chunks=n_chunks, ctx_chunks=ctx_chunks, n_ctx=n_ctx),
        out_shape=jax.ShapeDtypeStruct((batch, t_total - u0 * BLOCK, q_w), BF16),
        grid=(batch, n_chunks - u0),
        in_specs=[pl.BlockSpec((1, q_w, BLOCK), lambda b, n: (b, 0, n + u0)),
                  kwin(-1), kwin(0), kwin(1),
                  pl.BlockSpec((1, n_ctx, kv_w), lambda b, n: (b, 0, 0)),
                  vwin(-1), vwin(0), vwin(1),
                  pl.BlockSpec((1, kv_w, n_ctx), lambda b, n: (b, 0, 0)),
                  pl.BlockSpec((KV_HEADS, 1, per_kv * BLOCK), lambda b, n: (0, 0, 0))],
        out_specs=pl.BlockSpec((1, BLOCK, q_w), lambda b, n: (b, n, 0)),
        compiler_params=_cparams(2, 40),
        name="attn_core",
    )(qt, k, k, k, k, vt, vt, vt, vt, sink_rows)


def _rope_tables(seq, n_ctx):
    axis_dim = HEAD_DIM // 2
    freqs = axis_dim // 2
    rows = seq // GRID_W
    row_ids = jnp.repeat(jnp.arange(rows), GRID_W).astype(F32)
    col_ids = jnp.tile(jnp.arange(GRID_W), rows).astype(F32)
    inv_freq = ROPE_BASE ** (-jnp.arange(freqs, dtype=F32) * 2.0 / axis_dim)
    ang_r = row_ids[:, None] * inv_freq
    ang_c = col_ids[:, None] * inv_freq
    ang = jnp.concatenate([ang_r, ang_r, ang_c, ang_c], axis=1)
    cos = jnp.cos(ang)
    sin = jnp.sin(ang)
    lane = jnp.arange(HEAD_DIM)
    second_half = (lane % axis_dim) >= freqs
    sa = jnp.where(second_half, sin, 0.0)
    sb = jnp.where(second_half, 0.0, -sin)

    def full(t, ctx_val):
        t = jnp.concatenate([jnp.full((n_ctx, HEAD_DIM), ctx_val, F32), t], axis=0)
        return jnp.tile(t, (1, LANES // HEAD_DIM))

    return full(cos, 1.0), full(sa, 0.0), full(sb, 0.0)


def _swiglu_rows(h, w1_ref, w3_ref, w2_ref, lead, f_chunk, after_first_chunk=None):
    f_total = w1_ref.shape[-1]
    out = None
    for f0 in range(0, f_total, f_chunk):
        sl = lead + (slice(None), slice(f0, f0 + f_chunk))
        a = jnp.dot(h, w1_ref[sl], preferred_element_type=F32)
        b = jnp.dot(h, w3_ref[sl], preferred_element_type=F32)
        u = (_silu(a) * b).astype(BF16)
        part = jnp.dot(u, w2_ref[lead + (slice(f0, f0 + f_chunk), slice(None))], preferred_element_type=F32)
        out = part if out is None else out + part
        if f0 == 0 and after_first_chunk is not None:
            after_first_chunk()
    return out


def _mixer_residual(x_ref, a_ref, wp_ref, bp_ref, mod_ref):
    y = jnp.dot(a_ref[...], wp_ref[...], preferred_element_type=F32) + bp_ref[...]
    return x_ref[...] + mod_ref[0, 2:3, :] * y


def _ffn_dense_kernel(x_ref, a_ref, wp_ref, bp_ref, mod_ref, nw_ref, w1_ref, w3_ref, w2_ref, o_ref, *, f_chunk):
    x = _mixer_residual(x_ref, a_ref, wp_ref, bp_ref, mod_ref)
    h = _norm_mod(x, nw_ref[...], mod_ref[0, 3:4, :], mod_ref[0, 4:5, :]).astype(BF16)
    f = _swiglu_rows(h, w1_ref, w3_ref, w2_ref, (), f_chunk)
    o_ref[...] = x + mod_ref[0, 5:6, :] * f


def _ffn_dense(xs, a, w_proj, b_proj, mods, nw, w1, w3, w2, tiles):
    m, d = xs.shape
    k = a.shape[1]
    f = w1.shape[1]
    f_chunk = f // 2 if (f // 2) % LANES == 0 else f
    assert a.shape[0] == tiles.grid * ROW_TILE
    return pl.pallas_call(
        functools.partial(_ffn_dense_kernel, f_chunk=f_chunk),
        out_shape=jax.ShapeDtypeStruct((m, d), F32),
        grid=(tiles.grid,),
        in_specs=[tiles.x_spec(d), pl.BlockSpec((ROW_TILE, k), lambda i: (i, 0)),
                  _resident((k, d), lambda i: (0, 0)),
                  _resident((1, d), lambda i: (0, 0)),
                  tiles.mod_spec(d),
                  _resident((1, d), lambda i: (0, 0)),
                  _resident((d, f), lambda i: (0, 0)),
                  _resident((d, f), lambda i: (0, 0)),
                  _resident((f, d), lambda i: (0, 0))],
        out_specs=tiles.x_spec(d),
        input_output_aliases={0: 0},
        compiler_params=_cparams(1, 52),
        name="ffn_dense",
    )(xs, a, w_proj, b_proj, mods, nw, w1, w3, w2)


ROW_BLOCKS = 8


def _rows_to_tiles(ref, lead, val):
    rows = val.shape[0]
    for s in range(ROW_BLOCKS):
        ref[lead + (pl.ds(s, rows, stride=ROW_BLOCKS), slice(None))] = val[:, s * LANES:(s + 1) * LANES]


def _tiles_to_rows(ref, lead, rows):
    return jnp.concatenate([ref[lead + (pl.ds(s, rows, stride=ROW_BLOCKS), slice(None))]
                            for s in range(ROW_BLOCKS)], axis=1)


def _moe_router_kernel(x_ref, a_ref, wp_ref, bp_ref, mod_ref, nw_ref, rt_ref, xo_ref, h_ref, idx_ref, gate_ref):
    x = _mixer_residual(x_ref, a_ref, wp_ref, bp_ref, mod_ref)
    xo_ref[...] = x
    h = _norm_mod(x, nw_ref[...], mod_ref[0, 3:4, :], mod_ref[0, 4:5, :])
    _rows_to_tiles(h_ref, (), h)
    logits = lax.dot_general(rt_ref[...], h, (((1,), (1,)), ((), ())),
                             preferred_element_type=F32, precision=HIGHEST)
    e = lax.broadcasted_iota(jnp.int32, logits.shape, 0)
    m1 = jnp.max(logits, axis=0, keepdims=True)
    i1 = jnp.min(jnp.where(logits == m1, e, N_EXPERTS), axis=0, keepdims=True)
    rest = jnp.where(e == i1, -jnp.inf, logits)
    m2 = jnp.max(rest, axis=0, keepdims=True)
    i2 = jnp.min(jnp.where(rest == m2, e, N_EXPERTS), axis=0, keepdims=True)
    t = jnp.exp(m2 - m1)
    idx_ref[...] = jnp.concatenate([i1, i2], axis=0)
    gate_ref[...] = jnp.concatenate([1.0 / (1.0 + t), t / (1.0 + t)], axis=0)


def _moe_router(xs, a, w_proj, b_proj, mods, nw, router_t, tiles):
    m_all, d = xs.shape
    k = a.shape[1]
    m = tiles.grid * ROW_TILE
    assert a.shape[0] == m
    pair = pl.BlockSpec((TOP_K, ROW_TILE), lambda i: (0, i))
    return pl.pallas_call(
        _moe_router_kernel,
        out_shape=(jax.ShapeDtypeStruct((m_all, d), F32),
                   jax.ShapeDtypeStruct((m * ROW_BLOCKS, LANES), F32),
                   jax.ShapeDtypeStruct((TOP_K, m), jnp.int32),
                   jax.ShapeDtypeStruct((TOP_K, m), F32)),
        grid=(tiles.grid,),
        in_specs=[tiles.x_spec(d), pl.BlockSpec((ROW_TILE, k), lambda i: (i, 0)),
                  _resident((k, d), lambda i: (0, 0)),
                  _resident((1, d), lambda i: (0, 0)),
                  tiles.mod_spec(d),
                  _resident((1, d), lambda i: (0, 0)),
                  _resident((N_EXPERTS, d), lambda i: (0, 0))],
        out_specs=(tiles.x_spec(d),
                   pl.BlockSpec((ROW_TILE * ROW_BLOCKS, LANES), lambda i: (i, 0)), pair, pair),
        input_output_aliases={0: 0},
        compiler_params=_cparams(1, 40),
        name="moe_router",
    )(xs, a, w_proj, b_proj, mods, nw, router_t)


def _moe_experts_kernel(te_ref, src_ref, src_next_ref, dst_init_ref, dst_prev_ref, dst_ref, h_hbm, gate_ref,
                        w1_ref, w3_ref, w2_ref, y_hbm, gbuf, obuf, gsem, osem, *, f_chunk):
    j = pl.program_id(0)
    n = pl.num_programs(0)
    slot = j % 2
    other = 1 - slot
    tile = EXPERT_TILE

    def tile_rows(idx):
        return pl.ds(pl.multiple_of(idx, ROW_BLOCKS), ROW_BLOCKS)

    def gather(ref, i, s):
        return pltpu.make_async_copy(h_hbm.at[tile_rows(ref[0, 0, i])],
                                     gbuf.at[s, pl.ds(i * ROW_BLOCKS, ROW_BLOCKS)], gsem.at[s])

    def scatter(ref, i, s):
        return pltpu.make_async_copy(obuf.at[s, pl.ds(i * ROW_BLOCKS, ROW_BLOCKS)],
                                     y_hbm.at[tile_rows(ref[0, 0, i])], osem.at[s])

    def gather_wait(s):
        pltpu.make_async_copy(h_hbm.at[pl.ds(0, tile * ROW_BLOCKS)], gbuf.at[s], gsem.at[s]).wait()

    def scatter_wait(s):
        pltpu.make_async_copy(obuf.at[s], y_hbm.at[pl.ds(0, tile * ROW_BLOCKS)], osem.at[s]).wait()

    @pl.when(j == 0)
    def _():
        obuf[...] = jnp.zeros_like(obuf)

        def body(i, carry):
            gather(src_ref, i, 0).start()
            scatter(dst_init_ref, i, 0).start()
            return carry
        lax.fori_loop(0, tile, body, 0)

    gather_wait(slot)
    x = _tiles_to_rows(gbuf, (slot,), tile).astype(BF16)
    for i in range(tile):
        gather(src_next_ref, i, other).start()
    for i in range(tile):
        scatter(dst_prev_ref, i, other).start()
    f = _swiglu_rows(x, w1_ref, w3_ref, w2_ref, (0,), f_chunk)
    scatter_wait(slot)
    _rows_to_tiles(obuf, (slot,), gate_ref[...] * f)

    @pl.when(j == n - 1)
    def _():
        def body(i, carry):
            scatter(dst_ref, i, slot).start()
            return carry
        lax.fori_loop(0, tile, body, 0)
        gather_wait(other)
        scatter_wait(other)
        scatter_wait(slot)


def _moe_experts(h, tile_expert, src_rows, dst_rows, slot_gate, w1, w3, w2, n_out_rows):
    d = ROW_BLOCKS * LANES
    assert w1.shape[1] == d
    n_tiles = tile_expert.shape[0]
    f = w1.shape[2]
    tile = EXPERT_TILE
    src3 = src_rows.reshape(n_tiles, 1, tile)
    dst3 = dst_rows.reshape(n_tiles + 2, 1, tile)

    def smem(index_map):
        return pl.BlockSpec((1, 1, tile), index_map, memory_space=pltpu.SMEM)

    grid_spec = pltpu.PrefetchScalarGridSpec(
        num_scalar_prefetch=1,
        grid=(n_tiles,),
        in_specs=[
            smem(lambda j, te: (j, 0, 0)),
            smem(lambda j, te: (jnp.minimum(j + 1, n_tiles - 1), 0, 0)),
            smem(lambda j, te: (0, 0, 0)),
            smem(lambda j, te: (j + 1, 0, 0)),
            smem(lambda j, te: (j + 2, 0, 0)),
            pl.BlockSpec(memory_space=pl.ANY),
            pl.BlockSpec((tile, 1), lambda j, te: (j, 0)),
            _resident((1, d, f), lambda j, te: (te[j], 0, 0)),
            _resident((1, d, f), lambda j, te: (te[j], 0, 0)),
            _resident((1, f, d), lambda j, te: (te[j], 0, 0)),
        ],
        out_specs=pl.BlockSpec(memory_space=pl.ANY),
        scratch_shapes=[pltpu.VMEM((2, tile * ROW_BLOCKS, LANES), F32),
                        pltpu.VMEM((2, tile * ROW_BLOCKS, LANES), F32),
                        pltpu.SemaphoreType.DMA((2,)), pltpu.SemaphoreType.DMA((2,))],
    )
    return pl.pallas_call(
        functools.partial(_moe_experts_kernel, f_chunk=f // 2),
        out_shape=jax.ShapeDtypeStruct((n_out_rows * ROW_BLOCKS, LANES), F32),
        grid_spec=grid_spec,
        compiler_params=_cparams(1, 56),
        name="moe_experts",
    )(tile_expert, src3, src3, dst3, dst3, dst3, h, slot_gate, w1, w3, w2)


def _moe_combine_kernel(x_ref, mod_ref, y0_ref, y1_ref, fw_ref, o_ref, *, final):
    y = _tiles_to_rows(y0_ref, (), ROW_TILE) + _tiles_to_rows(y1_ref, (), ROW_TILE)
    x = x_ref[...] + mod_ref[0, 5:6, :] * y
    if final:
        x = x * lax.rsqrt(jnp.mean(x * x, axis=-1, keepdims=True) + EPS) * fw_ref[...]
    o_ref[...] = x


def _moe_combine(xs, mods, y, tiles, final_w=None):
    m, d = xs.shape
    k1 = tiles.grid
    final = final_w is not None
    if final:
        out_shape = jax.ShapeDtypeStruct((tiles.grid * ROW_TILE, d), F32)
        out_spec = pl.BlockSpec((ROW_TILE, d), lambda i: (i, 0))
    else:
        out_shape = jax.ShapeDtypeStruct((m, d), F32)
        out_spec = tiles.x_spec(d)
        final_w = jnp.ones((1, d), F32)
    return pl.pallas_call(
        functools.partial(_moe_combine_kernel, final=final),
        out_shape=out_shape,
        grid=(tiles.grid,),
        in_specs=[tiles.x_spec(d), tiles.mod_spec(d),
                  pl.BlockSpec((ROW_TILE * ROW_BLOCKS, LANES), lambda i: (i, 0)),
                  pl.BlockSpec((ROW_TILE * ROW_BLOCKS, LANES), lambda i: (k1 + i, 0)),
                  pl.BlockSpec((1, d), lambda i: (0, 0))],
        out_specs=out_spec,
        input_output_aliases={} if final else {0: 0},
        compiler_params=_cparams(1, 32),
        name="moe_combine",
    )(xs, mods, y, y, final_w)


def _moe_layer(xs, a, w_proj, b_proj, mods, nw, router, w1, w3, w2, tiles, final_w=None):
    xs, h, top_i, gates = _moe_router(xs, a, w_proj, b_proj, mods, nw, router.T.astype(F32), tiles)

    tile = EXPERT_TILE
    m = tiles.grid * ROW_TILE
    n_assign = TOP_K * m
    eid = top_i.reshape(-1)
    onehot = (eid[:, None] == jnp.arange(N_EXPERTS)[None, :]).astype(jnp.int32)
    csum = jnp.cumsum(onehot, axis=0)
    counts = csum[-1]
    rank = jnp.take_along_axis(csum, eid[:, None], axis=1)[:, 0] - 1
    padded = ((counts + tile - 1) // tile) * tile
    ends = jnp.cumsum(padded)
    starts = ends - padded
    n_slots = n_assign + N_EXPERTS * tile
    n_tiles = n_slots // tile
    pos = starts[eid] + rank
    slot_assign = jnp.zeros((n_slots,), jnp.int32).at[pos].set(jnp.arange(1, n_assign + 1, dtype=jnp.int32))
    valid = slot_assign > 0
    assign = jnp.maximum(slot_assign - 1, 0)
    src_rows = jnp.where(valid, jnp.where(assign >= m, assign - m, assign), 0)
    spare = n_assign + jnp.cumsum(jnp.logical_not(valid).astype(jnp.int32)) - 1
    dst_rows = jnp.where(valid, assign, spare)
    slot_gate = jnp.where(valid, gates.reshape(-1)[assign], 0.0)
    n_out_rows = n_slots + 2 * tile
    prime_rows = n_slots + jnp.arange(2 * tile, dtype=jnp.int32)
    tile_start = jnp.arange(n_tiles, dtype=jnp.int32) * tile
    tile_expert = jnp.sum(jnp.minimum(tile_start, ends[-1] - 1)[:, None] >= ends[None, :], axis=1)
    tile_expert = jnp.clip(tile_expert, 0, N_EXPERTS - 1).astype(jnp.int32)

    y = _moe_experts(h, tile_expert, src_rows.astype(jnp.int32) * ROW_BLOCKS,
                     jnp.concatenate([prime_rows, dst_rows.astype(jnp.int32)]) * ROW_BLOCKS,
                     slot_gate.reshape(-1, 1), w1, w3, w2, n_out_rows)
    return _moe_combine(xs, mods, y, tiles, final_w)


def kernel(x, c, ctx, c_ctx, w_mod, b_mod, norm1_w, norm2_w, ssd_w_in, ssd_conv_w, ssd_conv_b, ssd_dt_bias, ssd_a_log, ssd_d, ssd_norm_w, ssd_w_out, attn_w_qkv, attn_b_qkv, attn_sink, attn_w_o, attn_b_o, ffn_w1, ffn_w3, ffn_w2, moe_router, moe_w1, moe_w3, moe_w2, final_norm_w):
    batch, seq, d = x.shape
    n_ctx = ctx.shape[1]
    t_total = n_ctx + seq
    depth = w_mod.shape[0]
    assert seq % GRID_W == 0 and seq % BLOCK == 0 and n_ctx % ROW_TILE == 0 and depth % 2 == 0

    all_tiles = _Tiles(batch, t_total, n_ctx, 0)
    lat_tiles = _Tiles(batch, t_total, n_ctx, n_ctx // ROW_TILE)

    xs = jnp.concatenate([ctx, x], axis=1).reshape(batch * t_total, d)
    mods_all = _mod_all(c, c_ctx, w_mod, b_mod)
    cos, sa, sb = _rope_tables(seq, n_ctx)
    q_w = Q_HEADS * HEAD_DIM
    q_scale = jnp.concatenate([jnp.full((q_w,), 1.0 / math.sqrt(HEAD_DIM), F32),
                               jnp.ones((attn_w_qkv.shape[2] - q_w,), F32)])

    for i in range(depth):
        last = i == depth - 1
        j = i // 2
        mods = mods_all[i]
        nw1 = norm1_w[i].reshape(1, d)
        nw2 = norm2_w[i].reshape(1, d)
        upd = lat_tiles if last else all_tiles
        if i % 2 == 0:
            g = _ssd_mixer(xs, mods, nw1, ssd_w_in[j], ssd_conv_w[j], ssd_conv_b[j], ssd_dt_bias[j],
                           ssd_a_log[j], ssd_d[j], ssd_norm_w[j], all_tiles, batch, t_total, n_ctx)
            xs = _ffn_dense(xs, g, ssd_w_out[j].astype(BF16), jnp.zeros((1, d), F32), mods, nw2,
                            ffn_w1[j].astype(BF16), ffn_w3[j].astype(BF16), ffn_w2[j].astype(BF16), all_tiles)
        else:
            qt, k, vt = _qkv_rope(xs, mods, nw1, (attn_w_qkv[j] * q_scale).astype(BF16),
                                  (attn_b_qkv[j] * q_scale).reshape(1, -1), cos, sa, sb, all_tiles,
                                  batch, t_total)
            o = _attention(qt, k, vt, attn_sink[j], batch, t_total, n_ctx, not last)
            xs = _moe_layer(xs, o.reshape(-1, q_w), attn_w_o[j].astype(BF16), attn_b_o[j].reshape(1, d),
                            mods, nw2, moe_router[j], moe_w1[j].astype(BF16), moe_w3[j].astype(BF16),
                            moe_w2[j].astype(BF16), upd,
                            final_w=final_norm_w.reshape(1, d) if last else None)
    return xs.reshape(batch, seq, d)
```

```python
import functools
import math

import jax
import jax.numpy as jnp
from jax import lax
from jax.experimental import pallas as pl
from jax.experimental.pallas import tpu as pltpu

F32 = jnp.float32
BF16 = jnp.bfloat16
HIGHEST = lax.Precision.HIGHEST

EPS = 1e-6
LOG2_E = math.log2(math.e)
N_MOD = 6
GRID_W = 64
ROPE_BASE = 10000.0

SSD_HEAD_DIM = 64
SSD_GROUPS = 8
D_STATE = 128
CONV_K = 5
CHUNK = 128
HEADS_PER_GROUP = 4
GROUP_W = HEADS_PER_GROUP * SSD_HEAD_DIM

Q_HEADS = 16
KV_HEADS = 4
HEAD_DIM = 64
BLOCK = 128

N_EXPERTS = 8
TOP_K = 2

ROW_TILE = 256
EXPERT_TILE = 512
LANES = 128
PACK_ROWS_BF16 = 16

MIB = 1024 * 1024


def _cparams(n_axes, vmem_mib):
    return pltpu.CompilerParams(dimension_semantics=("arbitrary",) * n_axes,
                                vmem_limit_bytes=vmem_mib * MIB)


def _resident(block_shape, index_map):
    return pl.BlockSpec(block_shape, index_map, pipeline_mode=pl.Buffered(1))


def _sigmoid(v):
    return 1.0 / (1.0 + jnp.exp(-v))


def _silu(v):
    return v * _sigmoid(v)


def _norm_mod(x, nw, shift, scale):
    ms = jnp.mean(x * x, axis=-1, keepdims=True)
    y = x * lax.rsqrt(ms + EPS) * nw
    return y * (1.0 + scale) + shift


class _Tiles:
    def __init__(self, batch, t_total, n_ctx, lo_tiles):
        assert t_total % ROW_TILE == 0 and n_ctx % ROW_TILE == 0
        self.batch = batch
        self.tpb = t_total // ROW_TILE
        self.ctx_tiles = n_ctx // ROW_TILE
        self.lo = lo_tiles
        self.n_w = self.tpb - lo_tiles
        self.grid = batch * self.n_w

    def split(self, i):
        return i // self.n_w, self.lo + i % self.n_w

    def row(self, i):
        b, w = self.split(i)
        return b * self.tpb + w

    def mod_row(self, i):
        b, w = self.split(i)
        return jnp.where(w < self.ctx_tiles, self.batch, b)

    def x_spec(self, width):
        return pl.BlockSpec((ROW_TILE, width), lambda i: (self.row(i), 0))

    def mod_spec(self, d):
        return pl.BlockSpec((1, N_MOD, d), lambda i: (self.mod_row(i), 0, 0))


def _mod_kernel(c_ref, w_ref, b_ref, o_ref):
    s = _silu(c_ref[...])
    o_ref[0] = jnp.dot(s, w_ref[0], preferred_element_type=F32, precision=HIGHEST) + b_ref[0]


def _mod_all(c, c_ctx, w_mod, b_mod):
    depth, d, n = w_mod.shape
    cc = jnp.concatenate([c, c_ctx[None]], axis=0)
    rows = cc.shape[0]
    tn = 1536
    assert n % tn == 0
    out = pl.pallas_call(
        _mod_kernel,
        out_shape=jax.ShapeDtypeStruct((depth, rows, n), F32),
        grid=(depth, n // tn),
        in_specs=[pl.BlockSpec((rows, d), lambda i, j: (0, 0)),
                  pl.BlockSpec((1, d, tn), lambda i, j: (i, 0, j)),
                  pl.BlockSpec((1, 1, tn), lambda i, j: (i, 0, j))],
        out_specs=pl.BlockSpec((1, rows, tn), lambda i, j: (i, 0, j)),
        compiler_params=_cparams(2, 40),
        name="mod_all",
    )(cc, w_mod, b_mod.reshape(depth, 1, n))
    return out.reshape(depth, rows, N_MOD, d)


def _ssd_inproj_kernel(x_ref, mod_ref, nw_ref, wzx_ref, wdt_ref, zx_ref, dt_ref, *, n_chunk):
    h = _norm_mod(x_ref[...], nw_ref[...], mod_ref[0, 0:1, :], mod_ref[0, 1:2, :]).astype(BF16)
    n = wzx_ref.shape[1]
    for n0 in range(0, n, n_chunk):
        zx_ref[:, n0:n0 + n_chunk] = jnp.dot(
            h, wzx_ref[:, n0:n0 + n_chunk], preferred_element_type=F32).astype(BF16)
    dt_ref[...] = jnp.dot(h, wdt_ref[...], preferred_element_type=F32)


def _ssd_inproj(xs, mods, nw, wzx, wdt, tiles):
    m, d = xs.shape
    n = wzx.shape[1]
    return pl.pallas_call(
        functools.partial(_ssd_inproj_kernel, n_chunk=1536),
        out_shape=(jax.ShapeDtypeStruct((m, n), BF16), jax.ShapeDtypeStruct((m, LANES), F32)),
        grid=(tiles.grid,),
        in_specs=[tiles.x_spec(d), tiles.mod_spec(d),
                  _resident((1, d), lambda i: (0, 0)),
                  _resident((d, n), lambda i: (0, 0)),
                  _resident((d, LANES), lambda i: (0, 0))],
        out_specs=(tiles.x_spec(n), tiles.x_spec(LANES)),
        compiler_params=_cparams(1, 48),
        name="ssd_inproj",
    )(xs, mods, nw, wzx, wdt)


def _dt_prep_kernel(raw_ref, bias_ref, a_ref, dtc_ref, acc_ref, dtr_ref, acr_ref):
    ii = lax.broadcasted_iota(jnp.int32, (CHUNK, CHUNK), 0)
    jj = lax.broadcasted_iota(jnp.int32, (CHUNK, CHUNK), 1)
    lower = (ii >= jj).astype(F32)
    upper = (ii <= jj).astype(F32)
    col = lax.broadcasted_iota(jnp.int32, (CHUNK, LANES), 1)
    is_fwd = (col % (2 * HEADS_PER_GROUP)) < HEADS_PER_GROUP
    for c in range(raw_ref.shape[0] // CHUNK):
        rows = pl.ds(c * CHUNK, CHUNK)
        v = raw_ref[rows, :] + bias_ref[...]
        dt = jnp.maximum(v, 0.0) + jnp.log(1.0 + jnp.exp(-jnp.abs(v)))
        a = dt * a_ref[...]
        prefix = jnp.dot(lower, a, preferred_element_type=F32, precision=HIGHEST)
        suffix = jnp.dot(upper, a, preferred_element_type=F32, precision=HIGHEST)
        ac = jnp.where(is_fwd, prefix, suffix)
        dtc_ref[rows, :] = dt
        acc_ref[rows, :] = ac
        dtr_ref[c] = dt.T
        acr_ref[c] = ac.T


DT_PREP_CHUNKS = 4


def _dt_prep(dt_raw, bias, a_neg):
    m = dt_raw.shape[0]
    nchunks = m // CHUNK
    per = DT_PREP_CHUNKS
    assert nchunks % per == 0
    tile = pl.BlockSpec((per * CHUNK, LANES), lambda i: (i, 0))
    vec = pl.BlockSpec((1, LANES), lambda i: (0, 0))
    sq = pl.BlockSpec((per, LANES, CHUNK), lambda i: (i, 0, 0))
    return pl.pallas_call(
        _dt_prep_kernel,
        out_shape=(jax.ShapeDtypeStruct((m, LANES), F32), jax.ShapeDtypeStruct((m, LANES), F32),
                   jax.ShapeDtypeStruct((nchunks, LANES, CHUNK), F32),
                   jax.ShapeDtypeStruct((nchunks, LANES, CHUNK), F32)),
        grid=(nchunks // per,),
        in_specs=[tile, vec, vec],
        out_specs=(tile, tile, sq, sq),
        compiler_params=_cparams(1, 32),
        name="ssd_dt_prep",
    )(dt_raw, bias, a_neg)


def _split3(v):
    def top_bits(a):
        return lax.bitcast_convert_type(
            lax.bitcast_convert_type(a, jnp.uint32) & jnp.uint32(0xFFFF0000), F32)

    hi = top_bits(v)
    r1 = v - hi
    mid = top_bits(r1)
    lo = r1 - mid
    return hi.astype(BF16), mid.astype(BF16), lo.astype(BF16)


N_BCAST = 4 * CHUNK + 2 * GROUP_W
PIECE_LANES = 3 * 2 * HEADS_PER_GROUP


def _bcast_selector():
    lane = jnp.arange(LANES)[:, None]
    col = jnp.arange(N_BCAST)[None, :]
    n_q = 2 * HEADS_PER_GROUP
    sel = []
    for d in range(2):
        rel = lane - PIECE_LANES * d
        q = rel % n_q
        tile = (col < 4 * CHUNK) & (col // CHUNK == q)
        acc_exp = (col >= 4 * CHUNK) & (col < 4 * CHUNK + GROUP_W) & ((col - 4 * CHUNK) // SSD_HEAD_DIM == q)
        dt_exp = (col >= 4 * CHUNK + GROUP_W) & (
            (col - 4 * CHUNK - GROUP_W) // SSD_HEAD_DIM + HEADS_PER_GROUP == q)
        in_dir = (rel >= 0) & (rel < PIECE_LANES)
        sel.append(((tile | acc_exp | dt_exp) & in_dir).astype(BF16))
    return jnp.stack(sel)


CONV_HALO = 64
CONV_WIN = CHUNK + 2 * CONV_HALO


def _shift_selector():
    pad = CONV_K // 2
    taps = jnp.asarray([k for k in range(CONV_K) if k != pad])
    row = jnp.arange((CONV_K - 1) * CHUNK)
    src = CONV_HALO + row % CHUNK + taps[row // CHUNK] - pad
    return (src[:, None] == jnp.arange(CONV_WIN)[None, :]).astype(BF16)


def _ssd_scan_kernel(z_ref, x_ref, b_ref, c_ref, cw_ref, colp_ref, dtr_ref, acr_ref, e_ref, shift_ref,
                     dsk_ref, nw_ref, o_ref, xc_s, cc_s, bt_s, yf_s, yb_s, sf_s, sb_s, *, n_chunks, ctx_chunks):
    t_total = n_chunks * CHUNK
    halo = CONV_HALO
    pad = CONV_K // 2
    taps = [k for k in range(CONV_K) if k != pad]

    def window(ref, c, t0):
        cur = ref[0, pl.ds(t0, CHUNK), :]
        prev = ref[0, pl.ds(pl.multiple_of(jnp.maximum(t0 - halo, 0), halo), halo), :]
        nxt = ref[0, pl.ds(pl.multiple_of(jnp.minimum(t0 + CHUNK, t_total - halo), halo), halo), :]
        prev_ok = jnp.logical_and(c != 0, c != ctx_chunks)
        next_ok = jnp.logical_and(c != ctx_chunks - 1, c != n_chunks - 1)
        prev = jnp.where(prev_ok, prev, jnp.zeros_like(prev))
        nxt = jnp.where(next_ok, nxt, jnp.zeros_like(nxt))
        return jnp.concatenate([prev, cur, nxt], axis=0)

    def conv_shift(c):
        t0 = pl.multiple_of(c * CHUNK, CHUNK)
        w = jnp.concatenate([window(x_ref, c, t0), window(b_ref, c, t0), window(c_ref, c, t0)], axis=1)
        return c, t0, w, jnp.dot(shift_ref[...], w, preferred_element_type=F32)

    def conv_finish(c, t0, w, shifted):
        acc = cw_ref[0, CONV_K:CONV_K + 1, :] + w[halo:halo + CHUNK].astype(F32) * cw_ref[0, pad:pad + 1, :]
        for n, k in enumerate(taps):
            acc = acc + shifted[n * CHUNK:(n + 1) * CHUNK] * cw_ref[0, k:k + 1, :]
        v = _silu(acc)
        xc_s[pl.ds(t0, CHUNK), :] = v[:, :GROUP_W].astype(BF16)
        bt_s[c] = v[:, GROUP_W:GROUP_W + D_STATE].T.astype(BF16)
        cc_s[pl.ds(t0, CHUNK), :] = v[:, GROUP_W + D_STATE:].astype(BF16)

    def conv_body(p, carry):
        jobs = [conv_shift(2 * p), conv_shift(2 * p + 1)]
        for job in jobs:
            conv_finish(*job)
        return carry

    lax.fori_loop(0, n_chunks // 2, conv_body, 0)

    ii = lax.broadcasted_iota(jnp.int32, (CHUNK, CHUNK), 0)
    jj = lax.broadcasted_iota(jnp.int32, (CHUNK, CHUNK), 1)
    lane_w = lax.broadcasted_iota(jnp.int32, (CHUNK, GROUP_W), 1)
    y_s = (yf_s, yb_s)
    st_s = (sf_s, sb_s)
    sf_s[...] = jnp.zeros_like(sf_s)
    sb_s[...] = jnp.zeros_like(sb_s)

    def chunk_prep(d, c, bcast):
        t0 = pl.multiple_of(c * CHUNK, CHUNK)
        cch = cc_s[pl.ds(t0, CHUNK), :]
        bt = bt_s[c]
        return dict(d=d, c=c, t0=t0, cch=cch, bt=bt,
                    cb=jnp.dot(cch, bt, preferred_element_type=F32),
                    acc_exp=bcast(4 * CHUNK, GROUP_W),
                    dt_exp=bcast(4 * CHUNK + GROUP_W, GROUP_W),
                    acc_pairs=[bcast(r * CHUNK, 2 * CHUNK) for r in range(0, HEADS_PER_GROUP, 2)])

    def chunk_finish(job):
        d, c, t0, cch, bt, cb, acc_exp, dt_exp = (job[k] for k in
                                                  ("d", "c", "t0", "cch", "bt", "cb", "acc_exp", "dt_exp"))
        mask = (ii >= jj) if d == 0 else (ii <= jj)
        last = CHUNK - 1 if d == 0 else 0
        xch = xc_s[pl.ds(t0, CHUNK), :]
        dtr = dtr_ref[0, c]
        acr = acr_ref[0, c]
        ms = []
        xs_ = []
        for r in range(HEADS_PER_GROUP):
            k = 4 * d + r
            seg = job["acc_pairs"][r // 2][:, (r % 2) * CHUNK:(r % 2 + 1) * CHUNK] - acr[k:k + 1, :]
            decay = jnp.exp(jnp.where(mask, seg, -jnp.inf))
            ms.append((cb * decay * dtr[k:k + 1, :]).astype(BF16))
            in_head = jnp.logical_and(lane_w >= r * SSD_HEAD_DIM, lane_w < (r + 1) * SSD_HEAD_DIM)
            xs_.append(jnp.where(in_head, xch, jnp.zeros_like(xch)))
        y = jnp.dot(jnp.concatenate(ms, axis=1), jnp.concatenate(xs_, axis=0), preferred_element_type=F32)
        state = st_s[d][...]
        y = y + jnp.dot(cch, state.astype(BF16), preferred_element_type=F32) * jnp.exp(acc_exp)
        y_s[d][pl.ds(t0, CHUNK), :] = y

        a_last = acc_exp[last:last + 1, :]
        xw = (xch.astype(F32) * (jnp.exp(a_last - acc_exp) * dt_exp)).astype(BF16)
        st_s[d][...] = state * jnp.exp(a_last) + jnp.dot(bt, xw, preferred_element_type=F32)

    def pair_prep(d, c_lo, first_half):
        pieces = colp_ref[0, 0, pl.ds(pl.multiple_of(c_lo * CHUNK, CHUNK), 2 * CHUNK), :]
        products = {}

        def bcast(half, c0, width):
            if (c0, width) not in products:
                products[(c0, width)] = jnp.dot(pieces, e_ref[d, :, c0:c0 + width], preferred_element_type=F32)
            return products[(c0, width)][half * CHUNK:(half + 1) * CHUNK]

        return [chunk_prep(d, c_lo + half, functools.partial(bcast, half))
                for half in (first_half, 1 - first_half)]

    def scan_body(p, carry):
        s = 2 * p
        c_bwd = jnp.where(s < ctx_chunks, ctx_chunks - 1 - s, n_chunks - 1 - (s - ctx_chunks))
        fwd = pair_prep(0, s, 0)
        bwd = pair_prep(1, c_bwd - 1, 1)
        for job in (fwd[0], bwd[0], fwd[1], bwd[1]):
            chunk_finish(job)
        return carry

    assert n_chunks % 2 == 0 and ctx_chunks % 2 == 0
    lax.fori_loop(0, n_chunks // 2, scan_body, 0)

    def out_body(c, carry):
        t0 = pl.multiple_of(c * CHUNK, CHUNK)
        y = (yf_s[pl.ds(t0, CHUNK), :] + yb_s[pl.ds(t0, CHUNK), :]
             + dsk_ref[0] * xc_s[pl.ds(t0, CHUNK), :].astype(F32))
        g = y * _silu(z_ref[0, pl.ds(t0, CHUNK), :].astype(F32))
        g = g * lax.rsqrt(jnp.mean(g * g, axis=-1, keepdims=True) + EPS)
        o_ref[0, pl.ds(t0, CHUNK), :] = (g * nw_ref[0]).astype(BF16)
        return carry

    lax.fori_loop(0, n_chunks, out_body, 0, unroll=2)


def _ssd_scan(zx, cw, colp, dtr, acr, dskip, norm_w, batch, t_total, n_ctx):
    d_inner = SSD_GROUPS * GROUP_W
    n_chunks = t_total // CHUNK
    zx3 = zx.reshape(batch, t_total, zx.shape[-1])
    xoff = d_inner // GROUP_W
    boff = 2 * d_inner // D_STATE
    coff = boff + SSD_GROUPS
    pc = CONV_K + 3
    rows =pl.BlockSpec((1, n_chunks, 2 * HEADS_PER_GROUP, CHUNK), lambda b, g: (b, 0, g, 0))
    vec = pl.BlockSpec((1, 1, GROUP_W), lambda b, g: (g, 0, 0))
    return pl.pallas_call(
        functools.partial(_ssd_scan_kernel, n_chunks=n_chunks, ctx_chunks=n_ctx // CHUNK),
        out_shape=jax.ShapeDtypeStruct((batch, t_total, d_inner), BF16),
        grid=(batch, SSD_GROUPS),
        in_specs=[pl.BlockSpec((1, t_total, GROUP_W), lambda b, g: (b, 0, g)),
                  pl.BlockSpec((1, t_total, GROUP_W), lambda b, g: (b, 0, xoff + g)),
                  pl.BlockSpec((1, t_total, D_STATE), lambda b, g: (b, 0, boff + g)),
                  pl.BlockSpec((1, t_total, D_STATE), lambda b, g: (b, 0, coff + g)),
                  pl.BlockSpec((1, pc, GROUP_W + 2 * D_STATE), lambda b, g: (g, 0, 0)),
                  pl.BlockSpec((1, 1, t_total, LANES), lambda b, g: (b, g, 0, 0)),
                  rows, rows,
                  _resident((2, LANES, N_BCAST), lambda b, g: (0, 0, 0)),
                  _resident(((CONV_K - 1) * CHUNK, CONV_WIN), lambda b, g: (0, 0)),
                  vec, vec],
        out_specs=pl.BlockSpec((1, t_total, GROUP_W), lambda b, g: (b, 0, g)),
        scratch_shapes=[pltpu.VMEM((t_total, GROUP_W), BF16),
                        pltpu.VMEM((t_total, D_STATE), BF16),
                        pltpu.VMEM((n_chunks, D_STATE, CHUNK), BF16),
                        pltpu.VMEM((t_total, GROUP_W), F32),
                        pltpu.VMEM((t_total, GROUP_W), F32),
                        pltpu.VMEM((D_STATE, GROUP_W), F32),
                        pltpu.VMEM((D_STATE, GROUP_W), F32)],
        compiler_params=_cparams(2, 56),
        name="ssd_scan",
    )(zx3, zx3, zx3, zx3, cw, colp, dtr, acr, _bcast_selector(), _shift_selector(), dskip, norm_w)


def _ssd_mixer(xs, mods, nw, w_in, conv_w, conv_b, dt_bias, a_log, d_skip, norm_w,
               tiles, batch, t_total, n_ctx):
    d = xs.shape[1]
    d_inner = SSD_GROUPS * GROUP_W
    conv_ch = d_inner + 2 * SSD_GROUPS * D_STATE
    heads = SSD_GROUPS * HEADS_PER_GROUP
    n_dt = 2 * heads
    perm = jnp.arange(n_dt).reshape(2, SSD_GROUPS, HEADS_PER_GROUP).transpose(1, 0, 2).reshape(-1)
    wzx = w_in[:, :d_inner + conv_ch].astype(BF16)
    wdt = jnp.zeros((d, LANES), F32).at[:, :n_dt].set(w_in[:, d_inner + conv_ch:][:, perm]).astype(BF16)
    bias = jnp.zeros((1, LANES), F32).at[0, :n_dt].set(dt_bias.reshape(-1)[perm])
    a_neg = jnp.zeros((1, LANES), F32).at[0, :n_dt].set(-jnp.exp(a_log.astype(F32)).reshape(-1)[perm])

    zx, dt_raw = _ssd_inproj(xs, mods, nw, wzx, wdt, tiles)
    dtc, acc, dtr, acr = _dt_prep(dt_raw, bias, a_neg)

    def cols(a):
        return a[:, :n_dt].reshape(batch, t_total, SSD_GROUPS, 2, HEADS_PER_GROUP).transpose(0, 2, 1, 3, 4)

    vals = jnp.concatenate([cols(acc), cols(dtc)], axis=-1)
    colp = jnp.stack(_split3(vals), axis=-2).reshape(batch, SSD_GROUPS, t_total, 2 * PIECE_LANES)
    colp = jnp.pad(colp, ((0, 0), (0, 0), (0, 0), (0, LANES - 2 * PIECE_LANES)))

    n_chunks = t_total // CHUNK
    dtr = dtr.reshape(batch, n_chunks, LANES, CHUNK)
    acr = acr.reshape(batch, n_chunks, LANES, CHUNK)

    def per_group(v):
        gx = v[:, :d_inner].reshape(-1, SSD_GROUPS, GROUP_W)
        gb = v[:, d_inner:d_inner + SSD_GROUPS * D_STATE].reshape(-1, SSD_GROUPS, D_STATE)
        gc = v[:, d_inner + SSD_GROUPS * D_STATE:].reshape(-1, SSD_GROUPS, D_STATE)
        return jnp.concatenate([gx, gb, gc], axis=-1).transpose(1, 0, 2)

    cw = per_group(jnp.concatenate([conv_w, conv_b[None], jnp.zeros((2, conv_ch), F32)], axis=0))
    dsk = jnp.repeat((d_skip[0] + d_skip[1]).astype(F32), SSD_HEAD_DIM).reshape(SSD_GROUPS, 1, GROUP_W)
    gnw = norm_w.astype(F32).reshape(SSD_GROUPS, 1, GROUP_W)

    g = _ssd_scan(zx, cw, colp, dtr, acr, dsk, gnw, batch, t_total, n_ctx)
    return g.reshape(batch * t_total, d_inner)


def _transpose_rows(blk):
    return jnp.concatenate([blk[r0:r0 + LANES].T for r0 in range(0, blk.shape[0], LANES)], axis=1)


def _qkv_rope_kernel(x_ref, mod_ref, nw_ref, w_ref, b_ref, cos_ref, sa_ref, sb_ref,
                     qt_ref, k_ref, vt_ref):
    h = _norm_mod(x_ref[...], nw_ref[...], mod_ref[0, 0:1, :], mod_ref[0, 1:2, :]).astype(BF16)
    acc = jnp.dot(h, w_ref[...], preferred_element_type=F32) + b_ref[...]
    cos = cos_ref[...]
    sa = sa_ref[...]
    sb = sb_ref[...]
    half = HEAD_DIM // 4
    q_w = Q_HEADS * HEAD_DIM
    kv_w = KV_HEADS * HEAD_DIM

    def rope(blk):
        return blk * cos + pltpu.roll(blk, half, 1) * sa + pltpu.roll(blk, LANES - half, 1) * sb

    for c0 in range(0, q_w, LANES):
        qt_ref[0, c0:c0 + LANES, :] = _transpose_rows(rope(acc[:, c0:c0 + LANES] * LOG2_E)).astype(BF16)
    for c0 in range(0, kv_w, LANES):
        k_ref[0, :, c0:c0 + LANES] = rope(acc[:, q_w + c0:q_w + c0 + LANES]).astype(BF16)
        vt_ref[0, c0:c0 + LANES, :] = _transpose_rows(
            acc[:, q_w + kv_w + c0:q_w + kv_w + c0 + LANES]).astype(BF16)


def _qkv_rope(xs, mods, nw, w, b, cos, sa, sb, tiles, batch, t_total):
    m, d = xs.shape
    n = w.shape[1]
    q_w = Q_HEADS * HEAD_DIM
    kv_w = KV_HEADS * HEAD_DIM
    tab = pl.BlockSpec((ROW_TILE, LANES), lambda i: (tiles.split(i)[1], 0))

    def feat_major(width):
        return pl.BlockSpec((1, width, ROW_TILE), lambda i: (tiles.split(i)[0], 0, tiles.split(i)[1]))

    return pl.pallas_call(
        _qkv_rope_kernel,
        out_shape=(jax.ShapeDtypeStruct((batch, q_w, t_total), BF16),
                   jax.ShapeDtypeStruct((batch, t_total, kv_w), BF16),
                   jax.ShapeDtypeStruct((batch, kv_w, t_total), BF16)),
        grid=(tiles.grid,),
        in_specs=[tiles.x_spec(d), tiles.mod_spec(d),
                  _resident((1, d), lambda i: (0, 0)),
                  _resident((d, n), lambda i: (0, 0)),
                  _resident((1, n), lambda i: (0, 0)),
                  tab, tab, tab],
        out_specs=(feat_major(q_w),
                   pl.BlockSpec((1, ROW_TILE, kv_w), lambda i: (tiles.split(i)[0], tiles.split(i)[1], 0)),
                   feat_major(kv_w)),
        compiler_params=_cparams(1, 40),
        name="attn_qkv_rope",
    )(xs, mods, nw, w, b, cos, sa, sb)


def _attention_kernel(qt_ref, kp_ref, kc_ref, kn_ref, kx_ref, vp_ref, vc_ref, vn_ref, vx_ref, sink_ref,
                      o_ref, *, u0, n_chunks, ctx_chunks, n_ctx):
    u = pl.program_id(1) + u0
    jj = lax.broadcasted_iota(jnp.int32, (BLOCK, BLOCK), 0)
    ii = lax.broadcasted_iota(jnp.int32, (BLOCK, BLOCK), 1)
    latent = u >= ctx_chunks
    m_prev = jnp.logical_and(jj >= ii, jnp.logical_and(latent, u - 1 >= ctx_chunks))
    m_cur = jnp.logical_and(ii >= 0, latent)
    m_next = jnp.logical_and(ii >= jj, jnp.logical_and(latent, u + 1 <= n_chunks - 1))
    bias = jnp.concatenate([jnp.where(mk, 0.0, -jnp.inf).astype(F32) for mk in (m_prev, m_cur, m_next)]
                           + [jnp.zeros((n_ctx, BLOCK), F32)], axis=0)
    k_all = jnp.concatenate([kp_ref[0], kc_ref[0], kn_ref[0], kx_ref[0]], axis=0)
    vt_all = jnp.concatenate([vp_ref[0], vc_ref[0], vn_ref[0], vx_ref[0]], axis=1)
    per_kv = Q_HEADS // KV_HEADS
    group_w = per_kv * HEAD_DIM
    kv_w = KV_HEADS * HEAD_DIM
    def scores(g):
        q_heads = jnp.concatenate([qt_ref[0, g * group_w + r * HEAD_DIM:g * group_w + (r + 1) * HEAD_DIM, :]
                                   for r in range(per_kv)], axis=1)
        pieces = []
        if g > 0:
            pieces.append(jnp.zeros((g * HEAD_DIM, per_kv * BLOCK), BF16))
        pieces.append(q_heads)
        if g < KV_HEADS - 1:
            pieces.append(jnp.zeros((kv_w - (g + 1) * HEAD_DIM, per_kv * BLOCK), BF16))
        return jnp.dot(k_all, jnp.concatenate(pieces, axis=0), preferred_element_type=F32)

    s_next = scores(0)
    for g in range(KV_HEADS):
        s = s_next
        if g + 1 < KV_HEADS:
            s_next = scores(g + 1)
        s = jnp.concatenate([s[:, r * BLOCK:(r + 1) * BLOCK] + bias for r in range(per_kv)], axis=1)
        sink = sink_ref[g]
        mx = jnp.maximum(jnp.max(s, axis=0, keepdims=True), sink)
        p = jnp.exp2(s - mx)
        den = jnp.sum(p, axis=0, keepdims=True) + jnp.exp2(sink - mx)
        o_t = jnp.dot(vt_all[g * HEAD_DIM:(g + 1) * HEAD_DIM, :], p.astype(BF16),
                      preferred_element_type=F32) / den
        for pair in range(per_kv // 2):
            two = jnp.concatenate([o_t[:, (2 * pair) * BLOCK:(2 * pair + 1) * BLOCK],
                                   o_t[:, (2 * pair + 1) * BLOCK:(2 * pair + 2) * BLOCK]], axis=0)
            c0 = g * group_w + pair * 2 * HEAD_DIM
            o_ref[0, :, c0:c0 + 2 * HEAD_DIM] = two.T.astype(BF16)


def _attention(qt, k, vt, sink, batch, t_total, n_ctx, need_ctx):
    n_chunks = t_total // BLOCK
    ctx_chunks = n_ctx // BLOCK
    u0 = 0 if need_ctx else ctx_chunks
    q_w = Q_HEADS * HEAD_DIM
    kv_w = KV_HEADS * HEAD_DIM
    per_kv = Q_HEADS // KV_HEADS
    assert 2 * HEAD_DIM == LANES and per_kv % 2 == 0
    sink_rows = jnp.repeat((sink.astype(F32) * LOG2_E).reshape(KV_HEADS, 1, per_kv), BLOCK, axis=2)

    def clipped(n, delta):
        return jnp.clip(n + u0 + delta, ctx_chunks, n_chunks - 1)

    def kwin(delta):
        return pl.BlockSpec((1, BLOCK, kv_w), lambda b, n: (b, clipped(n, delta), 0))

    def vwin(delta):
        return pl.BlockSpec((1, kv_w, BLOCK), lambda b, n: (b, 0, clipped(n, delta)))

    return pl.pallas_call(
        functools.partial(_attention_kernel, u0=u0, n_chunks=n_chunks, ctx_chunks=ctx_chunks, n_ctx=n_ctx),
        out_shape=jax.ShapeDtypeStruct((batch, t_total - u0 * BLOCK, q_w), BF16),
        grid=(batch, n_chunks - u0),
        in_specs=[pl.BlockSpec((1, q_w, BLOCK), lambda b, n: (b, 0, n + u0)),
                  kwin(-1), kwin(0), kwin(1),
                  pl.BlockSpec((1, n_ctx, kv_w), lambda b, n: (b, 0, 0)),
                  vwin(-1), vwin(0), vwin(1),
                  pl.BlockSpec((1, kv_w, n_ctx), lambda b, n: (b, 0, 0)),
                  pl.BlockSpec((KV_HEADS, 1, per_kv * BLOCK), lambda b, n: (0, 0, 0))],
        out_specs=pl.BlockSpec((1, BLOCK, q_w), lambda b, n: (b, n, 0)),
        compiler_params=_cparams(2, 40),
        name="attn_core",
    )(qt, k, k, k, k, vt, vt, vt, vt, sink_rows)


def _rope_tables(seq, n_ctx):
    axis_dim = HEAD_DIM // 2
    freqs = axis_dim // 2
    rows = seq // GRID_W
    row_ids = jnp.repeat(jnp.arange(rows), GRID_W).astype(F32)
    col_ids = jnp.tile(jnp.arange(GRID_W), rows).astype(F32)
    inv_freq = ROPE_BASE ** (-jnp.arange(freqs, dtype=F32) * 2.0 / axis_dim)
    ang_r = row_ids[:, None] * inv_freq
    ang_c = col_ids[:, None] * inv_freq
    ang = jnp.concatenate([ang_r, ang_r, ang_c, ang_c], axis=1)
    cos = jnp.cos(ang)
    sin = jnp.sin(ang)
    lane = jnp.arange(HEAD_DIM)
    second_half = (lane % axis_dim) >= freqs
    sa = jnp.where(second_half, sin, 0.0)
    sb = jnp.where(second_half, 0.0, -sin)

    def full(t, ctx_val):
        t = jnp.concatenate([jnp.full((n_ctx, HEAD_DIM), ctx_val, F32), t], axis=0)
        return jnp.tile(t, (1, LANES // HEAD_DIM))

    return full(cos, 1.0), full(sa, 0.0), full(sb, 0.0)


def _swiglu_rows(h, w1_ref, w3_ref, w2_ref, lead, f_chunk, after_first_chunk=None):
    f_total = w1_ref.shape[-1]
    out = None
    for f0 in range(0, f_total, f_chunk):
        sl = lead + (slice(None), slice(f0, f0 + f_chunk))
        a = jnp.dot(h, w1_ref[sl], preferred_element_type=F32)
        b = jnp.dot(h, w3_ref[sl], preferred_element_type=F32)
        u = (_silu(a) * b).astype(BF16)
        part = jnp.dot(u, w2_ref[lead + (slice(f0, f0 + f_chunk), slice(None))], preferred_element_type=F32)
        out = part if out is None else out + part
        if f0 == 0 and after_first_chunk is not None:
            after_first_chunk()
    return out


def _mixer_residual(x_ref, a_ref, wp_ref, bp_ref, mod_ref):
    y = jnp.dot(a_ref[...], wp_ref[...], preferred_element_type=F32) + bp_ref[...]
    return x_ref[...] + mod_ref[0, 2:3, :] * y


def _ffn_dense_kernel(x_ref, a_ref, wp_ref, bp_ref, mod_ref, nw_ref, w1_ref, w3_ref, w2_ref, o_ref, *, f_chunk):
    x = _mixer_residual(x_ref, a_ref, wp_ref, bp_ref, mod_ref)
    h = _norm_mod(x, nw_ref[...], mod_ref[0, 3:4, :], mod_ref[0, 4:5, :]).astype(BF16)
    f = _swiglu_rows(h, w1_ref, w3_ref, w2_ref, (), f_chunk)
    o_ref[...] = x + mod_ref[0, 5:6, :] * f


def _ffn_dense(xs, a, w_proj, b_proj, mods, nw, w1, w3, w2, tiles):
    m, d = xs.shape
    k = a.shape[1]
    f = w1.shape[1]
    f_chunk = f // 2 if (f // 2) % LANES == 0 else f
    assert a.shape[0] == tiles.grid * ROW_TILE
    return pl.pallas_call(
        functools.partial(_ffn_dense_kernel, f_chunk=f_chunk),
        out_shape=jax.ShapeDtypeStruct((m, d), F32),
        grid=(tiles.grid,),
        in_specs=[tiles.x_spec(d), pl.BlockSpec((ROW_TILE, k), lambda i: (i, 0)),
                  _resident((k, d), lambda i: (0, 0)),
                  _resident((1, d), lambda i: (0, 0)),
                  tiles.mod_spec(d),
                  _resident((1, d), lambda i: (0, 0)),
                  _resident((d, f), lambda i: (0, 0)),
                  _resident((d, f), lambda i: (0, 0)),
                  _resident((f, d), lambda i: (0, 0))],
        out_specs=tiles.x_spec(d),
        input_output_aliases={0: 0},
        compiler_params=_cparams(1, 52),
        name="ffn_dense",
    )(xs, a, w_proj, b_proj, mods, nw, w1, w3, w2)


ROW_BLOCKS = 8


def _rows_to_tiles(ref, lead, val):
    rows = val.shape[0]
    for s in range(ROW_BLOCKS):
        ref[lead + (pl.ds(s, rows, stride=ROW_BLOCKS), slice(None))] = val[:, s * LANES:(s + 1) * LANES]


def _tiles_to_rows(ref, lead, rows):
    return jnp.concatenate([ref[lead + (pl.ds(s, rows, stride=ROW_BLOCKS), slice(None))]
                            for s in range(ROW_BLOCKS)], axis=1)


def _moe_router_kernel(x_ref, a_ref, wp_ref, bp_ref, mod_ref, nw_ref, rt_ref, xo_ref, h_ref, idx_ref, gate_ref):
    x = _mixer_residual(x_ref, a_ref, wp_ref, bp_ref, mod_ref)
    xo_ref[...] = x
    h = _norm_mod(x, nw_ref[...], mod_ref[0, 3:4, :], mod_ref[0, 4:5, :])
    _rows_to_tiles(h_ref, (), h)
    logits = lax.dot_general(rt_ref[...], h, (((1,), (1,)), ((), ())),
                             preferred_element_type=F32, precision=HIGHEST)
    e = lax.broadcasted_iota(jnp.int32, logits.shape, 0)
    m1 = jnp.max(logits, axis=0, keepdims=True)
    i1 = jnp.min(jnp.where(logits == m1, e, N_EXPERTS), axis=0, keepdims=True)
    rest = jnp.where(e == i1, -jnp.inf, logits)
    m2 = jnp.max(rest, axis=0, keepdims=True)
    i2 = jnp.min(jnp.where(rest == m2, e, N_EXPERTS), axis=0, keepdims=True)
    t = jnp.exp(m2 - m1)
    idx_ref[...] = jnp.concatenate([i1, i2], axis=0)
    gate_ref[...] = jnp.concatenate([1.0 / (1.0 + t), t / (1.0 + t)], axis=0)


def _moe_router(xs, a, w_proj, b_proj, mods, nw, router_t, tiles):
    m_all, d = xs.shape
    k = a.shape[1]
    m = tiles.grid * ROW_TILE
    assert a.shape[0] == m
    pair = pl.BlockSpec((TOP_K, ROW_TILE), lambda i: (0, i))
    return pl.pallas_call(
        _moe_router_kernel,
        out_shape=(jax.ShapeDtypeStruct((m_all, d), F32),
                   jax.ShapeDtypeStruct((m * ROW_BLOCKS, LANES), F32),
                   jax.ShapeDtypeStruct((TOP_K, m), jnp.int32),
                   jax.ShapeDtypeStruct((TOP_K, m), F32)),
        grid=(tiles.grid,),
        in_specs=[tiles.x_spec(d), pl.BlockSpec((ROW_TILE, k), lambda i: (i, 0)),
                  _resident((k, d), lambda i: (0, 0)),
                  _resident((1, d), lambda i: (0, 0)),
                  tiles.mod_spec(d),
                  _resident((1, d), lambda i: (0, 0)),
                  _resident((N_EXPERTS, d), lambda i: (0, 0))],
        out_specs=(tiles.x_spec(d),
                   pl.BlockSpec((ROW_TILE * ROW_BLOCKS, LANES), lambda i: (i, 0)), pair, pair),
        input_output_aliases={0: 0},
        compiler_params=_cparams(1, 40),
        name="moe_router",
    )(xs, a, w_proj, b_proj, mods, nw, router_t)


def _moe_experts_kernel(te_ref, src_ref, src_next_ref, dst_init_ref, dst_prev_ref, dst_ref, h_hbm, gate_ref,
                        w1_ref, w3_ref, w2_ref, y_hbm, gbuf, obuf, gsem, osem, *, f_chunk):
    j = pl.program_id(0)
    n = pl.num_programs(0)
    slot = j % 2
    other = 1 - slot
    tile = EXPERT_TILE

    def tile_rows(idx):
        return pl.ds(pl.multiple_of(idx, ROW_BLOCKS), ROW_BLOCKS)

    def gather(ref, i, s):
        return pltpu.make_async_copy(h_hbm.at[tile_rows(ref[0, 0, i])],
                                     gbuf.at[s, pl.ds(i * ROW_BLOCKS, ROW_BLOCKS)], gsem.at[s])

    def scatter(ref, i, s):
        return pltpu.make_async_copy(obuf.at[s, pl.ds(i * ROW_BLOCKS, ROW_BLOCKS)],
                                     y_hbm.at[tile_rows(ref[0, 0, i])], osem.at[s])

    def gather_wait(s):
        pltpu.make_async_copy(h_hbm.at[pl.ds(0, tile * ROW_BLOCKS)], gbuf.at[s], gsem.at[s]).wait()

    def scatter_wait(s):
        pltpu.make_async_copy(obuf.at[s], y_hbm.at[pl.ds(0, tile * ROW_BLOCKS)], osem.at[s]).wait()

    @pl.when(j == 0)
    def _():
        obuf[...] = jnp.zeros_like(obuf)

        def body(i, carry):
            gather(src_ref, i, 0).start()
            scatter(dst_init_ref, i, 0).start()
            return carry
        lax.fori_loop(0, tile, body, 0)

    gather_wait(slot)
    x = _tiles_to_rows(gbuf, (slot,), tile).astype(BF16)
    for i in range(tile):
        gather(src_next_ref, i, other).start()
    for i in range(tile):
        scatter(dst_prev_ref, i, other).start()
    f = _swiglu_rows(x, w1_ref, w3_ref, w2_ref, (0,), f_chunk)
    scatter_wait(slot)
    _rows_to_tiles(obuf, (slot,), gate_ref[...] * f)

    @pl.when(j == n - 1)
    def _():
        def body(i, carry):
            scatter(dst_ref, i, slot).start()
            return carry
        lax.fori_loop(0, tile, body, 0)
        gather_wait(other)
        scatter_wait(other)
        scatter_wait(slot)


def _moe_experts(h, tile_expert, src_rows, dst_rows, slot_gate, w1, w3, w2, n_out_rows):
    d = ROW_BLOCKS * LANES
    assert w1.shape[1] == d
    n_tiles = tile_expert.shape[0]
    f = w1.shape[2]
    tile = EXPERT_TILE
    src3 = src_rows.reshape(n_tiles, 1, tile)
    dst3 = dst_rows.reshape(n_tiles + 2, 1, tile)

    def smem(index_map):
        return pl.BlockSpec((1, 1, tile), index_map, memory_space=pltpu.SMEM)

    grid_spec = pltpu.PrefetchScalarGridSpec(
        num_scalar_prefetch=1,
        grid=(n_tiles,),
        in_specs=[
            smem(lambda j, te: (j, 0, 0)),
            smem(lambda j, te: (jnp.minimum(j + 1, n_tiles - 1), 0, 0)),
            smem(lambda j, te: (0, 0, 0)),
            smem(lambda j, te: (j + 1, 0, 0)),
            smem(lambda j, te: (j + 2, 0, 0)),
            pl.BlockSpec(memory_space=pl.ANY),
            pl.BlockSpec((tile, 1), lambda j, te: (j, 0)),
            _resident((1, d, f), lambda j, te: (te[j], 0, 0)),
            _resident((1, d, f), lambda j, te: (te[j], 0, 0)),
            _resident((1, f, d), lambda j, te: (te[j], 0, 0)),
        ],
        out_specs=pl.BlockSpec(memory_space=pl.ANY),
        scratch_shapes=[pltpu.VMEM((2, tile * ROW_BLOCKS, LANES), F32),
                        pltpu.VMEM((2, tile * ROW_BLOCKS, LANES), F32),
                        pltpu.SemaphoreType.DMA((2,)), pltpu.SemaphoreType.DMA((2,))],
    )
    return pl.pallas_call(
        functools.partial(_moe_experts_kernel, f_chunk=f // 2),
        out_shape=jax.ShapeDtypeStruct((n_out_rows * ROW_BLOCKS, LANES), F32),
        grid_spec=grid_spec,
        compiler_params=_cparams(1, 56),
        name="moe_experts",
    )(tile_expert, src3, src3, dst3, dst3, dst3, h, slot_gate, w1, w3, w2)


def _moe_combine_kernel(x_ref, mod_ref, y0_ref, y1_ref, fw_ref, o_ref, *, final):
    y = _tiles_to_rows(y0_ref, (), ROW_TILE) + _tiles_to_rows(y1_ref, (), ROW_TILE)
    x = x_ref[...] + mod_ref[0, 5:6, :] * y
    if final:
        x = x * lax.rsqrt(jnp.mean(x * x, axis=-1, keepdims=True) + EPS) * fw_ref[...]
    o_ref[...] = x


def _moe_combine(xs, mods, y, tiles, final_w=None):
    m, d = xs.shape
    k1 = tiles.grid
    final = final_w is not None
    if final:
        out_shape = jax.ShapeDtypeStruct((tiles.grid * ROW_TILE, d), F32)
        out_spec = pl.BlockSpec((ROW_TILE, d), lambda i: (i, 0))
    else:
        out_shape = jax.ShapeDtypeStruct((m, d), F32)
        out_spec = tiles.x_spec(d)
        final_w = jnp.ones((1, d), F32)
    return pl.pallas_call(
        functools.partial(_moe_combine_kernel, final=final),
        out_shape=out_shape,
        grid=(tiles.grid,),
        in_specs=[tiles.x_spec(d), tiles.mod_spec(d),
                  pl.BlockSpec((ROW_TILE * ROW_BLOCKS, LANES), lambda i: (i, 0)),
                  pl.BlockSpec((ROW_TILE * ROW_BLOCKS, LANES), lambda i: (k1 + i, 0)),
                  pl.BlockSpec((1, d), lambda i: (0, 0))],
        out_specs=out_spec,
        input_output_aliases={} if final else {0: 0},
        compiler_params=_cparams(1, 32),
        name="moe_combine",
    )(xs, mods, y, y, final_w)


def _moe_layer(xs, a, w_proj, b_proj, mods, nw, router, w1, w3, w2, tiles, final_w=None):
    xs, h, top_i, gates = _moe_router(xs, a, w_proj, b_proj, mods, nw, router.T.astype(F32), tiles)

    tile = EXPERT_TILE
    m = tiles.grid * ROW_TILE
    n_assign = TOP_K * m
    eid = top_i.reshape(-1)
    onehot = (eid[:, None] == jnp.arange(N_EXPERTS)[None, :]).astype(jnp.int32)
    csum = jnp.cumsum(onehot, axis=0)
    counts = csum[-1]
    rank = jnp.take_along_axis(csum, eid[:, None], axis=1)[:, 0] - 1
    padded = ((counts + tile - 1) // tile) * tile
    ends = jnp.cumsum(padded)
    starts = ends - padded
    n_slots = n_assign + N_EXPERTS * tile
    n_tiles = n_slots // tile
    pos = starts[eid] + rank
    slot_assign = jnp.zeros((n_slots,), jnp.int32).at[pos].set(jnp.arange(1, n_assign + 1, dtype=jnp.int32))
    valid = slot_assign > 0
    assign = jnp.maximum(slot_assign - 1, 0)
    src_rows = jnp.where(valid, jnp.where(assign >= m, assign - m, assign), 0)
    spare = n_assign + jnp.cumsum(jnp.logical_not(valid).astype(jnp.int32)) - 1
    dst_rows = jnp.where(valid, assign, spare)
    slot_gate = jnp.where(valid, gates.reshape(-1)[assign], 0.0)
    n_out_rows = n_slots + 2 * tile
    prime_rows = n_slots + jnp.arange(2 * tile, dtype=jnp.int32)
    tile_start = jnp.arange(n_tiles, dtype=jnp.int32) * tile
    tile_expert = jnp.sum(jnp.minimum(tile_start, ends[-1] - 1)[:, None] >= ends[None, :], axis=1)
    tile_expert = jnp.clip(tile_expert, 0, N_EXPERTS - 1).astype(jnp.int32)

    y = _moe_experts(h, tile_expert, src_rows.astype(jnp.int32) * ROW_BLOCKS,
                     jnp.concatenate([prime_rows, dst_rows.astype(jnp.int32)]) * ROW_BLOCKS,
                     slot_gate.reshape(-1, 1), w1, w3, w2, n_out_rows)
    return _moe_combine(xs, mods, y, tiles, final_w)


def kernel(x, c, ctx, c_ctx, w_mod, b_mod, norm1_w, norm2_w, ssd_w_in, ssd_conv_w, ssd_conv_b, ssd_dt_bias, ssd_a_log, ssd_d, ssd_norm_w, ssd_w_out, attn_w_qkv, attn_b_qkv, attn_sink, attn_w_o, attn_b_o, ffn_w1, ffn_w3, ffn_w2, moe_router, moe_w1, moe_w3, moe_w2, final_norm_w):
    batch, seq, d = x.shape
    n_ctx = ctx.shape[1]
    t_total = n_ctx + seq
    depth = w_mod.shape[0]
    assert seq % GRID_W == 0 and seq % BLOCK == 0 and n_ctx % ROW_TILE == 0 and depth % 2 == 0

    all_tiles = _Tiles(batch, t_total, n_ctx, 0)
    lat_tiles = _Tiles(batch, t_total, n_ctx, n_ctx // ROW_TILE)

    xs = jnp.concatenate([ctx, x], axis=1).reshape(batch * t_total, d)
    mods_all = _mod_all(c, c_ctx, w_mod, b_mod)
    cos, sa, sb = _rope_tables(seq, n_ctx)
    q_w = Q_HEADS * HEAD_DIM
    q_scale = jnp.concatenate([jnp.full((q_w,), 1.0 / math.sqrt(HEAD_DIM), F32),
                               jnp.ones((attn_w_qkv.shape[2] - q_w,), F32)])

    for i in range(depth):
        last = i == depth - 1
        j = i // 2
        mods = mods_all[i]
        nw1 = norm1_w[i].reshape(1, d)
        nw2 = norm2_w[i].reshape(1, d)
        upd = lat_tiles if last else all_tiles
        if i % 2 == 0:
            g = _ssd_mixer(xs, mods, nw1, ssd_w_in[j], ssd_conv_w[j], ssd_conv_b[j], ssd_dt_bias[j],
                           ssd_a_log[j], ssd_d[j], ssd_norm_w[j], all_tiles, batch, t_total, n_ctx)
            xs = _ffn_dense(xs, g, ssd_w_out[j].astype(BF16), jnp.zeros((1, d), F32), mods, nw2,
                            ffn_w1[j].astype(BF16), ffn_w3[j].astype(BF16), ffn_w2[j].astype(BF16), all_tiles)
        else:
            qt, k, vt = _qkv_rope(xs, mods, nw1, (attn_w_qkv[j] * q_scale).astype(BF16),
                                  (attn_b_qkv[j] * q_scale).reshape(1, -1), cos, sa, sb, all_tiles,
                                  batch, t_total)
            o = _attention(qt, k, vt, attn_sink[j], batch, t_total, n_ctx, not last)
            xs = _moe_layer(xs, o.reshape(-1, q_w), attn_w_o[j].astype(BF16), attn_b_o[j].reshape(1, d),
                            mods, nw2, moe_router[j], moe_w1[j].astype(BF16), moe_w3[j].astype(BF16),
                            moe_w2[j].astype(BF16), upd,
                            final_w=final_norm_w.reshape(1, d) if last else None)
    return xs.reshape(batch, seq, d)
```

```python
import functools
import math

import jax
import jax.numpy as jnp
from jax import lax
from jax.experimental import pallas as pl
from jax.experimental.pallas import tpu as pltpu

F32 = jnp.float32
BF16 = jnp.bfloat16
HIGHEST = lax.Precision.HIGHEST

EPS = 1e-6
LOG2_E = math.log2(math.e)
N_MOD = 6
GRID_W = 64
ROPE_BASE = 10000.0

SSD_HEAD_DIM = 64
SSD_GROUPS = 8
D_STATE = 128
CONV_K = 5
CHUNK = 128
HEADS_PER_GROUP = 4
GROUP_W = HEADS_PER_GROUP * SSD_HEAD_DIM

Q_HEADS = 16
KV_HEADS = 4
HEAD_DIM = 64
BLOCK = 128

N_EXPERTS = 8
TOP_K = 2

ROW_TILE = 256
EXPERT_TILE = 512
LANES = 128
PACK_ROWS_BF16 = 16

MIB = 1024 * 1024


def _cparams(n_axes, vmem_mib):
    return pltpu.CompilerParams(dimension_semantics=("arbitrary",) * n_axes,
                                vmem_limit_bytes=vmem_mib * MIB)


def _resident(block_shape, index_map):
    return pl.BlockSpec(block_shape, index_map, pipeline_mode=pl.Buffered(1))


def _sigmoid(v):
    return 1.0 / (1.0 + jnp.exp(-v))


def _silu(v):
    return v * _sigmoid(v)


def _norm_mod(x, nw, shift, scale):
    ms = jnp.mean(x * x, axis=-1, keepdims=True)
    y = x * lax.rsqrt(ms + EPS) * nw
    return y * (1.0 + scale) + shift


class _Tiles:
    def __init__(self, batch, t_total, n_ctx, lo_tiles):
        assert t_total % ROW_TILE == 0 and n_ctx % ROW_TILE == 0
        self.batch = batch
        self.tpb = t_total // ROW_TILE
        self.ctx_tiles = n_ctx // ROW_TILE
        self.lo = lo_tiles
        self.n_w = self.tpb - lo_tiles
        self.grid = batch * self.n_w

    def split(self, i):
        return i // self.n_w, self.lo + i % self.n_w

    def row(self, i):
        b, w = self.split(i)
        return b * self.tpb + w

    def mod_row(self, i):
        b, w = self.split(i)
        return jnp.where(w < self.ctx_tiles, self.batch, b)

    def x_spec(self, width):
        return pl.BlockSpec((ROW_TILE, width), lambda i: (self.row(i), 0))

    def mod_spec(self, d):
        return pl.BlockSpec((1, N_MOD, d), lambda i: (self.mod_row(i), 0, 0))


def _mod_kernel(c_ref, w_ref, b_ref, o_ref):
    s = _silu(c_ref[...])
    o_ref[0] = jnp.dot(s, w_ref[0], preferred_element_type=F32, precision=HIGHEST) + b_ref[0]


def _mod_all(c, c_ctx, w_mod, b_mod):
    depth, d, n = w_mod.shape
    cc = jnp.concatenate([c, c_ctx[None]], axis=0)
    rows = cc.shape[0]
    tn = 1536
    assert n % tn == 0
    out = pl.pallas_call(
        _mod_kernel,
        out_shape=jax.ShapeDtypeStruct((depth, rows, n), F32),
        grid=(depth, n // tn),
        in_specs=[pl.BlockSpec((rows, d), lambda i, j: (0, 0)),
                  pl.BlockSpec((1, d, tn), lambda i, j: (i, 0, j)),
                  pl.BlockSpec((1, 1, tn), lambda i, j: (i, 0, j))],
        out_specs=pl.BlockSpec((1, rows, tn), lambda i, j: (i, 0, j)),
        compiler_params=_cparams(2, 40),
        name="mod_all",
    )(cc, w_mod, b_mod.reshape(depth, 1, n))
    return out.reshape(depth, rows, N_MOD, d)


def _ssd_inproj_kernel(x_ref, mod_ref, nw_ref, wzx_ref, wdt_ref, zx_ref, dt_ref, *, n_chunk):
    h = _norm_mod(x_ref[...], nw_ref[...], mod_ref[0, 0:1, :], mod_ref[0, 1:2, :]).astype(BF16)
    n = wzx_ref.shape[1]
    for n0 in range(0, n, n_chunk):
        zx_ref[:, n0:n0 + n_chunk] = jnp.dot(
            h, wzx_ref[:, n0:n0 + n_chunk], preferred_element_type=F32).astype(BF16)
    dt_ref[...] = jnp.dot(h, wdt_ref[...], preferred_element_type=F32)


def _ssd_inproj(xs, mods, nw, wzx, wdt, tiles):
    m, d = xs.shape
    n = wzx.shape[1]
    return pl.pallas_call(
        functools.partial(_ssd_inproj_kernel, n_chunk=1536),
        out_shape=(jax.ShapeDtypeStruct((m, n), BF16), jax.ShapeDtypeStruct((m, LANES), F32)),
        grid=(tiles.grid,),
        in_specs=[tiles.x_spec(d), tiles.mod_spec(d),
                  _resident((1, d), lambda i: (0, 0)),
                  _resident((d, n), lambda i: (0, 0)),
                  _resident((d, LANES), lambda i: (0, 0))],
        out_specs=(tiles.x_spec(n), tiles.x_spec(LANES)),
        compiler_params=_cparams(1, 48),
        name="ssd_inproj",
    )(xs, mods, nw, wzx, wdt)


def _dt_prep_kernel(raw_ref, bias_ref, a_ref, dtc_ref, acc_ref, dtr_ref, acr_ref):
    ii = lax.broadcasted_iota(jnp.int32, (CHUNK, CHUNK), 0)
    jj = lax.broadcasted_iota(jnp.int32, (CHUNK, CHUNK), 1)
    lower = (ii >= jj).astype(F32)
    upper = (ii <= jj).astype(F32)
    col = lax.broadcasted_iota(jnp.int32, (CHUNK, LANES), 1)
    is_fwd = (col % (2 * HEADS_PER_GROUP)) < HEADS_PER_GROUP
    for c in range(raw_ref.shape[0] // CHUNK):
        rows = pl.ds(c * CHUNK, CHUNK)
        v = raw_ref[rows, :] + bias_ref[...]
        dt = jnp.maximum(v, 0.0) + jnp.log(1.0 + jnp.exp(-jnp.abs(v)))
        a = dt * a_ref[...]
        prefix = jnp.dot(lower, a, preferred_element_type=F32, precision=HIGHEST)
        suffix = jnp.dot(upper, a, preferred_element_type=F32, precision=HIGHEST)
        ac = jnp.where(is_fwd, prefix, suffix)
        dtc_ref[rows, :] = dt
        acc_ref[rows, :] = ac
        dtr_ref[c] = dt.T
        acr_ref[c] = ac.T


DT_PREP_CHUNKS = 4


def _dt_prep(dt_raw, bias, a_neg):
    m = dt_raw.shape[0]
    nchunks = m // CHUNK
    per = DT_PREP_CHUNKS
    assert nchunks % per == 0
    tile = pl.BlockSpec((per * CHUNK, LANES), lambda i: (i, 0))
    vec = pl.BlockSpec((1, LANES), lambda i: (0, 0))
    sq = pl.BlockSpec((per, LANES, CHUNK), lambda i: (i, 0, 0))
    return pl.pallas_call(
        _dt_prep_kernel,
        out_shape=(jax.ShapeDtypeStruct((m, LANES), F32), jax.ShapeDtypeStruct((m, LANES), F32),
                   jax.ShapeDtypeStruct((nchunks, LANES, CHUNK), F32),
                   jax.ShapeDtypeStruct((nchunks, LANES, CHUNK), F32)),
        grid=(nchunks // per,),
        in_specs=[tile, vec, vec],
        out_specs=(tile, tile, sq, sq),
        compiler_params=_cparams(1, 32),
        name="ssd_dt_prep",
    )(dt_raw, bias, a_neg)


def _split3(v):
    def top_bits(a):
        return lax.bitcast_convert_type(
            lax.bitcast_convert_type(a, jnp.uint32) & jnp.uint32(0xFFFF0000), F32)

    hi = top_bits(v)
    r1 = v - hi
    mid = top_bits(r1)
    lo = r1 - mid
    return hi.astype(BF16), mid.astype(BF16), lo.astype(BF16)


N_BCAST = 4 * CHUNK + 2 * GROUP_W
PIECE_LANES = 3 * 2 * HEADS_PER_GROUP


def _bcast_selector():
    lane = jnp.arange(LANES)[:, None]
    col = jnp.arange(N_BCAST)[None, :]
    n_q = 2 * HEADS_PER_GROUP
    sel = []
    for d in range(2):
        rel = lane - PIECE_LANES * d
        q = rel % n_q
        tile = (col < 4 * CHUNK) & (col // CHUNK == q)
        acc_exp = (col >= 4 * CHUNK) & (col < 4 * CHUNK + GROUP_W) & ((col - 4 * CHUNK) // SSD_HEAD_DIM == q)
        dt_exp = (col >= 4 * CHUNK + GROUP_W) & (
            (col - 4 * CHUNK - GROUP_W) // SSD_HEAD_DIM + HEADS_PER_GROUP == q)
        in_dir = (rel >= 0) & (rel < PIECE_LANES)
        sel.append(((tile | acc_exp | dt_exp) & in_dir).astype(BF16))
    return jnp.stack(sel)


CONV_HALO = 64
CONV_WIN = CHUNK + 2 * CONV_HALO


def _shift_selector():
    pad = CONV_K // 2
    taps = jnp.asarray([k for k in range(CONV_K) if k != pad])
    row = jnp.arange((CONV_K - 1) * CHUNK)
    src = CONV_HALO + row % CHUNK + taps[row // CHUNK] - pad
    return (src[:, None] == jnp.arange(CONV_WIN)[None, :]).astype(BF16)


def _ssd_scan_kernel(z_ref, x_ref, b_ref, c_ref, cw_ref, colp_ref, dtr_ref, acr_ref, e_ref, shift_ref,
                     dsk_ref, nw_ref, o_ref, xc_s, cc_s, bt_s, yf_s, yb_s, sf_s, sb_s, *, n_chunks, ctx_chunks):
    t_total = n_chunks * CHUNK
    halo = CONV_HALO
    pad = CONV_K // 2
    taps = [k for k in range(CONV_K) if k != pad]

    def window(ref, c, t0):
        cur = ref[0, pl.ds(t0, CHUNK), :]
        prev = ref[0, pl.ds(pl.multiple_of(jnp.maximum(t0 - halo, 0), halo), halo), :]
        nxt = ref[0, pl.ds(pl.multiple_of(jnp.minimum(t0 + CHUNK, t_total - halo), halo), halo), :]
        prev_ok = jnp.logical_and(c != 0, c != ctx_chunks)
        next_ok = jnp.logical_and(c != ctx_chunks - 1, c != n_chunks - 1)
        prev = jnp.where(prev_ok, prev, jnp.zeros_like(prev))
        nxt = jnp.where(next_ok, nxt, jnp.zeros_like(nxt))
        return jnp.concatenate([prev, cur, nxt], axis=0)

    def conv_shift(c):
        t0 = pl.multiple_of(c * CHUNK, CHUNK)
        w = jnp.concatenate([window(x_ref, c, t0), window(b_ref, c, t0), window(c_ref, c, t0)], axis=1)
        return c, t0, w, jnp.dot(shift_ref[...], w, preferred_element_type=F32)

    def conv_finish(c, t0, w, shifted):
        acc = cw_ref[0, CONV_K:CONV_K + 1, :] + w[halo:halo + CHUNK].astype(F32) * cw_ref[0, pad:pad + 1, :]
        for n, k in enumerate(taps):
            acc = acc + shifted[n * CHUNK:(n + 1) * CHUNK] * cw_ref[0, k:k + 1, :]
        v = _silu(acc)
        xc_s[pl.ds(t0, CHUNK), :] = v[:, :GROUP_W].astype(BF16)
        bt_s[c] = v[:, GROUP_W:GROUP_W + D_STATE].T.astype(BF16)
        cc_s[pl.ds(t0, CHUNK), :] = v[:, GROUP_W + D_STATE:].astype(BF16)

    def conv_body(p, carry):
        jobs = [conv_shift(2 * p), conv_shift(2 * p + 1)]
        for job in jobs:
            conv_finish(*job)
        return carry

    lax.fori_loop(0, n_chunks // 2, conv_body, 0)

    ii = lax.broadcasted_iota(jnp.int32, (CHUNK, CHUNK), 0)
    jj = lax.broadcasted_iota(jnp.int32, (CHUNK, CHUNK), 1)
    lane_w = lax.broadcasted_iota(jnp.int32, (CHUNK, GROUP_W), 1)
    y_s = (yf_s, yb_s)
    st_s = (sf_s, sb_s)
    sf_s[...] = jnp.zeros_like(sf_s)
    sb_s[...] = jnp.zeros_like(sb_s)

    def chunk_prep(d, c, bcast):
        t0 = pl.multiple_of(c * CHUNK, CHUNK)
        cch = cc_s[pl.ds(t0, CHUNK), :]
        bt = bt_s[c]
        return dict(d=d, c=c, t0=t0, cch=cch, bt=bt,
                    cb=jnp.dot(cch, bt, preferred_element_type=F32),
                    acc_exp=bcast(4 * CHUNK, GROUP_W),
                    dt_exp=bcast(4 * CHUNK + GROUP_W, GROUP_W),
                    acc_pairs=[bcast(r * CHUNK, 2 * CHUNK) for r in range(0, HEADS_PER_GROUP, 2)])

    def chunk_finish(job):
        d, c, t0, cch, bt, cb, acc_exp, dt_exp = (job[k] for k in
                                                  ("d", "c", "t0", "cch", "bt", "cb", "acc_exp", "dt_exp"))
        mask = (ii >= jj) if d == 0 else (ii <= jj)
        last = CHUNK - 1 if d == 0 else 0
        xch = xc_s[pl.ds(t0, CHUNK), :]
        dtr = dtr_ref[0, c]
        acr = acr_ref[0, c]
        ms = []
        xs_ = []
        for r in range(HEADS_PER_GROUP):
            k = 4 * d + r
            seg = job["acc_pairs"][r // 2][:, (r % 2) * CHUNK:(r % 2 + 1) * CHUNK] - acr[k:k + 1, :]
            decay = jnp.exp(jnp.where(mask, seg, -jnp.inf))
            ms.append((cb * decay * dtr[k:k + 1, :]).astype(BF16))
            in_head = jnp.logical_and(lane_w >= r * SSD_HEAD_DIM, lane_w < (r + 1) * SSD_HEAD_DIM)
            xs_.append(jnp.where(in_head, xch, jnp.zeros_like(xch)))
        y = jnp.dot(jnp.concatenate(ms, axis=1), jnp.concatenate(xs_, axis=0), preferred_element_type=F32)
        state = st_s[d][...]
        y = y + jnp.dot(cch, state.astype(BF16), preferred_element_type=F32) * jnp.exp(acc_exp)
        y_s[d][pl.ds(t0, CHUNK), :] = y

        a_last = acc_exp[last:last + 1, :]
        xw = (xch.astype(F32) * (jnp.exp(a_last - acc_exp) * dt_exp)).astype(BF16)
        st_s[d][...] = state * jnp.exp(a_last) + jnp.dot(bt, xw, preferred_element_type=F32)

    def pair_prep(d, c_lo, first_half):
        pieces = colp_ref[0, 0, pl.ds(pl.multiple_of(c_lo * CHUNK, CHUNK), 2 * CHUNK), :]
        products = {}

        def bcast(half, c0, width):
            if (c0, width) not in products:
                products[(c0, width)] = jnp.dot(pieces, e_ref[d, :, c0:c0 + width], preferred_element_type=F32)
            return products[(c0, width)][half * CHUNK:(half + 1) * CHUNK]

        return [chunk_prep(d, c_lo + half, functools.partial(bcast, half))
                for half in (first_half, 1 - first_half)]

    def scan_body(p, carry):
        s = 2 * p
        c_bwd = jnp.where(s < ctx_chunks, ctx_chunks - 1 - s, n_chunks - 1 - (s - ctx_chunks))
        fwd = pair_prep(0, s, 0)
        bwd = pair_prep(1, c_bwd - 1, 1)
        for job in (fwd[0], bwd[0], fwd[1], bwd[1]):
            chunk_finish(job)
        return carry

    assert n_chunks % 2 == 0 and ctx_chunks % 2 == 0
    lax.fori_loop(0, n_chunks // 2, scan_body, 0)

    def out_body(c, carry):
        t0 = pl.multiple_of(c * CHUNK, CHUNK)
        y = (yf_s[pl.ds(t0, CHUNK), :] + yb_s[pl.ds(t0, CHUNK), :]
             + dsk_ref[0] * xc_s[pl.ds(t0, CHUNK), :].astype(F32))
        g = y * _silu(z_ref[0, pl.ds(t0, CHUNK), :].astype(F32))
        g = g * lax.rsqrt(jnp.mean(g * g, axis=-1, keepdims=True) + EPS)
        o_ref[0, pl.ds(t0, CHUNK), :] = (g * nw_ref[0]).astype(BF16)
        return carry

    lax.fori_loop(0, n_chunks, out_body, 0, unroll=2)


def _ssd_scan(zx, cw, colp, dtr, acr, dskip, norm_w, batch, t_total, n_ctx):
    d_inner = SSD_GROUPS * GROUP_W
    n_chunks = t_total // CHUNK
    zx3 = zx.reshape(batch, t_total, zx.shape[-1])
    xoff = d_inner // GROUP_W
    boff = 2 * d_inner // D_STATE
    coff = boff + SSD_GROUPS
    pc = CONV_K + 3
    rows =pl.BlockSpec((1, n_chunks, 2 * HEADS_PER_GROUP, CHUNK), lambda b, g: (b, 0, g, 0))
    vec = pl.BlockSpec((1, 1, GROUP_W), lambda b, g: (g, 0, 0))
    return pl.pallas_call(
        functools.partial(_ssd_scan_kernel, n_chunks=n_chunks, ctx_chunks=n_ctx // CHUNK),
        out_shape=jax.ShapeDtypeStruct((batch, t_total, d_inner), BF16),
        grid=(batch, SSD_GROUPS),
        in_specs=[pl.BlockSpec((1, t_total, GROUP_W), lambda b, g: (b, 0, g)),
                  pl.BlockSpec((1, t_total, GROUP_W), lambda b, g: (b, 0, xoff + g)),
                  pl.BlockSpec((1, t_total, D_STATE), lambda b, g: (b, 0, boff + g)),
                  pl.BlockSpec((1, t_total, D_STATE), lambda b, g: (b, 0, coff + g)),
                  pl.BlockSpec((1, pc, GROUP_W + 2 * D_STATE), lambda b, g: (g, 0, 0)),
                  pl.BlockSpec((1, 1, t_total, LANES), lambda b, g: (b, g, 0, 0)),
                  rows, rows,
                  _resident((2, LANES, N_BCAST), lambda b, g: (0, 0, 0)),
                  _resident(((CONV_K - 1) * CHUNK, CONV_WIN), lambda b, g: (0, 0)),
                  vec, vec],
        out_specs=pl.BlockSpec((1, t_total, GROUP_W), lambda b, g: (b, 0, g)),
        scratch_shapes=[pltpu.VMEM((t_total, GROUP_W), BF16),
                        pltpu.VMEM((t_total, D_STATE), BF16),
                        pltpu.VMEM((n_chunks, D_STATE, CHUNK), BF16),
                        pltpu.VMEM((t_total, GROUP_W), F32),
                        pltpu.VMEM((t_total, GROUP_W), F32),
                        pltpu.VMEM((D_STATE, GROUP_W), F32),
                        pltpu.VMEM((D_STATE, GROUP_W), F32)],
        compiler_params=_cparams(2, 56),
        name="ssd_scan",
    )(zx3, zx3, zx3, zx3, cw, colp, dtr, acr, _bcast_selector(), _shift_selector(), dskip, norm_w)


def _ssd_mixer(xs, mods, nw, w_in, conv_w, conv_b, dt_bias, a_log, d_skip, norm_w,
               tiles, batch, t_total, n_ctx):
    d = xs.shape[1]
    d_inner = SSD_GROUPS * GROUP_W
    conv_ch = d_inner + 2 * SSD_GROUPS * D_STATE
    heads = SSD_GROUPS * HEADS_PER_GROUP
    n_dt = 2 * heads
    perm = jnp.arange(n_dt).reshape(2, SSD_GROUPS, HEADS_PER_GROUP).transpose(1, 0, 2).reshape(-1)
    wzx = w_in[:, :d_inner + conv_ch].astype(BF16)
    wdt = jnp.zeros((d, LANES), F32).at[:, :n_dt].set(w_in[:, d_inner + conv_ch:][:, perm]).astype(BF16)
    bias = jnp.zeros((1, LANES), F32).at[0, :n_dt].set(dt_bias.reshape(-1)[perm])
    a_neg = jnp.zeros((1, LANES), F32).at[0, :n_dt].set(-jnp.exp(a_log.astype(F32)).reshape(-1)[perm])

    zx, dt_raw = _ssd_inproj(xs, mods, nw, wzx, wdt, tiles)
    dtc, acc, dtr, acr = _dt_prep(dt_raw, bias, a_neg)

    def cols(a):
        return a[:, :n_dt].reshape(batch, t_total, SSD_GROUPS, 2, HEADS_PER_GROUP).transpose(0, 2, 1, 3, 4)

    vals = jnp.concatenate([cols(acc), cols(dtc)], axis=-1)
    colp = jnp.stack(_split3(vals), axis=-2).reshape(batch, SSD_GROUPS, t_total, 2 * PIECE_LANES)
    colp = jnp.pad(colp, ((0, 0), (0, 0), (0, 0), (0, LANES - 2 * PIECE_LANES)))

    n_chunks = t_total // CHUNK
    dtr = dtr.reshape(batch, n_chunks, LANES, CHUNK)
    acr = acr.reshape(batch, n_chunks, LANES, CHUNK)

    def per_group(v):
        gx = v[:, :d_inner].reshape(-1, SSD_GROUPS, GROUP_W)
        gb = v[:, d_inner:d_inner + SSD_GROUPS * D_STATE].reshape(-1, SSD_GROUPS, D_STATE)
        gc = v[:, d_inner + SSD_GROUPS * D_STATE:].reshape(-1, SSD_GROUPS, D_STATE)
        return jnp.concatenate([gx, gb, gc], axis=-1).transpose(1, 0, 2)

    cw = per_group(jnp.concatenate([conv_w, conv_b[None], jnp.zeros((2, conv_ch), F32)], axis=0))
    dsk = jnp.repeat((d_skip[0] + d_skip[1]).astype(F32), SSD_HEAD_DIM).reshape(SSD_GROUPS, 1, GROUP_W)
    gnw = norm_w.astype(F32).reshape(SSD_GROUPS, 1, GROUP_W)

    g = _ssd_scan(zx, cw, colp, dtr, acr, dsk, gnw, batch, t_total, n_ctx)
    return g.reshape(batch * t_total, d_inner)


def _transpose_rows(blk):
    return jnp.concatenate([blk[r0:r0 + LANES].T for r0 in range(0, blk.shape[0], LANES)], axis=1)


def _qkv_rope_kernel(x_ref, mod_ref, nw_ref, w_ref, b_ref, cos_ref, sa_ref, sb_ref,
                     qt_ref, k_ref, vt_ref):
    h = _norm_mod(x_ref[...], nw_ref[...], mod_ref[0, 0:1, :], mod_ref[0, 1:2, :]).astype(BF16)
    acc = jnp.dot(h, w_ref[...], preferred_element_type=F32) + b_ref[...]
    cos = cos_ref[...]
    sa = sa_ref[...]
    sb = sb_ref[...]
    half = HEAD_DIM // 4
    q_w = Q_HEADS * HEAD_DIM
    kv_w = KV_HEADS * HEAD_DIM

    def rope(blk):
        return blk * cos + pltpu.roll(blk, half, 1) * sa + pltpu.roll(blk, LANES - half, 1) * sb

    for c0 in range(0, q_w, LANES):
        qt_ref[0, c0:c0 + LANES, :] = _transpose_rows(rope(acc[:, c0:c0 + LANES] * LOG2_E)).astype(BF16)
    for c0 in range(0, kv_w, LANES):
        k_ref[0, :, c0:c0 + LANES] = rope(acc[:, q_w + c0:q_w + c0 + LANES]).astype(BF16)
        vt_ref[0, c0:c0 + LANES, :] = _transpose_rows(
            acc[:, q_w + kv_w + c0:q_w + kv_w + c0 + LANES]).astype(BF16)


def _qkv_rope(xs, mods, nw, w, b, cos, sa, sb, tiles, batch, t_total):
    m, d = xs.shape
    n = w.shape[1]
    q_w = Q_HEADS * HEAD_DIM
    kv_w = KV_HEADS * HEAD_DIM
    tab = pl.BlockSpec((ROW_TILE, LANES), lambda i: (tiles.split(i)[1], 0))

    def feat_major(width):
        return pl.BlockSpec((1, width, ROW_TILE), lambda i: (tiles.split(i)[0], 0, tiles.split(i)[1]))

    return pl.pallas_call(
        _qkv_rope_kernel,
        out_shape=(jax.ShapeDtypeStruct((batch, q_w, t_total), BF16),
                   jax.ShapeDtypeStruct((batch, t_total, kv_w), BF16),
                   jax.ShapeDtypeStruct((batch, kv_w, t_total), BF16)),
        grid=(tiles.grid,),
        in_specs=[tiles.x_spec(d), tiles.mod_spec(d),
                  _resident((1, d), lambda i: (0, 0)),
                  _resident((d, n), lambda i: (0, 0)),
                  _resident((1, n), lambda i: (0, 0)),
                  tab, tab, tab],
        out_specs=(feat_major(q_w),
                   pl.BlockSpec((1, ROW_TILE, kv_w), lambda i: (tiles.split(i)[0], tiles.split(i)[1], 0)),
                   feat_major(kv_w)),
        compiler_params=_cparams(1, 40),
        name="attn_qkv_rope",
    )(xs, mods, nw, w, b, cos, sa, sb)


def _attention_kernel(qt_ref, kp_ref, kc_ref, kn_ref, kx_ref, vp_ref, vc_ref, vn_ref, vx_ref, sink_ref,
                      o_ref, *, u0, n_chunks, ctx_chunks, n_ctx):
    u = pl.program_id(1) + u0
    jj = lax.broadcasted_iota(jnp.int32, (BLOCK, BLOCK), 0)
    ii = lax.broadcasted_iota(jnp.int32, (BLOCK, BLOCK), 1)
    latent = u >= ctx_chunks
    m_prev = jnp.logical_and(jj >= ii, jnp.logical_and(latent, u - 1 >= ctx_chunks))
    m_cur = jnp.logical_and(ii >= 0, latent)
    m_next = jnp.logical_and(ii >= jj, jnp.logical_and(latent, u + 1 <= n_chunks - 1))
    bias = jnp.concatenate([jnp.where(mk, 0.0, -jnp.inf).astype(F32) for mk in (m_prev, m_cur, m_next)]
                           + [jnp.zeros((n_ctx, BLOCK), F32)], axis=0)
    k_all = jnp.concatenate([kp_ref[0], kc_ref[0], kn_ref[0], kx_ref[0]], axis=0)
    vt_all = jnp.concatenate([vp_ref[0], vc_ref[0], vn_ref[0], vx_ref[0]], axis=1)
    per_kv = Q_HEADS // KV_HEADS
    group_w = per_kv * HEAD_DIM
    kv_w = KV_HEADS * HEAD_DIM
    def scores(g):
        q_heads = jnp.concatenate([qt_ref[0, g * group_w + r * HEAD_DIM:g * group_w + (r + 1) * HEAD_DIM, :]
                                   for r in range(per_kv)], axis=1)
        pieces = []
        if g > 0:
            pieces.append(jnp.zeros((g * HEAD_DIM, per_kv * BLOCK), BF16))
        pieces.append(q_heads)
        if g < KV_HEADS - 1:
            pieces.append(jnp.zeros((kv_w - (g + 1) * HEAD_DIM, per_kv * BLOCK), BF16))
        return jnp.dot(k_all, jnp.concatenate(pieces, axis=0), preferred_element_type=F32)

    s_next = scores(0)
    for g in range(KV_HEADS):
        s = s_next
        if g + 1 < KV_HEADS:
            s_next = scores(g + 1)
        s = jnp.concatenate([s[:, r * BLOCK:(r + 1) * BLOCK] + bias for r in range(per_kv)], axis=1)
        sink = sink_ref[g]
        mx = jnp.maximum(jnp.max(s, axis=0, keepdims=True), sink)
        p = jnp.exp2(s - mx)
        den = jnp.sum(p, axis=0, keepdims=True) + jnp.exp2(sink - mx)
        o_t = jnp.dot(vt_all[g * HEAD_DIM:(g + 1) * HEAD_DIM, :], p.astype(BF16),
                      preferred_element_type=F32) / den
        for pair in range(per_kv // 2):
            two = jnp.concatenate([o_t[:, (2 * pair) * BLOCK:(2 * pair + 1) * BLOCK],
                                   o_t[:, (2 * pair + 1) * BLOCK:(2 * pair + 2) * BLOCK]], axis=0)
            c0 = g * group_w + pair * 2 * HEAD_DIM
            o_ref[0, :, c0:c0 + 2 * HEAD_DIM] = two.T.astype(BF16)


def _attention(qt, k, vt, sink, batch, t_total, n_ctx, need_ctx):
    n_chunks = t_total // BLOCK
    ctx_chunks = n_ctx // BLOCK
    u0 = 0 if need_ctx else ctx_chunks
    q_w = Q_HEADS * HEAD_DIM
    kv_w = KV_HEADS * HEAD_DIM
    per_kv = Q_HEADS // KV_HEADS
    assert 2 * HEAD_DIM == LANES and per_kv % 2 == 0
    sink_rows = jnp.repeat((sink.astype(F32) * LOG2_E).reshape(KV_HEADS, 1, per_kv), BLOCK, axis=2)

    def clipped(n, delta):
        return jnp.clip(n + u0 + delta, ctx_chunks, n_chunks - 1)

    def kwin(delta):
        return pl.BlockSpec((1, BLOCK, kv_w), lambda b, n: (b, clipped(n, delta), 0))

    def vwin(delta):
        return pl.BlockSpec((1, kv_w, BLOCK), lambda b, n: (b, 0, clipped(n, delta)))

    return pl.pallas_call(
        functools.partial(_attention_kernel, u0=u0, n_chunks=n_chunks, ctx_chunks=ctx_chunks, n_ctx=n_ctx),
        out_shape=jax.ShapeDtypeStruct((batch, t_total - u0 * BLOCK, q_w), BF16),
        grid=(batch, n_chunks - u0),
        in_specs=[pl.BlockSpec((1, q_w, BLOCK), lambda b, n: (b, 0, n + u0)),
                  kwin(-1), kwin(0), kwin(1),
                  pl.BlockSpec((1, n_ctx, kv_w), lambda b, n: (b, 0, 0)),
                  vwin(-1), vwin(0), vwin(1),
                  pl.BlockSpec((1, kv_w, n_ctx), lambda b, n: (b, 0, 0)),
                  pl.BlockSpec((KV_HEADS, 1, per_kv * BLOCK), lambda b, n: (0, 0, 0))],
        out_specs=pl.BlockSpec((1, BLOCK, q_w), lambda b, n: (b, n, 0)),
        compiler_params=_cparams(2, 40),
        name="attn_core",
    )(qt, k, k, k, k, vt, vt, vt, vt, sink_rows)


def _rope_tables(seq, n_ctx):
    axis_dim = HEAD_DIM // 2
    freqs = axis_dim // 2
    rows = seq // GRID_W
    row_ids = jnp.repeat(jnp.arange(rows), GRID_W).astype(F32)
    col_ids = jnp.tile(jnp.arange(GRID_W), rows).astype(F32)
    inv_freq = ROPE_BASE ** (-jnp.arange(freqs, dtype=F32) * 2.0 / axis_dim)
    ang_r = row_ids[:, None] * inv_freq
    ang_c = col_ids[:, None] * inv_freq
    ang = jnp.concatenate([ang_r, ang_r, ang_c, ang_c], axis=1)
    cos = jnp.cos(ang)
    sin = jnp.sin(ang)
    lane = jnp.arange(HEAD_DIM)
    second_half = (lane % axis_dim) >= freqs
    sa = jnp.where(second_half, sin, 0.0)
    sb = jnp.where(second_half, 0.0, -sin)

    def full(t, ctx_val):
        t = jnp.concatenate([jnp.full((n_ctx, HEAD_DIM), ctx_val, F32), t], axis=0)
        return jnp.tile(t, (1, LANES // HEAD_DIM))

    return full(cos, 1.0), full(sa, 0.0), full(sb, 0.0)


def _swiglu_rows(h, w1_ref, w3_ref, w2_ref, lead, f_chunk, after_first_chunk=None):
    f_total = w1_ref.shape[-1]
    out = None
    for f0 in range(0, f_total, f_chunk):
        sl = lead + (slice(None), slice(f0, f0 + f_chunk))
        a = jnp.dot(h, w1_ref[sl], preferred_element_type=F32)
        b = jnp.dot(h, w3_ref[sl], preferred_element_type=F32)
        u = (_silu(a) * b).astype(BF16)
        part = jnp.dot(u, w2_ref[lead + (slice(f0, f0 + f_chunk), slice(None))], preferred_element_type=F32)
        out = part if out is None else out + part
        if f0 == 0 and after_first_chunk is not None:
            after_first_chunk()
    return out


def _mixer_residual(x_ref, a_ref, wp_ref, bp_ref, mod_ref):
    y = jnp.dot(a_ref[...], wp_ref[...], preferred_element_type=F32) + bp_ref[...]
    return x_ref[...] + mod_ref[0, 2:3, :] * y


def _ffn_dense_kernel(x_ref, a_ref, wp_ref, bp_ref, mod_ref, nw_ref, w1_ref, w3_ref, w2_ref, o_ref, *, f_chunk):
    x = _mixer_residual(x_ref, a_ref, wp_ref, bp_ref, mod_ref)
    h = _norm_mod(x, nw_ref[...], mod_ref[0, 3:4, :], mod_ref[0, 4:5, :]).astype(BF16)
    f = _swiglu_rows(h, w1_ref, w3_ref, w2_ref, (), f_chunk)
    o_ref[...] = x + mod_ref[0, 5:6, :] * f


def _ffn_dense(xs, a, w_proj, b_proj, mods, nw, w1, w3, w2, tiles):
    m, d = xs.shape
    k = a.shape[1]
    f = w1.shape[1]
    mxu_n = 2 * LANES
    f_chunk = f // 2 if (f // 2) % mxu_n == 0 else f
    assert a.shape[0] == tiles.grid * ROW_TILE
    return pl.pallas_call(
        functools.partial(_ffn_dense_kernel, f_chunk=f_chunk),
        out_shape=jax.ShapeDtypeStruct((m, d), F32),
        grid=(tiles.grid,),
        in_specs=[tiles.x_spec(d), pl.BlockSpec((ROW_TILE, k), lambda i: (i, 0)),
                  _resident((k, d), lambda i: (0, 0)),
                  _resident((1, d), lambda i: (0, 0)),
                  tiles.mod_spec(d),
                  _resident((1, d), lambda i: (0, 0)),
                  _resident((d, f), lambda i: (0, 0)),
                  _resident((d, f), lambda i: (0, 0)),
                  _resident((f, d), lambda i: (0, 0))],
        out_specs=tiles.x_spec(d),
        input_output_aliases={0: 0},
        compiler_params=_cparams(1, 52),
        name="ffn_dense",
    )(xs, a, w_proj, b_proj, mods, nw, w1, w3, w2)


ROW_BLOCKS = 8


def _rows_to_tiles(ref, lead, val):
    rows = val.shape[0]
    for s in range(ROW_BLOCKS):
        ref[lead + (pl.ds(s, rows, stride=ROW_BLOCKS), slice(None))] = val[:, s * LANES:(s + 1) * LANES]


def _tiles_to_rows(ref, lead, rows):
    return jnp.concatenate([ref[lead + (pl.ds(s, rows, stride=ROW_BLOCKS), slice(None))]
                            for s in range(ROW_BLOCKS)], axis=1)


def _moe_router_kernel(x_ref, a_ref, wp_ref, bp_ref, mod_ref, nw_ref, rt_ref, xo_ref, h_ref, idx_ref, gate_ref):
    x = _mixer_residual(x_ref, a_ref, wp_ref, bp_ref, mod_ref)
    xo_ref[...] = x
    h = _norm_mod(x, nw_ref[...], mod_ref[0, 3:4, :], mod_ref[0, 4:5, :])
    _rows_to_tiles(h_ref, (), h)
    logits = lax.dot_general(rt_ref[...], h, (((1,), (1,)), ((), ())),
                             preferred_element_type=F32, precision=HIGHEST)
    e = lax.broadcasted_iota(jnp.int32, logits.shape, 0)
    m1 = jnp.max(logits, axis=0, keepdims=True)
    i1 = jnp.min(jnp.where(logits == m1, e, N_EXPERTS), axis=0, keepdims=True)
    rest = jnp.where(e == i1, -jnp.inf, logits)
    m2 = jnp.max(rest, axis=0, keepdims=True)
    i2 = jnp.min(jnp.where(rest == m2, e, N_EXPERTS), axis=0, keepdims=True)
    t = jnp.exp(m2 - m1)
    idx_ref[...] = jnp.concatenate([i1, i2], axis=0)
    gate_ref[...] = jnp.concatenate([1.0 / (1.0 + t), t / (1.0 + t)], axis=0)


def _moe_router(xs, a, w_proj, b_proj, mods, nw, router_t, tiles):
    m_all, d = xs.shape
    k = a.shape[1]
    m = tiles.grid * ROW_TILE
    assert a.shape[0] == m
    pair = pl.BlockSpec((TOP_K, ROW_TILE), lambda i: (0, i))
    return pl.pallas_call(
        _moe_router_kernel,
        out_shape=(jax.ShapeDtypeStruct((m_all, d), F32),
                   jax.ShapeDtypeStruct((m * ROW_BLOCKS, LANES), F32),
                   jax.ShapeDtypeStruct((TOP_K, m), jnp.int32),
                   jax.ShapeDtypeStruct((TOP_K, m), F32)),
        grid=(tiles.grid,),
        in_specs=[tiles.x_spec(d), pl.BlockSpec((ROW_TILE, k), lambda i: (i, 0)),
                  _resident((k, d), lambda i: (0, 0)),
                  _resident((1, d), lambda i: (0, 0)),
                  tiles.mod_spec(d),
                  _resident((1, d), lambda i: (0, 0)),
                  _resident((N_EXPERTS, d), lambda i: (0, 0))],
        out_specs=(tiles.x_spec(d),
                   pl.BlockSpec((ROW_TILE * ROW_BLOCKS, LANES), lambda i: (i, 0)), pair, pair),
        input_output_aliases={0: 0},
        compiler_params=_cparams(1, 40),
        name="moe_router",
    )(xs, a, w_proj, b_proj, mods, nw, router_t)


def _moe_experts_kernel(te_ref, src_ref, src_next_ref, dst_init_ref, dst_prev_ref, dst_ref, h_hbm, gate_ref,
                        w1_ref, w3_ref, w2_ref, y_hbm, gbuf, obuf, gsem, osem, *, f_chunk):
    j = pl.program_id(0)
    n = pl.num_programs(0)
    slot = j % 2
    other = 1 - slot
    tile = EXPERT_TILE

    def tile_rows(idx):
        return pl.ds(pl.multiple_of(idx, ROW_BLOCKS), ROW_BLOCKS)

    def gather(ref, i, s):
        return pltpu.make_async_copy(h_hbm.at[tile_rows(ref[0, 0, i])],
                                     gbuf.at[s, pl.ds(i * ROW_BLOCKS, ROW_BLOCKS)], gsem.at[s])

    def scatter(ref, i, s):
        return pltpu.make_async_copy(obuf.at[s, pl.ds(i * ROW_BLOCKS, ROW_BLOCKS)],
                                     y_hbm.at[tile_rows(ref[0, 0, i])], osem.at[s])

    def gather_wait(s):
        pltpu.make_async_copy(h_hbm.at[pl.ds(0, tile * ROW_BLOCKS)], gbuf.at[s], gsem.at[s]).wait()

    def scatter_wait(s):
        pltpu.make_async_copy(obuf.at[s], y_hbm.at[pl.ds(0, tile * ROW_BLOCKS)], osem.at[s]).wait()

    @pl.when(j == 0)
    def _():
        obuf[...] = jnp.zeros_like(obuf)

        def body(i, carry):
            gather(src_ref, i, 0).start()
            scatter(dst_init_ref, i, 0).start()
            return carry
        lax.fori_loop(0, tile, body, 0)

    gather_wait(slot)
    x = _tiles_to_rows(gbuf, (slot,), tile).astype(BF16)
    for i in range(tile):
        gather(src_next_ref, i, other).start()
    for i in range(tile):
        scatter(dst_prev_ref, i, other).start()
    f = _swiglu_rows(x, w1_ref, w3_ref, w2_ref, (0,), f_chunk)
    scatter_wait(slot)
    _rows_to_tiles(obuf, (slot,), gate_ref[...] * f)

    @pl.when(j == n - 1)
    def _():
        def body(i, carry):
            scatter(dst_ref, i, slot).start()
            return carry
        lax.fori_loop(0, tile, body, 0)
        gather_wait(other)
        scatter_wait(other)
        scatter_wait(slot)


def _moe_experts(h, tile_expert, src_rows, dst_rows, slot_gate, w1, w3, w2, n_out_rows):
    d = ROW_BLOCKS * LANES
    assert w1.shape[1] == d
    n_tiles = tile_expert.shape[0]
    f = w1.shape[2]
    tile = EXPERT_TILE
    src3 = src_rows.reshape(n_tiles, 1, tile)
    dst3 = dst_rows.reshape(n_tiles + 2, 1, tile)

    def smem(index_map):
        return pl.BlockSpec((1, 1, tile), index_map, memory_space=pltpu.SMEM)

    grid_spec = pltpu.PrefetchScalarGridSpec(
        num_scalar_prefetch=1,
        grid=(n_tiles,),
        in_specs=[
            smem(lambda j, te: (j, 0, 0)),
            smem(lambda j, te: (jnp.minimum(j + 1, n_tiles - 1), 0, 0)),
            smem(lambda j, te: (0, 0, 0)),
            smem(lambda j, te: (j + 1, 0, 0)),
            smem(lambda j, te: (j + 2, 0, 0)),
            pl.BlockSpec(memory_space=pl.ANY),
            pl.BlockSpec((tile, 1), lambda j, te: (j, 0)),
            _resident((1, d, f), lambda j, te: (te[j], 0, 0)),
            _resident((1, d, f), lambda j, te: (te[j], 0, 0)),
            _resident((1, f, d), lambda j, te: (te[j], 0, 0)),
        ],
        out_specs=pl.BlockSpec(memory_space=pl.ANY),
        scratch_shapes=[pltpu.VMEM((2, tile * ROW_BLOCKS, LANES), F32),
                        pltpu.VMEM((2, tile * ROW_BLOCKS, LANES), F32),
                        pltpu.SemaphoreType.DMA((2,)), pltpu.SemaphoreType.DMA((2,))],
    )
    return pl.pallas_call(
        functools.partial(_moe_experts_kernel, f_chunk=f // 2),
        out_shape=jax.ShapeDtypeStruct((n_out_rows * ROW_BLOCKS, LANES), F32),
        grid_spec=grid_spec,
        compiler_params=_cparams(1, 56),
        name="moe_experts",
    )(tile_expert, src3, src3, dst3, dst3, dst3, h, slot_gate, w1, w3, w2)


def _moe_combine_kernel(x_ref, mod_ref, y0_ref, y1_ref, fw_ref, o_ref, *, final):
    y = _tiles_to_rows(y0_ref, (), ROW_TILE) + _tiles_to_rows(y1_ref, (), ROW_TILE)
    x = x_ref[...] + mod_ref[0, 5:6, :] * y
    if final:
        x = x * lax.rsqrt(jnp.mean(x * x, axis=-1, keepdims=True) + EPS) * fw_ref[...]
    o_ref[...] = x


def _moe_combine(xs, mods, y, tiles, final_w=None):
    m, d = xs.shape
    k1 = tiles.grid
    final = final_w is not None
    if final:
        out_shape = jax.ShapeDtypeStruct((tiles.grid * ROW_TILE, d), F32)
        out_spec = pl.BlockSpec((ROW_TILE, d), lambda i: (i, 0))
    else:
        out_shape = jax.ShapeDtypeStruct((m, d), F32)
        out_spec = tiles.x_spec(d)
        final_w = jnp.ones((1, d), F32)
    return pl.pallas_call(
        functools.partial(_moe_combine_kernel, final=final),
        out_shape=out_shape,
        grid=(tiles.grid,),
        in_specs=[tiles.x_spec(d), tiles.mod_spec(d),
                  pl.BlockSpec((ROW_TILE * ROW_BLOCKS, LANES), lambda i: (i, 0)),
                  pl.BlockSpec((ROW_TILE * ROW_BLOCKS, LANES), lambda i: (k1 + i, 0)),
                  pl.BlockSpec((1, d), lambda i: (0, 0))],
        out_specs=out_spec,
        input_output_aliases={} if final else {0: 0},
        compiler_params=_cparams(1, 32),
        name="moe_combine",
    )(xs, mods, y, y, final_w)


def _moe_layer(xs, a, w_proj, b_proj, mods, nw, router, w1, w3, w2, tiles, final_w=None):
    xs, h, top_i, gates = _moe_router(xs, a, w_proj, b_proj, mods, nw, router.T.astype(F32), tiles)

    tile = EXPERT_TILE
    m = tiles.grid * ROW_TILE
    n_assign = TOP_K * m
    eid = top_i.reshape(-1)
    onehot = (eid[:, None] == jnp.arange(N_EXPERTS)[None, :]).astype(jnp.int32)
    csum = jnp.cumsum(onehot, axis=0)
    counts = csum[-1]
    rank = jnp.take_along_axis(csum, eid[:, None], axis=1)[:, 0] - 1
    padded = ((counts + tile - 1) // tile) * tile
    ends = jnp.cumsum(padded)
    starts = ends - padded
    n_slots = n_assign + N_EXPERTS * tile
    n_tiles = n_slots // tile
    pos = starts[eid] + rank
    slot_assign = jnp.zeros((n_slots,), jnp.int32).at[pos].set(jnp.arange(1, n_assign + 1, dtype=jnp.int32))
    valid = slot_assign > 0
    assign = jnp.maximum(slot_assign - 1, 0)
    src_rows = jnp.where(valid, jnp.where(assign >= m, assign - m, assign), 0)
    spare = n_assign + jnp.cumsum(jnp.logical_not(valid).astype(jnp.int32)) - 1
    dst_rows = jnp.where(valid, assign, spare)
    slot_gate = jnp.where(valid, gates.reshape(-1)[assign], 0.0)
    n_out_rows = n_slots + 2 * tile
    prime_rows = n_slots + jnp.arange(2 * tile, dtype=jnp.int32)
    tile_start = jnp.arange(n_tiles, dtype=jnp.int32) * tile
    tile_expert = jnp.sum(jnp.minimum(tile_start, ends[-1] - 1)[:, None] >= ends[None, :], axis=1)
    tile_expert = jnp.clip(tile_expert, 0, N_EXPERTS - 1).astype(jnp.int32)

    y = _moe_experts(h, tile_expert, src_rows.astype(jnp.int32) * ROW_BLOCKS,
                     jnp.concatenate([prime_rows, dst_rows.astype(jnp.int32)]) * ROW_BLOCKS,
                     slot_gate.reshape(-1, 1), w1, w3, w2, n_out_rows)
    return _moe_combine(xs, mods, y, tiles, final_w)


def kernel(x, c, ctx, c_ctx, w_mod, b_mod, norm1_w, norm2_w, ssd_w_in, ssd_conv_w, ssd_conv_b, ssd_dt_bias, ssd_a_log, ssd_d, ssd_norm_w, ssd_w_out, attn_w_qkv, attn_b_qkv, attn_sink, attn_w_o, attn_b_o, ffn_w1, ffn_w3, ffn_w2, moe_router, moe_w1, moe_w3, moe_w2, final_norm_w):
    batch, seq, d = x.shape
    n_ctx = ctx.shape[1]
    t_total = n_ctx + seq
    depth = w_mod.shape[0]
    assert seq % GRID_W == 0 and seq % BLOCK == 0 and n_ctx % ROW_TILE == 0 and depth % 2 == 0

    all_tiles = _Tiles(batch, t_total, n_ctx, 0)
    lat_tiles = _Tiles(batch, t_total, n_ctx, n_ctx // ROW_TILE)

    xs = jnp.concatenate([ctx, x], axis=1).reshape(batch * t_total, d)
    mods_all = _mod_all(c, c_ctx, w_mod, b_mod)
    cos, sa, sb = _rope_tables(seq, n_ctx)
    q_w = Q_HEADS * HEAD_DIM
    q_scale = jnp.concatenate([jnp.full((q_w,), 1.0 / math.sqrt(HEAD_DIM), F32),
                               jnp.ones((attn_w_qkv.shape[2] - q_w,), F32)])

    for i in range(depth):
        last = i == depth - 1
        j = i // 2
        mods = mods_all[i]
        nw1 = norm1_w[i].reshape(1, d)
        nw2 = norm2_w[i].reshape(1, d)
        upd = lat_tiles if last else all_tiles
        if i % 2 == 0:
            g = _ssd_mixer(xs, mods, nw1, ssd_w_in[j], ssd_conv_w[j], ssd_conv_b[j], ssd_dt_bias[j],
                           ssd_a_log[j], ssd_d[j], ssd_norm_w[j], all_tiles, batch, t_total, n_ctx)
            xs = _ffn_dense(xs, g, ssd_w_out[j].astype(BF16), jnp.zeros((1, d), F32), mods, nw2,
                            ffn_w1[j].astype(BF16), ffn_w3[j].astype(BF16), ffn_w2[j].astype(BF16), all_tiles)
        else:
            qt, k, vt = _qkv_rope(xs, mods, nw1, (attn_w_qkv[j] * q_scale).astype(BF16),
                                  (attn_b_qkv[j] * q_scale).reshape(1, -1), cos, sa, sb, all_tiles,
                                  batch, t_total)
            o = _attention(qt, k, vt, attn_sink[j], batch, t_total, n_ctx, not last)
            xs = _moe_layer(xs, o.reshape(-1, q_w), attn_w_o[j].astype(BF16), attn_b_o[j].reshape(1, d),
                            mods, nw2, moe_router[j], moe_w1[j].astype(BF16), moe_w3[j].astype(BF16),
                            moe_w2[j].astype(BF16), upd,
                            final_w=final_norm_w.reshape(1, d) if last else None)
    return xs.reshape(batch, seq, d)
```

```python
import functools
import math

import jax
import jax.numpy as jnp
from jax import lax
from jax.experimental import pallas as pl
from jax.experimental.pallas import tpu as pltpu

F32 = jnp.float32
BF16 = jnp.bfloat16
HIGHEST = lax.Precision.HIGHEST

EPS = 1e-6
LOG2_E = math.log2(math.e)
N_MOD = 6
GRID_W = 64
ROPE_BASE = 10000.0

SSD_HEAD_DIM = 64
SSD_GROUPS = 8
D_STATE = 128
CONV_K = 5
CHUNK = 128
HEADS_PER_GROUP = 4
GROUP_W = HEADS_PER_GROUP * SSD_HEAD_DIM

Q_HEADS = 16
KV_HEADS = 4
HEAD_DIM = 64
BLOCK = 128

N_EXPERTS = 8
TOP_K = 2

ROW_TILE = 256
EXPERT_TILE = 512
LANES = 128
PACK_ROWS_BF16 = 16

MIB = 1024 * 1024


def _cparams(n_axes, vmem_mib):
    return pltpu.CompilerParams(dimension_semantics=("arbitrary",) * n_axes,
                                vmem_limit_bytes=vmem_mib * MIB)


def _resident(block_shape, index_map):
    return pl.BlockSpec(block_shape, index_map, pipeline_mode=pl.Buffered(1))


def _sigmoid(v):
    return 1.0 / (1.0 + jnp.exp(-v))


def _silu(v):
    return v * _sigmoid(v)


def _norm_mod(x, nw, shift, scale):
    ms = jnp.mean(x * x, axis=-1, keepdims=True)
    y = x * lax.rsqrt(ms + EPS) * nw
    return y * (1.0 + scale) + shift


class _Tiles:
    def __init__(self, batch, t_total, n_ctx, lo_tiles):
        assert t_total % ROW_TILE == 0 and n_ctx % ROW_TILE == 0
        self.batch = batch
        self.tpb = t_total // ROW_TILE
        self.ctx_tiles = n_ctx // ROW_TILE
        self.lo = lo_tiles
        self.n_w = self.tpb - lo_tiles
        self.grid = batch * self.n_w

    def split(self, i):
        return i // self.n_w, self.lo + i % self.n_w

    def row(self, i):
        b, w = self.split(i)
        return b * self.tpb + w

    def mod_row(self, i):
        b, w = self.split(i)
        return jnp.where(w < self.ctx_tiles, self.batch, b)

    def x_spec(self, width):
        return pl.BlockSpec((ROW_TILE, width), lambda i: (self.row(i), 0))

    def mod_spec(self, d):
        return pl.BlockSpec((1, N_MOD, d), lambda i: (self.mod_row(i), 0, 0))


def _mod_kernel(c_ref, w_ref, b_ref, o_ref):
    s = _silu(c_ref[...])
    o_ref[0] = jnp.dot(s, w_ref[0], preferred_element_type=F32, precision=HIGHEST) + b_ref[0]


def _mod_all(c, c_ctx, w_mod, b_mod):
    depth, d, n = w_mod.shape
    cc = jnp.concatenate([c, c_ctx[None]], axis=0)
    rows = cc.shape[0]
    tn = 1536
    assert n % tn == 0
    out = pl.pallas_call(
        _mod_kernel,
        out_shape=jax.ShapeDtypeStruct((depth, rows, n), F32),
        grid=(depth, n // tn),
        in_specs=[pl.BlockSpec((rows, d), lambda i, j: (0, 0)),
                  pl.BlockSpec((1, d, tn), lambda i, j: (i, 0, j)),
                  pl.BlockSpec((1, 1, tn), lambda i, j: (i, 0, j))],
        out_specs=pl.BlockSpec((1, rows, tn), lambda i, j: (i, 0, j)),
        compiler_params=_cparams(2, 40),
        name="mod_all",
    )(cc, w_mod, b_mod.reshape(depth, 1, n))
    return out.reshape(depth, rows, N_MOD, d)


def _ssd_inproj_kernel(x_ref, mod_ref, nw_ref, wzx_ref, wdt_ref, zx_ref, dt_ref, *, n_chunk):
    h = _norm_mod(x_ref[...], nw_ref[...], mod_ref[0, 0:1, :], mod_ref[0, 1:2, :]).astype(BF16)
    n = wzx_ref.shape[1]
    for n0 in range(0, n, n_chunk):
        zx_ref[:, n0:n0 + n_chunk] = jnp.dot(
            h, wzx_ref[:, n0:n0 + n_chunk], preferred_element_type=F32).astype(BF16)
    dt_ref[...] = jnp.dot(h, wdt_ref[...], preferred_element_type=F32)


def _ssd_inproj(xs, mods, nw, wzx, wdt, tiles):
    m, d = xs.shape
    n = wzx.shape[1]
    return pl.pallas_call(
        functools.partial(_ssd_inproj_kernel, n_chunk=1536),
        out_shape=(jax.ShapeDtypeStruct((m, n), BF16), jax.ShapeDtypeStruct((m, LANES), F32)),
        grid=(tiles.grid,),
        in_specs=[tiles.x_spec(d), tiles.mod_spec(d),
                  _resident((1, d), lambda i: (0, 0)),
                  _resident((d, n), lambda i: (0, 0)),
                  _resident((d, LANES), lambda i: (0, 0))],
        out_specs=(tiles.x_spec(n), tiles.x_spec(LANES)),
        compiler_params=_cparams(1, 48),
        name="ssd_inproj",
    )(xs, mods, nw, wzx, wdt)


def _dt_prep_kernel(raw_ref, bias_ref, a_ref, dtc_ref, acc_ref, dtr_ref, acr_ref):
    ii = lax.broadcasted_iota(jnp.int32, (CHUNK, CHUNK), 0)
    jj = lax.broadcasted_iota(jnp.int32, (CHUNK, CHUNK), 1)
    lower = (ii >= jj).astype(F32)
    upper = (ii <= jj).astype(F32)
    col = lax.broadcasted_iota(jnp.int32, (CHUNK, LANES), 1)
    is_fwd = (col % (2 * HEADS_PER_GROUP)) < HEADS_PER_GROUP
    for c in range(raw_ref.shape[0] // CHUNK):
        rows = pl.ds(c * CHUNK, CHUNK)
        v = raw_ref[rows, :] + bias_ref[...]
        dt = jnp.maximum(v, 0.0) + jnp.log(1.0 + jnp.exp(-jnp.abs(v)))
        a = dt * a_ref[...]
        prefix = jnp.dot(lower, a, preferred_element_type=F32, precision=HIGHEST)
        suffix = jnp.dot(upper, a, preferred_element_type=F32, precision=HIGHEST)
        ac = jnp.where(is_fwd, prefix, suffix)
        dtc_ref[rows, :] = dt
        acc_ref[rows, :] = ac
        dtr_ref[c] = dt.T
        acr_ref[c] = ac.T


DT_PREP_CHUNKS = 4


def _dt_prep(dt_raw, bias, a_neg):
    m = dt_raw.shape[0]
    nchunks = m // CHUNK
    per = DT_PREP_CHUNKS
    assert nchunks % per == 0
    tile = pl.BlockSpec((per * CHUNK, LANES), lambda i: (i, 0))
    vec = pl.BlockSpec((1, LANES), lambda i: (0, 0))
    sq = pl.BlockSpec((per, LANES, CHUNK), lambda i: (i, 0, 0))
    return pl.pallas_call(
        _dt_prep_kernel,
        out_shape=(jax.ShapeDtypeStruct((m, LANES), F32), jax.ShapeDtypeStruct((m, LANES), F32),
                   jax.ShapeDtypeStruct((nchunks, LANES, CHUNK), F32),
                   jax.ShapeDtypeStruct((nchunks, LANES, CHUNK), F32)),
        grid=(nchunks // per,),
        in_specs=[tile, vec, vec],
        out_specs=(tile, tile, sq, sq),
        compiler_params=_cparams(1, 32),
        name="ssd_dt_prep",
    )(dt_raw, bias, a_neg)


def _split3(v):
    def top_bits(a):
        return lax.bitcast_convert_type(
            lax.bitcast_convert_type(a, jnp.uint32) & jnp.uint32(0xFFFF0000), F32)

    hi = top_bits(v)
    r1 = v - hi
    mid = top_bits(r1)
    lo = r1 - mid
    return hi.astype(BF16), mid.astype(BF16), lo.astype(BF16)


N_BCAST = 4 * CHUNK + 2 * GROUP_W
PIECE_LANES = 3 * 2 * HEADS_PER_GROUP


def _bcast_selector():
    lane = jnp.arange(LANES)[:, None]
    col = jnp.arange(N_BCAST)[None, :]
    n_q = 2 * HEADS_PER_GROUP
    sel = []
    for d in range(2):
        rel = lane - PIECE_LANES * d
        q = rel % n_q
        tile = (col < 4 * CHUNK) & (col // CHUNK == q)
        acc_exp = (col >= 4 * CHUNK) & (col < 4 * CHUNK + GROUP_W) & ((col - 4 * CHUNK) // SSD_HEAD_DIM == q)
        dt_exp = (col >= 4 * CHUNK + GROUP_W) & (
            (col - 4 * CHUNK - GROUP_W) // SSD_HEAD_DIM + HEADS_PER_GROUP == q)
        in_dir = (rel >= 0) & (rel < PIECE_LANES)
        sel.append(((tile | acc_exp | dt_exp) & in_dir).astype(BF16))
    return jnp.stack(sel)


CONV_HALO = 64
CONV_WIN = CHUNK + 2 * CONV_HALO


def _shift_selector():
    pad = CONV_K // 2
    taps = jnp.asarray([k for k in range(CONV_K) if k != pad])
    row = jnp.arange((CONV_K - 1) * CHUNK)
    src = CONV_HALO + row % CHUNK + taps[row // CHUNK] - pad
    return (src[:, None] == jnp.arange(CONV_WIN)[None, :]).astype(BF16)


def _ssd_scan_kernel(z_ref, x_ref, b_ref, c_ref, cw_ref, colp_ref, dtr_ref, acr_ref, e_ref, shift_ref,
                     dsk_ref, nw_ref, o_ref, xc_s, cc_s, bt_s, yf_s, yb_s, sf_s, sb_s, *, n_chunks, ctx_chunks):
    t_total = n_chunks * CHUNK
    halo = CONV_HALO
    pad = CONV_K // 2
    taps = [k for k in range(CONV_K) if k != pad]

    def window(ref, c, t0):
        cur = ref[0, pl.ds(t0, CHUNK), :]
        prev = ref[0, pl.ds(pl.multiple_of(jnp.maximum(t0 - halo, 0), halo), halo), :]
        nxt = ref[0, pl.ds(pl.multiple_of(jnp.minimum(t0 + CHUNK, t_total - halo), halo), halo), :]
        prev_ok = jnp.logical_and(c != 0, c != ctx_chunks)
        next_ok = jnp.logical_and(c != ctx_chunks - 1, c != n_chunks - 1)
        prev = jnp.where(prev_ok, prev, jnp.zeros_like(prev))
        nxt = jnp.where(next_ok, nxt, jnp.zeros_like(nxt))
        return jnp.concatenate([prev, cur, nxt], axis=0)

    def conv_shift(c):
        t0 = pl.multiple_of(c * CHUNK, CHUNK)
        w = jnp.concatenate([window(x_ref, c, t0), window(b_ref, c, t0), window(c_ref, c, t0)], axis=1)
        return c, t0, w, jnp.dot(shift_ref[...], w, preferred_element_type=F32)

    def conv_finish(c, t0, w, shifted):
        acc = cw_ref[0, CONV_K:CONV_K + 1, :] + w[halo:halo + CHUNK].astype(F32) * cw_ref[0, pad:pad + 1, :]
        for n, k in enumerate(taps):
            acc = acc + shifted[n * CHUNK:(n + 1) * CHUNK] * cw_ref[0, k:k + 1, :]
        v = _silu(acc)
        xc_s[pl.ds(t0, CHUNK), :] = v[:, :GROUP_W].astype(BF16)
        bt_s[c] = v[:, GROUP_W:GROUP_W + D_STATE].T.astype(BF16)
        cc_s[pl.ds(t0, CHUNK), :] = v[:, GROUP_W + D_STATE:].astype(BF16)

    def conv_body(p, carry):
        jobs = [conv_shift(2 * p), conv_shift(2 * p + 1)]
        for job in jobs:
            conv_finish(*job)
        return carry

    lax.fori_loop(0, n_chunks // 2, conv_body, 0)

    ii = lax.broadcasted_iota(jnp.int32, (CHUNK, CHUNK), 0)
    jj = lax.broadcasted_iota(jnp.int32, (CHUNK, CHUNK), 1)
    lane_w = lax.broadcasted_iota(jnp.int32, (CHUNK, GROUP_W), 1)
    y_s = (yf_s, yb_s)
    st_s = (sf_s, sb_s)
    sf_s[...] = jnp.zeros_like(sf_s)
    sb_s[...] = jnp.zeros_like(sb_s)

    def chunk_prep(d, c, bcast):
        t0 = pl.multiple_of(c * CHUNK, CHUNK)
        cch = cc_s[pl.ds(t0, CHUNK), :]
        bt = bt_s[c]
        return dict(d=d, c=c, t0=t0, cch=cch, bt=bt,
                    cb=jnp.dot(cch, bt, preferred_element_type=F32),
                    acc_exp=bcast(4 * CHUNK, GROUP_W),
                    dt_exp=bcast(4 * CHUNK + GROUP_W, GROUP_W),
                    acc_pairs=[bcast(r * CHUNK, 2 * CHUNK) for r in range(0, HEADS_PER_GROUP, 2)])

    def chunk_finish(job):
        d, c, t0, cch, bt, cb, acc_exp, dt_exp = (job[k] for k in
                                                  ("d", "c", "t0", "cch", "bt", "cb", "acc_exp", "dt_exp"))
        mask = (ii >= jj) if d == 0 else (ii <= jj)
        last = CHUNK - 1 if d == 0 else 0
        xch = xc_s[pl.ds(t0, CHUNK), :]
        dtr = dtr_ref[0, c]
        acr = acr_ref[0, c]
        ms = []
        xs_ = []
        for r in range(HEADS_PER_GROUP):
            k = 4 * d + r
            seg = job["acc_pairs"][r // 2][:, (r % 2) * CHUNK:(r % 2 + 1) * CHUNK] - acr[k:k + 1, :]
            decay = jnp.exp(jnp.where(mask, seg, -jnp.inf))
            ms.append((cb * decay * dtr[k:k + 1, :]).astype(BF16))
            in_head = jnp.logical_and(lane_w >= r * SSD_HEAD_DIM, lane_w < (r + 1) * SSD_HEAD_DIM)
            xs_.append(jnp.where(in_head, xch, jnp.zeros_like(xch)))
        y = jnp.dot(jnp.concatenate(ms, axis=1), jnp.concatenate(xs_, axis=0), preferred_element_type=F32)
        state = st_s[d][...]
        y = y + jnp.dot(cch, state.astype(BF16), preferred_element_type=F32) * jnp.exp(acc_exp)
        y_s[d][pl.ds(t0, CHUNK), :] = y

        a_last = acc_exp[last:last + 1, :]
        xw = (xch.astype(F32) * (jnp.exp(a_last - acc_exp) * dt_exp)).astype(BF16)
        st_s[d][...] = state * jnp.exp(a_last) + jnp.dot(bt, xw, preferred_element_type=F32)

    def pair_prep(d, c_lo, first_half):
        pieces = colp_ref[0, 0, pl.ds(pl.multiple_of(c_lo * CHUNK, CHUNK), 2 * CHUNK), :]
        products = {}

        def bcast(half, c0, width):
            if (c0, width) not in products:
                products[(c0, width)] = jnp.dot(pieces, e_ref[d, :, c0:c0 + width], preferred_element_type=F32)
            return products[(c0, width)][half * CHUNK:(half + 1) * CHUNK]

        return [chunk_prep(d, c_lo + half, functools.partial(bcast, half))
                for half in (first_half, 1 - first_half)]

    def scan_body(p, carry):
        s = 2 * p
        c_bwd = jnp.where(s < ctx_chunks, ctx_chunks - 1 - s, n_chunks - 1 - (s - ctx_chunks))
        fwd = pair_prep(0, s, 0)
        bwd = pair_prep(1, c_bwd - 1, 1)
        for job in (fwd[0], bwd[0], fwd[1], bwd[1]):
            chunk_finish(job)
        return carry

    assert n_chunks % 2 == 0 and ctx_chunks % 2 == 0
    lax.fori_loop(0, n_chunks // 2, scan_body, 0)

    def out_body(c, carry):
        t0 = pl.multiple_of(c * CHUNK, CHUNK)
        y = (yf_s[pl.ds(t0, CHUNK), :] + yb_s[pl.ds(t0, CHUNK), :]
             + dsk_ref[0] * xc_s[pl.ds(t0, CHUNK), :].astype(F32))
        g = y * _silu(z_ref[0, pl.ds(t0, CHUNK), :].astype(F32))
        g = g * lax.rsqrt(jnp.mean(g * g, axis=-1, keepdims=True) + EPS)
        o_ref[0, pl.ds(t0, CHUNK), :] = (g * nw_ref[0]).astype(BF16)
        return carry

    lax.fori_loop(0, n_chunks, out_body, 0, unroll=2)


def _ssd_scan(zx, cw, colp, dtr, acr, dskip, norm_w, batch, t_total, n_ctx):
    d_inner = SSD_GROUPS * GROUP_W
    n_chunks = t_total // CHUNK
    zx3 = zx.reshape(batch, t_total, zx.shape[-1])
    xoff = d_inner // GROUP_W
    boff = 2 * d_inner // D_STATE
    coff = boff + SSD_GROUPS
    pc = CONV_K + 3
    rows =pl.BlockSpec((1, n_chunks, 2 * HEADS_PER_GROUP, CHUNK), lambda b, g: (b, 0, g, 0))
    vec = pl.BlockSpec((1, 1, GROUP_W), lambda b, g: (g, 0, 0))
    return pl.pallas_call(
        functools.partial(_ssd_scan_kernel, n_chunks=n_chunks, ctx_chunks=n_ctx // CHUNK),
        out_shape=jax.ShapeDtypeStruct((batch, t_total, d_inner), BF16),
        grid=(batch, SSD_GROUPS),
        in_specs=[pl.BlockSpec((1, t_total, GROUP_W), lambda b, g: (b, 0, g)),
                  pl.BlockSpec((1, t_total, GROUP_W), lambda b, g: (b, 0, xoff + g)),
                  pl.BlockSpec((1, t_total, D_STATE), lambda b, g: (b, 0, boff + g)),
                  pl.BlockSpec((1, t_total, D_STATE), lambda b, g: (b, 0, coff + g)),
                  pl.BlockSpec((1, pc, GROUP_W + 2 * D_STATE), lambda b, g: (g, 0, 0)),
                  pl.BlockSpec((1, 1, t_total, LANES), lambda b, g: (b, g, 0, 0)),
                  rows, rows,
                  _resident((2, LANES, N_BCAST), lambda b, g: (0, 0, 0)),
                  _resident(((CONV_K - 1) * CHUNK, CONV_WIN), lambda b, g: (0, 0)),
                  vec, vec],
        out_specs=pl.BlockSpec((1, t_total, GROUP_W), lambda b, g: (b, 0, g)),
        scratch_shapes=[pltpu.VMEM((t_total, GROUP_W), BF16),
                        pltpu.VMEM((t_total, D_STATE), BF16),
                        pltpu.VMEM((n_chunks, D_STATE, CHUNK), BF16),
                        pltpu.VMEM((t_total, GROUP_W), F32),
                        pltpu.VMEM((t_total, GROUP_W), F32),
                        pltpu.VMEM((D_STATE, GROUP_W), F32),
                        pltpu.VMEM((D_STATE, GROUP_W), F32)],
        compiler_params=_cparams(2, 56),
        name="ssd_scan",
    )(zx3, zx3, zx3, zx3, cw, colp, dtr, acr, _bcast_selector(), _shift_selector(), dskip, norm_w)


def _ssd_mixer(xs, mods, nw, w_in, conv_w, conv_b, dt_bias, a_log, d_skip, norm_w,
               tiles, batch, t_total, n_ctx):
    d = xs.shape[1]
    d_inner = SSD_GROUPS * GROUP_W
    conv_ch = d_inner + 2 * SSD_GROUPS * D_STATE
    heads = SSD_GROUPS * HEADS_PER_GROUP
    n_dt = 2 * heads
    perm = jnp.arange(n_dt).reshape(2, SSD_GROUPS, HEADS_PER_GROUP).transpose(1, 0, 2).reshape(-1)
    wzx = w_in[:, :d_inner + conv_ch].astype(BF16)
    wdt = jnp.zeros((d, LANES), F32).at[:, :n_dt].set(w_in[:, d_inner + conv_ch:][:, perm]).astype(BF16)
    bias = jnp.zeros((1, LANES), F32).at[0, :n_dt].set(dt_bias.reshape(-1)[perm])
    a_neg = jnp.zeros((1, LANES), F32).at[0, :n_dt].set(-jnp.exp(a_log.astype(F32)).reshape(-1)[perm])

    zx, dt_raw = _ssd_inproj(xs, mods, nw, wzx, wdt, tiles)
    dtc, acc, dtr, acr = _dt_prep(dt_raw, bias, a_neg)

    def cols(a):
        return a[:, :n_dt].reshape(batch, t_total, SSD_GROUPS, 2, HEADS_PER_GROUP).transpose(0, 2, 1, 3, 4)

    vals = jnp.concatenate([cols(acc), cols(dtc)], axis=-1)
    colp = jnp.stack(_split3(vals), axis=-2).reshape(batch, SSD_GROUPS, t_total, 2 * PIECE_LANES)
    colp = jnp.pad(colp, ((0, 0), (0, 0), (0, 0), (0, LANES - 2 * PIECE_LANES)))

    n_chunks = t_total // CHUNK
    dtr = dtr.reshape(batch, n_chunks, LANES, CHUNK)
    acr = acr.reshape(batch, n_chunks, LANES, CHUNK)

    def per_group(v):
        gx = v[:, :d_inner].reshape(-1, SSD_GROUPS, GROUP_W)
        gb = v[:, d_inner:d_inner + SSD_GROUPS * D_STATE].reshape(-1, SSD_GROUPS, D_STATE)
        gc = v[:, d_inner + SSD_GROUPS * D_STATE:].reshape(-1, SSD_GROUPS, D_STATE)
        return jnp.concatenate([gx, gb, gc], axis=-1).transpose(1, 0, 2)

    cw = per_group(jnp.concatenate([conv_w, conv_b[None], jnp.zeros((2, conv_ch), F32)], axis=0))
    dsk = jnp.repeat((d_skip[0] + d_skip[1]).astype(F32), SSD_HEAD_DIM).reshape(SSD_GROUPS, 1, GROUP_W)
    gnw = norm_w.astype(F32).reshape(SSD_GROUPS, 1, GROUP_W)

    g = _ssd_scan(zx, cw, colp, dtr, acr, dsk, gnw, batch, t_total, n_ctx)
    return g.reshape(batch * t_total, d_inner)


def _transpose_rows(blk):
    return jnp.concatenate([blk[r0:r0 + LANES].T for r0 in range(0, blk.shape[0], LANES)], axis=1)


def _qkv_rope_kernel(x_ref, mod_ref, nw_ref, w_ref, b_ref, cos_ref, sa_ref, sb_ref,
                     qt_ref, k_ref, vt_ref):
    h = _norm_mod(x_ref[...], nw_ref[...], mod_ref[0, 0:1, :], mod_ref[0, 1:2, :]).astype(BF16)
    acc = jnp.dot(h, w_ref[...], preferred_element_type=F32) + b_ref[...]
    cos = cos_ref[...]
    sa = sa_ref[...]
    sb = sb_ref[...]
    half = HEAD_DIM // 4
    q_w = Q_HEADS * HEAD_DIM
    kv_w = KV_HEADS * HEAD_DIM

    def rope(blk):
        return blk * cos + pltpu.roll(blk, half, 1) * sa + pltpu.roll(blk, LANES - half, 1) * sb

    for c0 in range(0, q_w, LANES):
        qt_ref[0, c0:c0 + LANES, :] = _transpose_rows(rope(acc[:, c0:c0 + LANES] * LOG2_E)).astype(BF16)
    for c0 in range(0, kv_w, LANES):
        k_ref[0, :, c0:c0 + LANES] = rope(acc[:, q_w + c0:q_w + c0 + LANES]).astype(BF16)
        vt_ref[0, c0:c0 + LANES, :] = _transpose_rows(
            acc[:, q_w + kv_w + c0:q_w + kv_w + c0 + LANES]).astype(BF16)


def _qkv_rope(xs, mods, nw, w, b, cos, sa, sb, tiles, batch, t_total):
    m, d = xs.shape
    n = w.shape[1]
    q_w = Q_HEADS * HEAD_DIM
    kv_w = KV_HEADS * HEAD_DIM
    tab = pl.BlockSpec((ROW_TILE, LANES), lambda i: (tiles.split(i)[1], 0))

    def feat_major(width):
        return pl.BlockSpec((1, width, ROW_TILE), lambda i: (tiles.split(i)[0], 0, tiles.split(i)[1]))

    return pl.pallas_call(
        _qkv_rope_kernel,
        out_shape=(jax.ShapeDtypeStruct((batch, q_w, t_total), BF16),
                   jax.ShapeDtypeStruct((batch, t_total, kv_w), BF16),
                   jax.ShapeDtypeStruct((batch, kv_w, t_total), BF16)),
        grid=(tiles.grid,),
        in_specs=[tiles.x_spec(d), tiles.mod_spec(d),
                  _resident((1, d), lambda i: (0, 0)),
                  _resident((d, n), lambda i: (0, 0)),
                  _resident((1, n), lambda i: (0, 0)),
                  tab, tab, tab],
        out_specs=(feat_major(q_w),
                   pl.BlockSpec((1, ROW_TILE, kv_w), lambda i: (tiles.split(i)[0], tiles.split(i)[1], 0)),
                   feat_major(kv_w)),
        compiler_params=_cparams(1, 40),
        name="attn_qkv_rope",
    )(xs, mods, nw, w, b, cos, sa, sb)


def _attention_kernel(qt_ref, kp_ref, kc_ref, kn_ref, kx_ref, vp_ref, vc_ref, vn_ref, vx_ref, sink_ref,
                      o_ref, *, u0, n_chunks, ctx_chunks, n_ctx):
    u = pl.program_id(1) + u0
    jj = lax.broadcasted_iota(jnp.int32, (BLOCK, BLOCK), 0)
    ii = lax.broadcasted_iota(jnp.int32, (BLOCK, BLOCK), 1)
    latent = u >= ctx_chunks
    m_prev = jnp.logical_and(jj >= ii, jnp.logical_and(latent, u - 1 >= ctx_chunks))
    m_cur = jnp.logical_and(ii >= 0, latent)
    m_next = jnp.logical_and(ii >= jj, jnp.logical_and(latent, u + 1 <= n_chunks - 1))
    bias = jnp.concatenate([jnp.where(mk, 0.0, -jnp.inf).astype(F32) for mk in (m_prev, m_cur, m_next)]
                           + [jnp.zeros((n_ctx, BLOCK), F32)], axis=0)
    k_all = jnp.concatenate([kp_ref[0], kc_ref[0], kn_ref[0], kx_ref[0]], axis=0)
    vt_all = jnp.concatenate([vp_ref[0], vc_ref[0], vn_ref[0], vx_ref[0]], axis=1)
    per_kv = Q_HEADS // KV_HEADS
    group_w = per_kv * HEAD_DIM
    kv_w = KV_HEADS * HEAD_DIM
    def scores(g):
        q_heads = jnp.concatenate([qt_ref[0, g * group_w + r * HEAD_DIM:g * group_w + (r + 1) * HEAD_DIM, :]
                                   for r in range(per_kv)], axis=1)
        pieces = []
        if g > 0:
            pieces.append(jnp.zeros((g * HEAD_DIM, per_kv * BLOCK), BF16))
        pieces.append(q_heads)
        if g < KV_HEADS - 1:
            pieces.append(jnp.zeros((kv_w - (g + 1) * HEAD_DIM, per_kv * BLOCK), BF16))
        return jnp.dot(k_all, jnp.concatenate(pieces, axis=0), preferred_element_type=F32)

    def weighted_values(g, p, den):
        o_t = jnp.dot(vt_all[g * HEAD_DIM:(g + 1) * HEAD_DIM, :], p, preferred_element_type=F32) / den
        for pair in range(per_kv // 2):
            two = jnp.concatenate([o_t[:, (2 * pair) * BLOCK:(2 * pair + 1) * BLOCK],
                                   o_t[:, (2 * pair + 1) * BLOCK:(2 * pair + 2) * BLOCK]], axis=0)
            c0 = g * group_w + pair * 2 * HEAD_DIM
            o_ref[0, :, c0:c0 + 2 * HEAD_DIM] = two.T.astype(BF16)

    s_next = scores(0)
    pending = None
    for g in range(KV_HEADS):
        s = s_next
        if g + 1 < KV_HEADS:
            s_next = scores(g + 1)
        s = jnp.concatenate([s[:, r * BLOCK:(r + 1) * BLOCK] + bias for r in range(per_kv)], axis=1)
        sink = sink_ref[g]
        mx = jnp.maximum(jnp.max(s, axis=0, keepdims=True), sink)
        p = jnp.exp2(s - mx)
        den = jnp.sum(p, axis=0, keepdims=True) + jnp.exp2(sink - mx)
        if pending is not None:
            weighted_values(*pending)
        pending = (g, p.astype(BF16), den)
    weighted_values(*pending)


def _attention(qt, k, vt, sink, batch, t_total, n_ctx, need_ctx):
    n_chunks = t_total // BLOCK
    ctx_chunks = n_ctx // BLOCK
    u0 = 0 if need_ctx else ctx_chunks
    q_w = Q_HEADS * HEAD_DIM
    kv_w = KV_HEADS * HEAD_DIM
    per_kv = Q_HEADS // KV_HEADS
    assert 2 * HEAD_DIM == LANES and per_kv % 2 == 0
    sink_rows = jnp.repeat((sink.astype(F32) * LOG2_E).reshape(KV_HEADS, 1, per_kv), BLOCK, axis=2)

    def clipped(n, delta):
        return jnp.clip(n + u0 + delta, ctx_chunks, n_chunks - 1)

    def kwin(delta):
        return pl.BlockSpec((1, BLOCK, kv_w), lambda b, n: (b, clipped(n, delta), 0))

    def vwin(delta):
        return pl.BlockSpec((1, kv_w, BLOCK), lambda b, n: (b, 0, clipped(n, delta)))

    return pl.pallas_call(
        functools.partial(_attention_kernel, u0=u0, n_chunks=n_chunks, ctx_chunks=ctx_chunks, n_ctx=n_ctx),
        out_shape=jax.ShapeDtypeStruct((batch, t_total - u0 * BLOCK, q_w), BF16),
        grid=(batch, n_chunks - u0),
        in_specs=[pl.BlockSpec((1, q_w, BLOCK), lambda b, n: (b, 0, n + u0)),
                  kwin(-1), kwin(0), kwin(1),
                  pl.BlockSpec((1, n_ctx, kv_w), lambda b, n: (b, 0, 0)),
                  vwin(-1), vwin(0), vwin(1),
                  pl.BlockSpec((1, kv_w, n_ctx), lambda b, n: (b, 0, 0)),
                  pl.BlockSpec((KV_HEADS, 1, per_kv * BLOCK), lambda b, n: (0, 0, 0))],
        out_specs=pl.BlockSpec((1, BLOCK, q_w), lambda b, n: (b, n, 0)),
        compiler_params=_cparams(2, 40),
        name="attn_core",
    )(qt, k, k, k, k, vt, vt, vt, vt, sink_rows)


def _rope_tables(seq, n_ctx):
    axis_dim = HEAD_DIM // 2
    freqs = axis_dim // 2
    rows = seq // GRID_W
    row_ids = jnp.repeat(jnp.arange(rows), GRID_W).astype(F32)
    col_ids = jnp.tile(jnp.arange(GRID_W), rows).astype(F32)
    inv_freq = ROPE_BASE ** (-jnp.arange(freqs, dtype=F32) * 2.0 / axis_dim)
    ang_r = row_ids[:, None] * inv_freq
    ang_c = col_ids[:, None] * inv_freq
    ang = jnp.concatenate([ang_r, ang_r, ang_c, ang_c], axis=1)
    cos = jnp.cos(ang)
    sin = jnp.sin(ang)
    lane = jnp.arange(HEAD_DIM)
    second_half = (lane % axis_dim) >= freqs
    sa = jnp.where(second_half, sin, 0.0)
    sb = jnp.where(second_half, 0.0, -sin)

    def full(t, ctx_val):
        t = jnp.concatenate([jnp.full((n_ctx, HEAD_DIM), ctx_val, F32), t], axis=0)
        return jnp.tile(t, (1, LANES // HEAD_DIM))

    return full(cos, 1.0), full(sa, 0.0), full(sb, 0.0)


def _swiglu_rows(h, w1_ref, w3_ref, w2_ref, lead, f_chunk, after_first_chunk=None):
    f_total = w1_ref.shape[-1]
    out = None
    for f0 in range(0, f_total, f_chunk):
        sl = lead + (slice(None), slice(f0, f0 + f_chunk))
        a = jnp.dot(h, w1_ref[sl], preferred_element_type=F32)
        b = jnp.dot(h, w3_ref[sl], preferred_element_type=F32)
        u = (_silu(a) * b).astype(BF16)
        part = jnp.dot(u, w2_ref[lead + (slice(f0, f0 + f_chunk), slice(None))], preferred_element_type=F32)
        out = part if out is None else out + part
        if f0 == 0 and after_first_chunk is not None:
            after_first_chunk()
    return out


def _mixer_residual(x_ref, a_ref, wp_ref, bp_ref, mod_ref):
    y = jnp.dot(a_ref[...], wp_ref[...], preferred_element_type=F32) + bp_ref[...]
    return x_ref[...] + mod_ref[0, 2:3, :] * y


def _ffn_dense_kernel(x_ref, a_ref, wp_ref, bp_ref, mod_ref, nw_ref, w1_ref, w3_ref, w2_ref, o_ref, *, f_chunk):
    x = _mixer_residual(x_ref, a_ref, wp_ref, bp_ref, mod_ref)
    h = _norm_mod(x, nw_ref[...], mod_ref[0, 3:4, :], mod_ref[0, 4:5, :]).astype(BF16)
    f = _swiglu_rows(h, w1_ref, w3_ref, w2_ref, (), f_chunk)
    o_ref[...] = x + mod_ref[0, 5:6, :] * f


def _ffn_dense(xs, a, w_proj, b_proj, mods, nw, w1, w3, w2, tiles):
    m, d = xs.shape
    k = a.shape[1]
    f = w1.shape[1]
    mxu_n = 2 * LANES
    f_chunk = f // 2 if (f // 2) % mxu_n == 0 else f
    assert a.shape[0] == tiles.grid * ROW_TILE
    return pl.pallas_call(
        functools.partial(_ffn_dense_kernel, f_chunk=f_chunk),
        out_shape=jax.ShapeDtypeStruct((m, d), F32),
        grid=(tiles.grid,),
        in_specs=[tiles.x_spec(d), pl.BlockSpec((ROW_TILE, k), lambda i: (i, 0)),
                  _resident((k, d), lambda i: (0, 0)),
                  _resident((1, d), lambda i: (0, 0)),
                  tiles.mod_spec(d),
                  _resident((1, d), lambda i: (0, 0)),
                  _resident((d, f), lambda i: (0, 0)),
                  _resident((d, f), lambda i: (0, 0)),
                  _resident((f, d), lambda i: (0, 0))],
        out_specs=tiles.x_spec(d),
        input_output_aliases={0: 0},
        compiler_params=_cparams(1, 52),
        name="ffn_dense",
    )(xs, a, w_proj, b_proj, mods, nw, w1, w3, w2)


ROW_BLOCKS = 8


def _rows_to_tiles(ref, lead, val):
    rows = val.shape[0]
    for s in range(ROW_BLOCKS):
        ref[lead + (pl.ds(s, rows, stride=ROW_BLOCKS), slice(None))] = val[:, s * LANES:(s + 1) * LANES]


def _tiles_to_rows(ref, lead, rows):
    return jnp.concatenate([ref[lead + (pl.ds(s, rows, stride=ROW_BLOCKS), slice(None))]
                            for s in range(ROW_BLOCKS)], axis=1)


def _moe_router_kernel(x_ref, a_ref, wp_ref, bp_ref, mod_ref, nw_ref, rt_ref, xo_ref, h_ref, idx_ref, gate_ref):
    x = _mixer_residual(x_ref, a_ref, wp_ref, bp_ref, mod_ref)
    xo_ref[...] = x
    h = _norm_mod(x, nw_ref[...], mod_ref[0, 3:4, :], mod_ref[0, 4:5, :])
    _rows_to_tiles(h_ref, (), h)
    logits = lax.dot_general(rt_ref[...], h, (((1,), (1,)), ((), ())),
                             preferred_element_type=F32, precision=HIGHEST)
    e = lax.broadcasted_iota(jnp.int32, logits.shape, 0)
    m1 = jnp.max(logits, axis=0, keepdims=True)
    i1 = jnp.min(jnp.where(logits == m1, e, N_EXPERTS), axis=0, keepdims=True)
    rest = jnp.where(e == i1, -jnp.inf, logits)
    m2 = jnp.max(rest, axis=0, keepdims=True)
    i2 = jnp.min(jnp.where(rest == m2, e, N_EXPERTS), axis=0, keepdims=True)
    t = jnp.exp(m2 - m1)
    idx_ref[...] = jnp.concatenate([i1, i2], axis=0)
    gate_ref[...] = jnp.concatenate([1.0 / (1.0 + t), t / (1.0 + t)], axis=0)


def _moe_router(xs, a, w_proj, b_proj, mods, nw, router_t, tiles):
    m_all, d = xs.shape
    k = a.shape[1]
    m = tiles.grid * ROW_TILE
    assert a.shape[0] == m
    pair = pl.BlockSpec((TOP_K, ROW_TILE), lambda i: (0, i))
    return pl.pallas_call(
        _moe_router_kernel,
        out_shape=(jax.ShapeDtypeStruct((m_all, d), F32),
                   jax.ShapeDtypeStruct((m * ROW_BLOCKS, LANES), F32),
                   jax.ShapeDtypeStruct((TOP_K, m), jnp.int32),
                   jax.ShapeDtypeStruct((TOP_K, m), F32)),
        grid=(tiles.grid,),
        in_specs=[tiles.x_spec(d), pl.BlockSpec((ROW_TILE, k), lambda i: (i, 0)),
                  _resident((k, d), lambda i: (0, 0)),
                  _resident((1, d), lambda i: (0, 0)),
                  tiles.mod_spec(d),
                  _resident((1, d), lambda i: (0, 0)),
                  _resident((N_EXPERTS, d), lambda i: (0, 0))],
        out_specs=(tiles.x_spec(d),
                   pl.BlockSpec((ROW_TILE * ROW_BLOCKS, LANES), lambda i: (i, 0)), pair, pair),
        input_output_aliases={0: 0},
        compiler_params=_cparams(1, 40),
        name="moe_router",
    )(xs, a, w_proj, b_proj, mods, nw, router_t)


def _moe_experts_kernel(te_ref, src_ref, src_next_ref, dst_init_ref, dst_prev_ref, dst_ref, h_hbm, gate_ref,
                        w1_ref, w3_ref, w2_ref, y_hbm, gbuf, obuf, gsem, osem, *, f_chunk):
    j = pl.program_id(0)
    n = pl.num_programs(0)
    slot = j % 2
    other = 1 - slot
    tile = EXPERT_TILE

    def tile_rows(idx):
        return pl.ds(pl.multiple_of(idx, ROW_BLOCKS), ROW_BLOCKS)

    def gather(ref, i, s):
        return pltpu.make_async_copy(h_hbm.at[tile_rows(ref[0, 0, i])],
                                     gbuf.at[s, pl.ds(i * ROW_BLOCKS, ROW_BLOCKS)], gsem.at[s])

    def scatter(ref, i, s):
        return pltpu.make_async_copy(obuf.at[s, pl.ds(i * ROW_BLOCKS, ROW_BLOCKS)],
                                     y_hbm.at[tile_rows(ref[0, 0, i])], osem.at[s])

    def gather_wait(s):
        pltpu.make_async_copy(h_hbm.at[pl.ds(0, tile * ROW_BLOCKS)], gbuf.at[s], gsem.at[s]).wait()

    def scatter_wait(s):
        pltpu.make_async_copy(obuf.at[s], y_hbm.at[pl.ds(0, tile * ROW_BLOCKS)], osem.at[s]).wait()

    @pl.when(j == 0)
    def _():
        obuf[...] = jnp.zeros_like(obuf)

        def body(i, carry):
            gather(src_ref, i, 0).start()
            scatter(dst_init_ref, i, 0).start()
            return carry
        lax.fori_loop(0, tile, body, 0)

    gather_wait(slot)
    x = _tiles_to_rows(gbuf, (slot,), tile).astype(BF16)
    for i in range(tile):
        gather(src_next_ref, i, other).start()
    for i in range(tile):
        scatter(dst_prev_ref, i, other).start()
    f = _swiglu_rows(x, w1_ref, w3_ref, w2_ref, (0,), f_chunk)
    scatter_wait(slot)
    _rows_to_tiles(obuf, (slot,), gate_ref[...] * f)

    @pl.when(j == n - 1)
    def _():
        def body(i, carry):
            scatter(dst_ref, i, slot).start()
            return carry
        lax.fori_loop(0, tile, body, 0)
        gather_wait(other)
        scatter_wait(other)
        scatter_wait(slot)


def _moe_experts(h, tile_expert, src_rows, dst_rows, slot_gate, w1, w3, w2, n_out_rows):
    d = ROW_BLOCKS * LANES
    assert w1.shape[1] == d
    n_tiles = tile_expert.shape[0]
    f = w1.shape[2]
    tile = EXPERT_TILE
    src3 = src_rows.reshape(n_tiles, 1, tile)
    dst3 = dst_rows.reshape(n_tiles + 2, 1, tile)

    def smem(index_map):
        return pl.BlockSpec((1, 1, tile), index_map, memory_space=pltpu.SMEM)

    grid_spec = pltpu.PrefetchScalarGridSpec(
        num_scalar_prefetch=1,
        grid=(n_tiles,),
        in_specs=[
            smem(lambda j, te: (j, 0, 0)),
            smem(lambda j, te: (jnp.minimum(j + 1, n_tiles - 1), 0, 0)),
            smem(lambda j, te: (0, 0, 0)),
            smem(lambda j, te: (j + 1, 0, 0)),
            smem(lambda j, te: (j + 2, 0, 0)),
            pl.BlockSpec(memory_space=pl.ANY),
            pl.BlockSpec((tile, 1), lambda j, te: (j, 0)),
            _resident((1, d, f), lambda j, te: (te[j], 0, 0)),
            _resident((1, d, f), lambda j, te: (te[j], 0, 0)),
            _resident((1, f, d), lambda j, te: (te[j], 0, 0)),
        ],
        out_specs=pl.BlockSpec(memory_space=pl.ANY),
        scratch_shapes=[pltpu.VMEM((2, tile * ROW_BLOCKS, LANES), F32),
                        pltpu.VMEM((2, tile * ROW_BLOCKS, LANES), F32),
                        pltpu.SemaphoreType.DMA((2,)), pltpu.SemaphoreType.DMA((2,))],
    )
    return pl.pallas_call(
        functools.partial(_moe_experts_kernel, f_chunk=f // 2),
        out_shape=jax.ShapeDtypeStruct((n_out_rows * ROW_BLOCKS, LANES), F32),
        grid_spec=grid_spec,
        compiler_params=_cparams(1, 56),
        name="moe_experts",
    )(tile_expert, src3, src3, dst3, dst3, dst3, h, slot_gate, w1, w3, w2)


def _moe_combine_kernel(x_ref, mod_ref, y0_ref, y1_ref, fw_ref, o_ref, *, final):
    y = _tiles_to_rows(y0_ref, (), ROW_TILE) + _tiles_to_rows(y1_ref, (), ROW_TILE)
    x = x_ref[...] + mod_ref[0, 5:6, :] * y
    if final:
        x = x * lax.rsqrt(jnp.mean(x * x, axis=-1, keepdims=True) + EPS) * fw_ref[...]
    o_ref[...] = x


def _moe_combine(xs, mods, y, tiles, final_w=None):
    m, d = xs.shape
    k1 = tiles.grid
    final = final_w is not None
    if final:
        out_shape = jax.ShapeDtypeStruct((tiles.grid * ROW_TILE, d), F32)
        out_spec = pl.BlockSpec((ROW_TILE, d), lambda i: (i, 0))
    else:
        out_shape = jax.ShapeDtypeStruct((m, d), F32)
        out_spec = tiles.x_spec(d)
        final_w = jnp.ones((1, d), F32)
    return pl.pallas_call(
        functools.partial(_moe_combine_kernel, final=final),
        out_shape=out_shape,
        grid=(tiles.grid,),
        in_specs=[tiles.x_spec(d), tiles.mod_spec(d),
                  pl.BlockSpec((ROW_TILE * ROW_BLOCKS, LANES), lambda i: (i, 0)),
                  pl.BlockSpec((ROW_TILE * ROW_BLOCKS, LANES), lambda i: (k1 + i, 0)),
                  pl.BlockSpec((1, d), lambda i: (0, 0))],
        out_specs=out_spec,
        input_output_aliases={} if final else {0: 0},
        compiler_params=_cparams(1, 32),
        name="moe_combine",
    )(xs, mods, y, y, final_w)


def _moe_layer(xs, a, w_proj, b_proj, mods, nw, router, w1, w3, w2, tiles, final_w=None):
    xs, h, top_i, gates = _moe_router(xs, a, w_proj, b_proj, mods, nw, router.T.astype(F32), tiles)

    tile = EXPERT_TILE
    m = tiles.grid * ROW_TILE
    n_assign = TOP_K * m
    eid = top_i.reshape(-1)
    onehot = (eid[:, None] == jnp.arange(N_EXPERTS)[None, :]).astype(jnp.int32)
    csum = jnp.cumsum(onehot, axis=0)
    counts = csum[-1]
    rank = jnp.take_along_axis(csum, eid[:, None], axis=1)[:, 0] - 1
    padded = ((counts + tile - 1) // tile) * tile
    ends = jnp.cumsum(padded)
    starts = ends - padded
    n_slots = n_assign + N_EXPERTS * tile
    n_tiles = n_slots // tile
    pos = starts[eid] + rank
    slot_assign = jnp.zeros((n_slots,), jnp.int32).at[pos].set(jnp.arange(1, n_assign + 1, dtype=jnp.int32))
    valid = slot_assign > 0
    assign = jnp.maximum(slot_assign - 1, 0)
    src_rows = jnp.where(valid, jnp.where(assign >= m, assign - m, assign), 0)
    spare = n_assign + jnp.cumsum(jnp.logical_not(valid).astype(jnp.int32)) - 1
    dst_rows = jnp.where(valid, assign, spare)
    slot_gate = jnp.where(valid, gates.reshape(-1)[assign], 0.0)
    n_out_rows = n_slots + 2 * tile
    prime_rows = n_slots + jnp.arange(2 * tile, dtype=jnp.int32)
    tile_start = jnp.arange(n_tiles, dtype=jnp.int32) * tile
    tile_expert = jnp.sum(jnp.minimum(tile_start, ends[-1] - 1)[:, None] >= ends[None, :], axis=1)
    tile_expert = jnp.clip(tile_expert, 0, N_EXPERTS - 1).astype(jnp.int32)

    y = _moe_experts(h, tile_expert, src_rows.astype(jnp.int32) * ROW_BLOCKS,
                     jnp.concatenate([prime_rows, dst_rows.astype(jnp.int32)]) * ROW_BLOCKS,
                     slot_gate.reshape(-1, 1), w1, w3, w2, n_out_rows)
    return _moe_combine(xs, mods, y, tiles, final_w)


def kernel(x, c, ctx, c_ctx, w_mod, b_mod, norm1_w, norm2_w, ssd_w_in, ssd_conv_w, ssd_conv_b, ssd_dt_bias, ssd_a_log, ssd_d, ssd_norm_w, ssd_w_out, attn_w_qkv, attn_b_qkv, attn_sink, attn_w_o, attn_b_o, ffn_w1, ffn_w3, ffn_w2, moe_router, moe_w1, moe_w3, moe_w2, final_norm_w):
    batch, seq, d = x.shape
    n_ctx = ctx.shape[1]
    t_total = n_ctx + seq
    depth = w_mod.shape[0]
    assert seq % GRID_W == 0 and seq % BLOCK == 0 and n_ctx % ROW_TILE == 0 and depth % 2 == 0

    all_tiles = _Tiles(batch, t_total, n_ctx, 0)
    lat_tiles = _Tiles(batch, t_total, n_ctx, n_ctx // ROW_TILE)

    xs = jnp.concatenate([ctx, x], axis=1).reshape(batch * t_total, d)
    mods_all = _mod_all(c, c_ctx, w_mod, b_mod)
    cos, sa, sb = _rope_tables(seq, n_ctx)
    q_w = Q_HEADS * HEAD_DIM
    q_scale = jnp.concatenate([jnp.full((q_w,), 1.0 / math.sqrt(HEAD_DIM), F32),
                               jnp.ones((attn_w_qkv.shape[2] - q_w,), F32)])

    for i in range(depth):
        last = i == depth - 1
        j = i // 2
        mods = mods_all[i]
        nw1 = norm1_w[i].reshape(1, d)
        nw2 = norm2_w[i].reshape(1, d)
        upd = lat_tiles if last else all_tiles
        if i % 2 == 0:
            g = _ssd_mixer(xs, mods, nw1, ssd_w_in[j], ssd_conv_w[j], ssd_conv_b[j], ssd_dt_bias[j],
                           ssd_a_log[j], ssd_d[j], ssd_norm_w[j], all_tiles, batch, t_total, n_ctx)
            xs = _ffn_dense(xs, g, ssd_w_out[j].astype(BF16), jnp.zeros((1, d), F32), mods, nw2,
                            ffn_w1[j].astype(BF16), ffn_w3[j].astype(BF16), ffn_w2[j].astype(BF16), all_tiles)
        else:
            qt, k, vt = _qkv_rope(xs, mods, nw1, (attn_w_qkv[j] * q_scale).astype(BF16),
                                  (attn_b_qkv[j] * q_scale).reshape(1, -1), cos, sa, sb, all_tiles,
                                  batch, t_total)
            o = _attention(qt, k, vt, attn_sink[j], batch, t_total, n_ctx, not last)
            xs = _moe_layer(xs, o.reshape(-1, q_w), attn_w_o[j].astype(BF16), attn_b_o[j].reshape(1, d),
                            mods, nw2, moe_router[j], moe_w1[j].astype(BF16), moe_w3[j].astype(BF16),
                            moe_w2[j].astype(BF16), upd,
                            final_w=final_norm_w.reshape(1, d) if last else None)
    return xs.reshape(batch, seq, d)
```
